```python
import math
import jax, jax.numpy as jnp
from jax import lax
import numpy as np

D_MODEL = 2048
BATCH = 4
SEQ = 2048
DEPTH = 4
DEC_BATCH = 128
DEC_SEQ = 4
PAST_LEN = 16384
PAGE_SIZE = 128

N_META = 16
CONV_K = 4
CHUNK = 64
MIX_W = D_MODEL
GROUP_W = MIX_W // 4
N_HEADS_GRP = 4
HEAD_V = GROUP_W // N_HEADS_GRP
DN_DK = HEAD_V
GLA_DK = HEAD_V // 2
RET_DK = HEAD_V // 2
GLA_RANK = 16
GLA_TAU = 16
LRU_BLOCKS = N_HEADS_GRP
LRU_C = 8
ROPE_BASE = 10000.0
EPS = 1e-6
DN_QKV_W = 3 * N_HEADS_GRP * DN_DK
IN_SIZES = (DN_QKV_W, N_HEADS_GRP, N_HEADS_GRP, GROUP_W,
            N_HEADS_GRP * GLA_DK, N_HEADS_GRP * GLA_DK, GROUP_W, GLA_RANK,
            N_HEADS_GRP * RET_DK, N_HEADS_GRP * RET_DK, GROUP_W, MIX_W)
IN_W = sum(IN_SIZES)

kernel_name = 'hymba_delta_lru_gla_retention_step'


def rms_norm(x, w=None):
    xf = x.astype(jnp.float32)
    y = xf * lax.rsqrt(jnp.mean(xf * xf, axis=-1, keepdims=True) + EPS)
    if w is not None:
        y = y * w.astype(jnp.float32)
    return y.astype(x.dtype)


def l2_normalize(x):
    return x * lax.rsqrt(jnp.sum(x * x, axis=-1, keepdims=True) + EPS)


def split_last(x, sizes):
    outs, off = [], 0
    for s in sizes:
        outs.append(x[..., off:off + s])
        off += s
    return outs


def causal_conv(x, buf, w):
    xp = jnp.concatenate([buf, x], axis=1)
    y = lax.conv_general_dilated(xp, w[:, None, :], window_strides=(1,), padding='VALID',
                                 dimension_numbers=('NWC', 'WIO', 'NWC'),
                                 feature_group_count=x.shape[-1])
    return y, xp[:, -(CONV_K - 1):]


def rotary(x, pos):
    half = x.shape[-1] // 2
    freqs = ROPE_BASE ** (-jnp.arange(half, dtype=jnp.float32) / half)
    ang = pos.astype(jnp.float32)[:, None] * freqs
    cos, sin = jnp.cos(ang)[None, :, None, :], jnp.sin(ang)[None, :, None, :]
    x1, x2 = x[..., :half], x[..., half:]
    return jnp.concatenate([x1 * cos - x2 * sin, x2 * cos + x1 * sin], axis=-1)


def to_chunks(x, c):
    b, l = x.shape[:2]
    return jnp.moveaxis(x.reshape(b, l // c, c, *x.shape[2:]), 1, 0)


def from_chunks(y):
    y = jnp.moveaxis(y, 0, 1)
    return y.reshape(y.shape[0], -1, *y.shape[3:])


def run_pieces(scan_fn, arrays, s0, pieces):
    outs, s = [], s0
    for start, stop, c in pieces:
        o, s = scan_fn(*[a[:, start:stop] for a in arrays], s, c)
        outs.append(o)
    return jnp.concatenate(outs, axis=1), s


def delta_chunk_scan(q, k, v, beta, g, s0, c):
    dv = v.shape[-1]
    incl = jnp.tril(jnp.ones((c, c), dtype=bool))
    strict = jnp.tril(jnp.ones((c, c), dtype=bool), -1)
    eye = jnp.eye(c, dtype=jnp.float32)

    def step(s, inp):
        qc, kc, vc, bc, gc = inp
        gcum = jnp.moveaxis(jnp.cumsum(gc, axis=1), 1, 2)
        decay = jnp.exp(jnp.where(incl, gcum[..., :, None] - gcum[..., None, :], -jnp.inf))
        kb = kc * bc[..., None]
        a = jnp.where(strict, jnp.einsum('bihk,bjhk->bhij', kb, kc) * decay, 0.0)
        rhs = jnp.concatenate([jnp.moveaxis(vc * bc[..., None], 1, 2),
                               jnp.moveaxis(kb, 1, 2) * jnp.exp(gcum)[..., None]], axis=-1)
        sol = lax.linalg.triangular_solve(a + eye, rhs, left_side=True, lower=True)
        u, w = sol[..., :dv], sol[..., dv:]
        v_new = u - jnp.einsum('bhik,bhkv->bhiv', w, s)
        attn = jnp.einsum('bihk,bjhk->bhij', qc, kc) * decay
        o = (jnp.einsum('bihk,bhi,bhkv->bihv', qc, jnp.exp(gcum), s)
             + jnp.einsum('bhij,bhjv->bihv', attn, v_new))
        g_last = gcum[..., -1]
        s_new = (s * jnp.exp(g_last)[..., None, None]
                 + jnp.einsum('bjhk,bhj,bhjv->bhkv', kc, jnp.exp(g_last[..., None] - gcum), v_new))
        return s_new, o

    s, o = lax.scan(step, s0, tuple(to_chunks(t, c) for t in (q, k, v, beta, g)))
    return from_chunks(o), s


def gla_chunk_scan(q, k, v, lg, s0, c):
    incl = jnp.tril(jnp.ones((c, c), dtype=bool))

    def step(s, inp):
        qc, kc, vc, lc = inp
        b = jnp.cumsum(lc, axis=1)
        rel = jnp.exp(jnp.where(incl[None, :, :, None, None], b[:, :, None] - b[:, None, :], -jnp.inf))
        attn = jnp.einsum('bihk,bjhk,bijhk->bhij', qc, kc, rel)
        o = (jnp.einsum('bihk,bhkv->bihv', qc * jnp.exp(b), s)
             + jnp.einsum('bhij,bjhv->bihv', attn, vc))
        b_last = b[:, -1]
        s_new = (s * jnp.exp(b_last)[..., None]
                 + jnp.einsum('bjhk,bjhv->bhkv', kc * jnp.exp(b_last[:, None] - b), vc))
        return s_new, o

    s, o = lax.scan(step, s0, tuple(to_chunks(t, c) for t in (q, k, v, lg)))
    return from_chunks(o), s


def retention_chunk_scan(q, k, v, s0, c, log_gamma):
    pos = jnp.arange(c, dtype=jnp.float32)
    rel = pos[:, None] - pos[None, :]
    intra = jnp.where(rel >= 0, jnp.exp(log_gamma[:, None, None] * jnp.maximum(rel, 0.0)), 0.0)
    from_state = jnp.exp(log_gamma[:, None] * (pos + 1.0))
    to_state = jnp.exp(log_gamma[:, None] * (c - 1.0 - pos))
    chunk_decay = jnp.exp(log_gamma * c)

    def step(s, inp):
        qc, kc, vc = inp
        attn = jnp.einsum('bihk,bjhk->bhij', qc, kc) * intra
        o = (jnp.einsum('bihk,hi,bhkv->bihv', qc, from_state, s)
             + jnp.einsum('bhij,bjhv->bihv', attn, vc))
        s_new = s * chunk_decay[:, None, None] + jnp.einsum('bjhk,hj,bjhv->bhkv', kc, to_state, vc)
        return s_new, o

    s, o = lax.scan(step, s0, tuple(to_chunks(t, c) for t in (q, k, v)))
    return from_chunks(o), s


def rg_lru(x, h0, wa, ba, wx, bx, lam):
    bsz, l, w = x.shape
    xb = x.reshape(bsz, l, LRU_BLOCKS, w // LRU_BLOCKS)
    r = jax.nn.sigmoid(jnp.einsum('blnc,ncd->blnd', xb, wa).reshape(bsz, l, w) + ba)
    i = jax.nn.sigmoid(jnp.einsum('blnc,ncd->blnd', xb, wx).reshape(bsz, l, w) + bx)
    log_a = -LRU_C * r * jax.nn.softplus(-lam)
    a = jnp.exp(log_a)
    b = jnp.sqrt(-jnp.expm1(2.0 * log_a)) * (i * x)
    b = b.at[:, 0].add(a[:, 0] * h0)

    def combine(left, right):
        return (left[0] * right[0], right[0] * left[1] + right[1])

    _, h = lax.associative_scan(combine, (a, b), axis=1)
    return h, h[:, -1]


def hybrid_layer(x, pos, pieces, s_delta, buf_delta, h_lru, buf_lru, s_gla, s_ret, lp):
    f32 = jnp.float32
    bsz, l, _ = x.shape
    nh = N_HEADS_GRP
    proj = jnp.einsum('bld,de->ble', rms_norm(x, lp['norm']), lp['w_in']).astype(f32)
    (qkv_a, alpha_a, beta_a, x_b, q_c, k_c, v_c, r_c, q_d, k_d, v_d, gate) = split_last(proj, IN_SIZES)

    qkv_a, new_buf_delta = causal_conv(qkv_a, buf_delta.astype(f32), lp['conv_a'].astype(f32))
    qkv_a = jax.nn.silu(qkv_a).reshape(bsz, l, 3, nh, DN_DK)
    q_a = l2_normalize(qkv_a[:, :, 0]) * (DN_DK ** -0.5)
    k_a = l2_normalize(qkv_a[:, :, 1])
    v_a = qkv_a[:, :, 2]
    beta = jax.nn.sigmoid(beta_a)
    g = -jnp.exp(lp['a_log'].astype(f32)) * jax.nn.softplus(alpha_a + lp['dt_bias'].astype(f32))
    o_a, new_s_delta = run_pieces(delta_chunk_scan, (q_a, k_a, v_a, beta, g), s_delta.astype(f32), pieces)
    o_a = rms_norm(o_a, lp['norm_a']).reshape(bsz, l, GROUP_W)

    x_b, new_buf_lru = causal_conv(x_b, buf_lru.astype(f32), lp['conv_b'].astype(f32))
    x_b = x_b + lp['conv_b_bias'].astype(f32)
    o_b, new_h_lru = rg_lru(x_b, h_lru.astype(f32), lp['lru_wa'].astype(f32), lp['lru_ba'].astype(f32),
                           lp['lru_wx'].astype(f32), lp['lru_bx'].astype(f32), lp['lru_lambda'].astype(f32))

    q_c = q_c.reshape(bsz, l, nh, GLA_DK) * (GLA_DK ** -0.5)
    k_c = k_c.reshape(bsz, l, nh, GLA_DK)
    v_c = v_c.reshape(bsz, l, nh, HEAD_V)
    lg_c = jax.nn.log_sigmoid(r_c @ lp['gla_w2'].astype(f32) + lp['gla_b2'].astype(f32))
    lg_c = lg_c.reshape(bsz, l, nh, GLA_DK) / GLA_TAU
    o_c, new_s_gla = run_pieces(gla_chunk_scan, (q_c, k_c, v_c, lg_c), s_gla.astype(f32), pieces)
    o_c = rms_norm(o_c, lp['norm_c']).reshape(bsz, l, GROUP_W)

    q_d = rotary(q_d.reshape(bsz, l, nh, RET_DK), pos)
    k_d = rotary(k_d.reshape(bsz, l, nh, RET_DK), pos) * (RET_DK ** -0.5)
    v_d = v_d.reshape(bsz, l, nh, HEAD_V)
    log_gamma = jnp.log(1.0 - 2.0 ** (-5.0 - jnp.arange(nh, dtype=f32)))
    ret_fn = lambda q, k, v, s, c: retention_chunk_scan(q, k, v, s, c, log_gamma)
    o_d, new_s_ret = run_pieces(ret_fn, (q_d, k_d, v_d), s_ret.astype(f32), pieces)
    o_d = rms_norm(o_d).reshape(bsz, l, GROUP_W)

    mixed = jnp.concatenate([o_a, o_b, o_c, o_d], axis=-1) * jax.nn.silu(gate)
    y = x + jnp.einsum('ble,ed->bld', mixed.astype(x.dtype), lp['w_out'])
    return y, (new_s_delta, new_buf_delta, new_h_lru, new_buf_lru, new_s_gla, new_s_ret)


def setup_inputs(seed: int = 0) -> dict:
    key = jax.random.key(seed)
    ks = jax.random.split(key, 32)
    f32 = jnp.float32
    nh = N_HEADS_GRP
    bw = GROUP_W // LRU_BLOCKS

    def nrm(k, shape, s):
        return s * jax.random.normal(k, shape, f32)

    a_log = jnp.log(jax.random.uniform(ks[10], (DEPTH, nh), f32, 1.0, 16.0))
    dt = jnp.exp(jax.random.uniform(ks[11], (DEPTH, nh), f32, math.log(1e-3), math.log(0.1)))
    dt_bias = dt + jnp.log(-jnp.expm1(-dt))
    a0 = jax.random.uniform(ks[12], (DEPTH, GROUP_W), f32, 0.9, 0.999) ** (1.0 / LRU_C)
    lru_lambda = jnp.log(a0) - jnp.log1p(-a0)
    return {
        'x_prompt': nrm(ks[0], (BATCH, SEQ, D_MODEL), 1.0),
        'x_sample': nrm(ks[1], (DEC_BATCH, DEC_SEQ, D_MODEL), 1.0),
        'state_delta': nrm(ks[2], (DEPTH, DEC_BATCH, nh, DN_DK, HEAD_V), 0.1),
        'state_delta_conv': nrm(ks[3], (DEPTH, DEC_BATCH, CONV_K - 1, DN_QKV_W), 1.0),
        'state_lru': nrm(ks[4], (DEPTH, DEC_BATCH, GROUP_W), 0.5),
        'state_lru_conv': nrm(ks[5], (DEPTH, DEC_BATCH, CONV_K - 1, GROUP_W), 1.0),
        'state_gla': nrm(ks[6], (DEPTH, DEC_BATCH, nh, GLA_DK, HEAD_V), 0.5),
        'state_ret': nrm(ks[7], (DEPTH, DEC_BATCH, nh, RET_DK, HEAD_V), 0.5),
        'meta_tokens': nrm(ks[8], (N_META, D_MODEL), 1.0),
        'norm_w': 1.0 + nrm(ks[9], (DEPTH, D_MODEL), 0.01),
        'w_in': nrm(ks[13], (DEPTH, D_MODEL, IN_W), D_MODEL ** -0.5),
        'conv_a': nrm(ks[14], (DEPTH, CONV_K, DN_QKV_W), CONV_K ** -0.5),
        'a_log': a_log,
        'dt_bias': dt_bias,
        'norm_a': 1.0 + nrm(ks[15], (DEPTH, HEAD_V), 0.01),
        'conv_b': nrm(ks[16], (DEPTH, CONV_K, GROUP_W), CONV_K ** -0.5),
        'conv_b_bias': nrm(ks[17], (DEPTH, GROUP_W), 0.01),
        'lru_wa': nrm(ks[18], (DEPTH, LRU_BLOCKS, bw, bw), bw ** -0.5),
        'lru_ba': nrm(ks[19], (DEPTH, GROUP_W), 0.01),
        'lru_wx': nrm(ks[20], (DEPTH, LRU_BLOCKS, bw, bw), bw ** -0.5),
        'lru_bx': nrm(ks[21], (DEPTH, GROUP_W), 0.01),
        'lru_lambda': lru_lambda,
        'gla_w2': nrm(ks[22], (DEPTH, GLA_RANK, nh * GLA_DK), GLA_RANK ** -0.5),
        'gla_b2': nrm(ks[23], (DEPTH, nh * GLA_DK), 0.01),
        'norm_c': 1.0 + nrm(ks[24], (DEPTH, HEAD_V), 0.01),
        'w_out': nrm(ks[25], (DEPTH, MIX_W, D_MODEL), 0.5 * MIX_W ** -0.5),
        'final_norm': 1.0 + nrm(ks[26], (D_MODEL,), 0.01),
    }


def reference(x_prompt, x_sample, state_delta, state_delta_conv, state_lru, state_lru_conv, state_gla,
              state_ret, meta_tokens, norm_w, w_in, conv_a, a_log, dt_bias, norm_a, conv_b, conv_b_bias,
              lru_wa, lru_ba, lru_wx, lru_bx, lru_lambda, gla_w2, gla_b2, norm_c, w_out, final_norm):
    f32 = jnp.float32
    nh = N_HEADS_GRP
    bp, lp_len = x_prompt.shape[0], x_prompt.shape[1]
    ls = x_sample.shape[1]
    meta = jnp.broadcast_to(meta_tokens[None].astype(x_prompt.dtype), (bp, N_META, D_MODEL))
    hp = jnp.concatenate([meta, x_prompt], axis=1)
    hs = x_sample
    pos_p = jnp.arange(N_META + lp_len)
    pos_s = PAST_LEN + jnp.arange(ls)
    pieces_p = ((0, N_META, N_META), (N_META, N_META + lp_len, math.gcd(lp_len, CHUNK)))
    pieces_s = ((0, ls, math.gcd(ls, CHUNK)),)

    p_states, s_states = [], []
    for layer in range(DEPTH):
        lp = {'norm': norm_w[layer], 'w_in': w_in[layer], 'conv_a': conv_a[layer], 'a_log': a_log[layer],
              'dt_bias': dt_bias[layer], 'norm_a': norm_a[layer], 'conv_b': conv_b[layer],
              'conv_b_bias': conv_b_bias[layer], 'lru_wa': lru_wa[layer], 'lru_ba': lru_ba[layer],
              'lru_wx': lru_wx[layer], 'lru_bx': lru_bx[layer], 'lru_lambda': lru_lambda[layer],
              'gla_w2': gla_w2[layer], 'gla_b2': gla_b2[layer], 'norm_c': norm_c[layer], 'w_out': w_out[layer]}
        hp, st_p = hybrid_layer(
            hp, pos_p, pieces_p,
            jnp.zeros((bp, nh, DN_DK, HEAD_V), f32), jnp.zeros((bp, CONV_K - 1, DN_QKV_W), f32),
            jnp.zeros((bp, GROUP_W), f32), jnp.zeros((bp, CONV_K - 1, GROUP_W), f32),
            jnp.zeros((bp, nh, GLA_DK, HEAD_V), f32), jnp.zeros((bp, nh, RET_DK, HEAD_V), f32), lp)
        hs, st_s = hybrid_layer(
            hs, pos_s, pieces_s, state_delta[layer], state_delta_conv[layer], state_lru[layer],
            state_lru_conv[layer], state_gla[layer], state_ret[layer], lp)
        p_states.append(st_p)
        s_states.append(st_s)

    p_delta, p_delta_conv, p_lru, p_lru_conv, p_gla, p_ret = [
        jnp.stack([st[i] for st in p_states]) for i in range(6)]
    s_delta, s_delta_conv, s_lru, s_lru_conv, s_gla, s_ret = [
        jnp.stack([st[i] for st in s_states]) for i in range(6)]
    y_prompt = rms_norm(hp, final_norm)[:, N_META:]
    y_sample = rms_norm(hs, final_norm)
    return (y_prompt, y_sample, p_delta, p_delta_conv, p_lru, p_lru_conv, p_gla, p_ret,
            s_delta, s_delta_conv, s_lru, s_lru_conv, s_gla, s_ret)
```

```python
import functools
import math

import jax
import jax.numpy as jnp
from jax import lax
from jax.experimental import pallas as pl
from jax.experimental.pallas import tpu as pltpu

f32 = jnp.float32
bf16 = jnp.bfloat16

D_MODEL = 2048
N_META = 16
CONV_K = 4
CHUNK = 64
N_HEADS = 4
HEAD_V = 128
GROUP_W = N_HEADS * HEAD_V
DN_DK = 128
GLA_DK = 64
RET_DK = 64
GLA_RANK = 16
GLA_TAU = 16.0
LRU_C = 8.0
ROPE_BASE = 10000.0
EPS = 1e-6
PAST_LEN = 16384
QKV_W = 3 * N_HEADS * DN_DK

OFF_QKV = 0
OFF_XB = 1536
OFF_QC = 2048
OFF_KC = 2304
OFF_VC = 2560
OFF_QD = 3072
OFF_KD = 3328
OFF_VD = 3584
OFF_GATE = 4096
OFF_SMALL = 6144
N_PACK = 6400
SM_ALPHA = 0
SM_BETA = 4
SM_RC = 8

TILE = 8
TILE_OFF = CONV_K - 1

VMEM_LIMIT = 52 * 1024 * 1024

NN = (((1,), (0,)), ((), ()))
NT = (((1,), (1,)), ((), ()))
TN = (((0,), (0,)), ((), ()))


def _split(a):
    hi = a.astype(bf16)
    lo = (a - hi.astype(f32)).astype(bf16)
    return hi, lo


def _mm(a, b, dims=NN, passes=3):
    if passes == 6:
        return lax.dot_general(a, b, dims, precision=lax.Precision.HIGHEST, preferred_element_type=f32)
    if passes == 1:
        return lax.dot_general(a.astype(bf16), b.astype(bf16), dims, preferred_element_type=f32)
    ah, al = _split(a)
    bh, bl = _split(b)
    d = lambda x, y: lax.dot_general(x, y, dims, preferred_element_type=f32)
    return d(ah, bh) + (d(ah, bl) + d(al, bh))


def _softplus(x):
    return jnp.maximum(x, 0.0) + jnp.log1p(jnp.exp(-jnp.abs(x)))


def _sigmoid(x):
    return 1.0 / (1.0 + jnp.exp(-x))


def _silu(x):
    return x * _sigmoid(x)


def _rms(x):
    return x * lax.rsqrt(jnp.mean(x * x, axis=-1, keepdims=True) + EPS)


def _inproj_kernel(x_ref, nw_ref, w_ref, o_ref, xn_ref):
    @pl.when(pl.program_id(1) == 0)
    def _():
        xn_ref[...] = (_rms(x_ref[...]) * nw_ref[...]).astype(bf16)

    o_ref[...] = jnp.dot(xn_ref[...], w_ref[...], preferred_element_type=f32)


def _inproj(x, nw, w, tm, tn):
    m = x.shape[0]
    return pl.pallas_call(
        _inproj_kernel,
        grid=(pl.cdiv(m, tm), N_PACK // tn),
        in_specs=[
            pl.BlockSpec((tm, D_MODEL), lambda i, j: (i, 0)),
            pl.BlockSpec((1, D_MODEL), lambda i, j: (0, 0)),
            pl.BlockSpec((D_MODEL, tn), lambda i, j: (0, j)),
        ],
        out_specs=pl.BlockSpec((tm, tn), lambda i, j: (i, j)),
        out_shape=jax.ShapeDtypeStruct((m, N_PACK), f32),
        scratch_shapes=[pltpu.VMEM((tm, D_MODEL), bf16)],
        compiler_params=pltpu.CompilerParams(
            dimension_semantics=("arbitrary", "arbitrary"), vmem_limit_bytes=VMEM_LIMIT),
        name="inproj",
    )(x, nw, w)


def _outproj_kernel(m_ref, w_ref, x_ref, o_ref):
    o_ref[...] = x_ref[...] + jnp.dot(m_ref[...], w_ref[...], preferred_element_type=f32)


def _outproj(mixed, w, x, tm):
    m = x.shape[0]
    return pl.pallas_call(
        _outproj_kernel,
        grid=(pl.cdiv(m, tm),),
        in_specs=[
            pl.BlockSpec((tm, D_MODEL), lambda i: (i, 0)),
            pl.BlockSpec((D_MODEL, D_MODEL), lambda i: (0, 0)),
            pl.BlockSpec((tm, D_MODEL), lambda i: (i, 0)),
        ],
        out_specs=pl.BlockSpec((tm, D_MODEL), lambda i: (i, 0)),
        out_shape=jax.ShapeDtypeStruct((m, D_MODEL), f32),
        compiler_params=pltpu.CompilerParams(
            dimension_semantics=("arbitrary",), vmem_limit_bytes=VMEM_LIMIT),
        name="outproj",
    )(mixed, w, x)


def _final_norm_kernel(x_ref, w_ref, o_ref):
    o_ref[...] = _rms(x_ref[...]) * w_ref[...]


def _final_norm(x, w, tm):
    m = x.shape[0]
    return pl.pallas_call(
        _final_norm_kernel,
        grid=(pl.cdiv(m, tm),),
        in_specs=[pl.BlockSpec((tm, D_MODEL), lambda i: (i, 0)),
                  pl.BlockSpec((1, D_MODEL), lambda i: (0, 0))],
        out_specs=pl.BlockSpec((tm, D_MODEL), lambda i: (i, 0)),
        out_shape=jax.ShapeDtypeStruct((m, D_MODEL), f32),
        compiler_params=pltpu.CompilerParams(dimension_semantics=("arbitrary",)),
        name="final_norm",
    )(x, w)


class _Cfg:
    def __init__(self, rows, seg, voff, vlen, nc, groups, embedded, row_block_off):
        self.rows, self.seg, self.voff, self.vlen = rows, seg, voff, vlen
        self.nc, self.groups, self.embedded, self.row_block_off = nc, groups, embedded, row_block_off
        self.nseg = rows // seg
        self.all_valid = (voff == 0 and vlen == seg)


def _tri_inverse(a, seg, row, col, eye):
    def blk(s):
        return (row // s) == (col // s)

    n = -jnp.where(blk(8), a, 0.0)
    n2 = _mm(n, n)
    n4 = _mm(n2, n2)
    t = _mm(_mm(eye + n, eye + n2), eye + n4)
    s = 8
    while s < seg:
        off = jnp.where(blk(2 * s) & jnp.logical_not(blk(s)), a, 0.0)
        t = t - _mm(_mm(t, off), t)
        s *= 2
    return t


def _mixer_kernel(cfg, *refs):
    R, SEG, NSEG = cfg.rows, cfg.seg, cfg.nseg
    it = iter(refs)
    proj = next(it)
    if cfg.embedded:
        buf_a, buf_b = next(it), next(it)
    else:
        ic_a, ic_b = next(it), next(it)
    sd_in, h_in, sg_in, sr_in = next(it), next(it), next(it), next(it)
    cos_t, sin_t, intra_t, fs_t, ts_t, cd_t = (next(it) for _ in range(6))
    (conv_a, alogv, dtbv, norm_a, conv_b, cbb, wa, wx, ba, bx, lam, w2p, b2, norm_c) = (
        next(it) for _ in range(14))
    mixed, sd, conv_a_o, h_o, conv_b_o, sg, sr = (next(it) for _ in range(7))
    xp_a, xp_b, l_a, l_b, l_o = (next(it) for _ in range(5))

    c = pl.program_id(1)

    @pl.when(c == 0)
    def _init():
        sd[...] = sd_in[...]
        h_o[...] = h_in[...]
        sg[...] = sg_in[...]
        sr[...] = sr_in[...]
        if cfg.embedded:
            xp_a[0:8, :] = jnp.zeros((8, QKV_W), f32)
            xp_b[0:8, :] = jnp.zeros((8, GROUP_W), f32)
        else:
            xp_a[0:8, :] = ic_a[...]
            xp_b[0:8, :] = ic_b[...]
        xp_a[8 + R:16 + R, :] = jnp.zeros((8, QKV_W), f32)
        xp_b[8 + R:16 + R, :] = jnp.zeros((8, GROUP_W), f32)

    row = lax.broadcasted_iota(jnp.int32, (R, R), 0)
    col = lax.broadcasted_iota(jnp.int32, (R, R), 1)
    same = (row // SEG) == (col // SEG)
    incl = same & (col <= row)
    strict = same & (col < row)
    eye = (row == col).astype(f32)
    l_incl = incl.astype(f32)
    m_same = same.astype(f32)
    rmod = lax.broadcasted_iota(jnp.int32, (R, 1), 0) % SEG
    valid = (rmod >= cfg.voff) & (rmod < cfg.voff + cfg.vlen)
    is_hist = rmod < TILE_OFF

    def conv(xp, x, buf, w_ref, width):
        if cfg.embedded:
            x = jnp.where(is_hist, buf[...], x)
        xp[8:8 + R, :] = x
        y = w_ref[CONV_K - 1:CONV_K, :] * x
        for s in range(1, CONV_K):
            y = y + w_ref[CONV_K - 1 - s:CONV_K - s, :] * xp[8 - s:8 - s + R, :]
        return y

    ya = conv(xp_a, proj[:, OFF_QKV:OFF_QKV + QKV_W], buf_a if cfg.embedded else None, conv_a, QKV_W)
    yb = conv(xp_b, proj[:, OFF_XB:OFF_XB + GROUP_W], buf_b if cfg.embedded else None, conv_b, GROUP_W)
    yb = yb + cbb[...]
    if cfg.embedded:
        conv_a_o[...] = xp_a[8 + TILE_OFF + 1:8 + TILE_OFF + 1 + R, :]
        conv_b_o[...] = xp_b[8 + TILE_OFF + 1:8 + TILE_OFF + 1 + R, :]
    else:
        @pl.when(c == cfg.nc - 1)
        def _():
            conv_a_o[...] = xp_a[8 + R - (CONV_K - 1):8 + R, :]
            conv_b_o[...] = xp_b[8 + R - (CONV_K - 1):8 + R, :]
        xp_a[0:8, :] = xp_a[R:R + 8, :]
        xp_b[0:8, :] = xp_b[R:R + 8, :]

    small = proj[:, OFF_SMALL:OFF_SMALL + 128]
    gate = proj[:, OFF_GATE:OFF_GATE + D_MODEL]

    def put(k, o):
        g = gate[:, k * GROUP_W:(k + 1) * GROUP_W]
        mixed[:, k * GROUP_W:(k + 1) * GROUP_W] = (o * _silu(g)).astype(mixed.dtype)

    r_pre = jnp.concatenate(
        [_mm(yb[:, n * 128:(n + 1) * 128], wa[n]) for n in range(N_HEADS)], axis=1) + ba[...]
    i_pre = jnp.concatenate(
        [_mm(yb[:, n * 128:(n + 1) * 128], wx[n]) for n in range(N_HEADS)], axis=1) + bx[...]
    log_a = -LRU_C * _sigmoid(r_pre) * _softplus(-lam[...])
    a_t = jnp.exp(log_a)
    b_t = jnp.sqrt(-jnp.tanh(log_a) * (a_t * a_t + 1.0)) * (_sigmoid(i_pre) * yb)
    h_all = h_o[...]
    h_new, o_b = [], []
    for n in range(N_HEADS):
        ls = slice(n * 128, (n + 1) * 128)
        l_a[n] = a_t[:, ls]
        l_b[n] = b_t[:, ls]
        if not cfg.all_valid:
            l_o[n] = jnp.zeros((R, 128), f32)
        h = h_all[:, ls]
        for t in range(cfg.vlen):
            if NSEG == 1:
                idx = pl.ds(cfg.voff + t, 1)
            else:
                idx = pl.ds(cfg.voff + t, NSEG, stride=SEG)
            h = l_a[n, idx, :] * h + l_b[n, idx, :]
            l_o[n, idx, :] = h
        h_new.append(h)
        o_b.append(l_o[n])
    h_o[...] = jnp.concatenate(h_new, axis=1)
    put(1, jnp.concatenate(o_b, axis=1))

    qkv = _silu(ya)
    g_all = -jnp.exp(alogv[...]) * _softplus(small + dtbv[...])
    beta_all = _sigmoid(small)
    if not cfg.all_valid:
        g_all = jnp.where(valid, g_all, 0.0)
        beta_all = jnp.where(valid, beta_all, 0.0)
    gcum = _mm(l_incl, g_all, passes=6)
    gtot = _mm(m_same, g_all, passes=6)
    gcum_t = gcum.T
    outs = []
    for hd in range(N_HEADS):
        q = qkv[:, hd * 128:(hd + 1) * 128]
        k = qkv[:, 512 + hd * 128:512 + (hd + 1) * 128]
        v = qkv[:, 1024 + hd * 128:1024 + (hd + 1) * 128]
        q = q * lax.rsqrt(jnp.sum(q * q, axis=-1, keepdims=True) + EPS) * (DN_DK ** -0.5)
        k = k * lax.rsqrt(jnp.sum(k * k, axis=-1, keepdims=True) + EPS)
        beta = beta_all[:, SM_BETA + hd:SM_BETA + hd + 1]
        gc = gcum[:, SM_ALPHA + hd:SM_ALPHA + hd + 1]
        gr = gcum_t[SM_ALPHA + hd:SM_ALPHA + hd + 1, :]
        gt = gtot[:, SM_ALPHA + hd:SM_ALPHA + hd + 1]
        decay = jnp.where(incl, jnp.exp(jnp.where(incl, gc - gr, 0.0)), 0.0)
        eg = jnp.exp(gc)
        kb = k * beta
        a_mat = jnp.where(strict, _mm(kb, k, NT) * decay, 0.0)
        t_inv = _tri_inverse(a_mat, SEG, row, col, eye)
        uw = _mm(t_inv, jnp.concatenate([v * beta, kb * eg], axis=1))
        u, w = uw[:, :128], uw[:, 128:]
        qe = q * eg
        kd = k * jnp.exp(gt - gc)
        attn = jnp.where(incl, _mm(q, k, NT) * decay, 0.0)
        q_s, w_s = [], []
        for s in range(NSEG):
            rs = slice(s * SEG, (s + 1) * SEG)
            both = _mm(jnp.concatenate([qe[rs], w[rs]], axis=0), sd[s, hd])
            q_s.append(both[:SEG])
            w_s.append(both[SEG:])
        q_st = q_s[0] if NSEG == 1 else jnp.concatenate(q_s, axis=0)
        w_st = w_s[0] if NSEG == 1 else jnp.concatenate(w_s, axis=0)
        v_new = u - w_st
        o = q_st + _mm(attn, v_new)
        for s in range(NSEG):
            rs = slice(s * SEG, (s + 1) * SEG)
            g_last = jnp.exp(gt[s * SEG:s * SEG + 1, :])
            sd[s, hd] = sd[s, hd] * g_last + _mm(kd[rs], v_new[rs], TN)
        outs.append(_rms(o) * norm_a[:, hd * 128:(hd + 1) * 128])
    put(0, jnp.concatenate(outs, axis=1))

    x_gate = _mm(small, w2p[...]) + b2[...]
    lg = -_softplus(-x_gate) * (1.0 / GLA_TAU)
    if not cfg.all_valid:
        lg = jnp.where(valid, lg, 0.0)
    bcum = _mm(l_incl, lg, passes=6)
    btot = _mm(m_same, lg, passes=6)
    q_c = proj[:, OFF_QC:OFF_QC + 256] * (GLA_DK ** -0.5)
    k_c = proj[:, OFF_KC:OFF_KC + 256]
    if not cfg.all_valid:
        k_c = jnp.where(valid, k_c, 0.0)
    v_c = proj[:, OFF_VC:OFF_VC + GROUP_W]
    qe_c = q_c * jnp.exp(bcum)
    ke_c = k_c * jnp.exp(jnp.minimum(-bcum, 80.0))
    kd_c = k_c * jnp.exp(btot - bcum)
    ones = jnp.ones((SEG, 128), f32)
    dec_s = [jnp.exp(_mm(lg[s * SEG:(s + 1) * SEG], ones, TN, passes=6)) for s in range(NSEG)]
    outs = []
    for hd in range(N_HEADS):
        ks = slice(hd * GLA_DK, (hd + 1) * GLA_DK)
        v = v_c[:, hd * 128:(hd + 1) * 128]
        attn = jnp.where(incl, _mm(qe_c[:, ks], ke_c[:, ks], NT), 0.0)
        o = _mm(attn, v)
        o_s = []
        for s in range(NSEG):
            rs = slice(s * SEG, (s + 1) * SEG)
            o_s.append(_mm(qe_c[rs, ks], sg[s, hd]))
            sg[s, hd] = sg[s, hd] * dec_s[s][ks, :] + _mm(kd_c[rs, ks], v[rs], TN)
        o = o + (o_s[0] if NSEG == 1 else jnp.concatenate(o_s, axis=0))
        outs.append(_rms(o) * norm_c[:, hd * 128:(hd + 1) * 128])
    put(2, jnp.concatenate(outs, axis=1))

    half = RET_DK // 2
    lane = lax.broadcasted_iota(jnp.int32, (R, 256), 1)
    first_half = (lane % RET_DK) < half

    def rotary(x):
        rot = jnp.where(first_half, pltpu.roll(x, 256 - half, 1), pltpu.roll(x, half, 1))
        return x * cos_t[...] + rot * sin_t[...]

    q_d = rotary(proj[:, OFF_QD:OFF_QD + 256])
    k_d = rotary(proj[:, OFF_KD:OFF_KD + 256]) * (RET_DK ** -0.5)
    v_d = proj[:, OFF_VD:OFF_VD + GROUP_W]
    qf = q_d * fs_t[...]
    kt = k_d * ts_t[...]
    outs = []
    for hd in range(N_HEADS):
        ks = slice(hd * RET_DK, (hd + 1) * RET_DK)
        v = v_d[:, hd * 128:(hd + 1) * 128]
        attn = _mm(q_d[:, ks], k_d[:, ks], NT) * intra_t[hd]
        o = _mm(attn, v)
        o_s = []
        for s in range(NSEG):
            rs = slice(s * SEG, (s + 1) * SEG)
            o_s.append(_mm(qf[rs, ks], sr[s, hd]))
            sr[s, hd] = sr[s, hd] * cd_t[hd, 0:1, :] + _mm(kt[rs, ks], v[rs], TN)
        o = o + (o_s[0] if NSEG == 1 else jnp.concatenate(o_s, axis=0))
        outs.append(_rms(o))
    put(3, jnp.concatenate(outs, axis=1))


def _mixer(cfg, proj, hist, states, tables, weights):
    R, NSEG, G, NC = cfg.rows, cfg.nseg, cfg.groups, cfg.nc
    sd_in, h_in, sg_in, sr_in = states
    shared = sd_in.shape[0] != G * NSEG

    def st(i):
        return 0 if shared else i

    row_map = lambda g, c: (cfg.row_block_off + g * NC + c, 0)
    if cfg.embedded:
        hist_specs = [pl.BlockSpec((R, QKV_W), lambda g, c: (g * NC + c, 0)),
                      pl.BlockSpec((R, GROUP_W), lambda g, c: (g * NC + c, 0))]
    else:
        hist_specs = [pl.BlockSpec((8, QKV_W), lambda g, c: (0, 0)),
                      pl.BlockSpec((8, GROUP_W), lambda g, c: (0, 0))]
    in_specs = [pl.BlockSpec((R, N_PACK), row_map)] + hist_specs + [
        pl.BlockSpec((NSEG, N_HEADS, DN_DK, HEAD_V), lambda g, c: (st(g), 0, 0, 0)),
        pl.BlockSpec((None, NSEG, GROUP_W), lambda g, c: (st(g), 0, 0)),
        pl.BlockSpec((NSEG, N_HEADS, GLA_DK, HEAD_V), lambda g, c: (st(g), 0, 0, 0)),
        pl.BlockSpec((NSEG, N_HEADS, RET_DK, HEAD_V), lambda g, c: (st(g), 0, 0, 0)),
        pl.BlockSpec((R, 256), lambda g, c: (c, 0)),
        pl.BlockSpec((R, 256), lambda g, c: (c, 0)),
        pl.BlockSpec((N_HEADS, R, R), lambda g, c: (0, 0, 0)),
        pl.BlockSpec((R, 256), lambda g, c: (0, 0)),
        pl.BlockSpec((R, 256), lambda g, c: (0, 0)),
        pl.BlockSpec((N_HEADS, 8, 128), lambda g, c: (0, 0, 0)),
    ]
    for w in weights:
        in_specs.append(pl.BlockSpec(w.shape, lambda g, c, nd=w.ndim: (0,) * nd))

    n_rows = G * NC * R
    out_shape = [
        jax.ShapeDtypeStruct((n_rows, D_MODEL), bf16),
        jax.ShapeDtypeStruct((G * NSEG, N_HEADS, DN_DK, HEAD_V), f32),
        jax.ShapeDtypeStruct((n_rows, QKV_W) if cfg.embedded else (G, CONV_K - 1, QKV_W), f32),
        jax.ShapeDtypeStruct((G, NSEG, GROUP_W), f32),
        jax.ShapeDtypeStruct((n_rows, GROUP_W) if cfg.embedded else (G, CONV_K - 1, GROUP_W), f32),
        jax.ShapeDtypeStruct((G * NSEG, N_HEADS, GLA_DK, HEAD_V), f32),
        jax.ShapeDtypeStruct((G * NSEG, N_HEADS, RET_DK, HEAD_V), f32),
    ]
    if cfg.embedded:
        conv_specs = [pl.BlockSpec((R, QKV_W), lambda g, c: (g * NC + c, 0)),
                      pl.BlockSpec((R, GROUP_W), lambda g, c: (g * NC + c, 0))]
    else:
        conv_specs = [pl.BlockSpec((None, CONV_K - 1, QKV_W), lambda g, c: (g, 0, 0)),
                      pl.BlockSpec((None, CONV_K - 1, GROUP_W), lambda g, c: (g, 0, 0))]
    out_specs = [
        pl.BlockSpec((R, D_MODEL), lambda g, c: (g * NC + c, 0)),
        pl.BlockSpec((NSEG, N_HEADS, DN_DK, HEAD_V), lambda g, c: (g, 0, 0, 0)),
        conv_specs[0],
        pl.BlockSpec((None, NSEG, GROUP_W), lambda g, c: (g, 0, 0)),
        conv_specs[1],
        pl.BlockSpec((NSEG, N_HEADS, GLA_DK, HEAD_V), lambda g, c: (g, 0, 0, 0)),
        pl.BlockSpec((NSEG, N_HEADS, RET_DK, HEAD_V), lambda g, c: (g, 0, 0, 0)),
    ]
    scratch = [
        pltpu.VMEM((R + 16, QKV_W), f32),
        pltpu.VMEM((R + 16, GROUP_W), f32),
        pltpu.VMEM((N_HEADS, R, 128), f32),
        pltpu.VMEM((N_HEADS, R, 128), f32),
        pltpu.VMEM((N_HEADS, R, 128), f32),
    ]
    return pl.pallas_call(
        functools.partial(_mixer_kernel, cfg),
        grid=(G, NC),
        in_specs=in_specs,
        out_specs=out_specs,
        out_shape=out_shape,
        scratch_shapes=scratch,
        compiler_params=pltpu.CompilerParams(
            dimension_semantics=("arbitrary", "arbitrary"), vmem_limit_bytes=VMEM_LIMIT),
        name="mixer_r%d_s%d" % (R, cfg.seg),
    )(proj, *hist, sd_in, h_in, sg_in, sr_in, *tables, *weights)


def _rope_tables(pos):
    half = RET_DK // 2
    freqs = ROPE_BASE ** (-jnp.arange(half, dtype=f32) / half)
    ang = pos.astype(f32)[:, None] * freqs
    cos, sin = jnp.cos(ang), jnp.sin(ang)
    cos_h = jnp.concatenate([cos, cos], axis=1)
    sin_h = jnp.concatenate([-sin, sin], axis=1)
    return jnp.tile(cos_h, (1, N_HEADS)), jnp.tile(sin_h, (1, N_HEADS))


def _ret_tables(rows, seg, voff, vlen):
    log_gamma = jnp.log(1.0 - 2.0 ** (-5.0 - jnp.arange(N_HEADS, dtype=f32)))
    r = jnp.arange(rows)
    p = (r % seg - voff).astype(f32)
    ok = ((r % seg) >= voff) & ((r % seg) < voff + vlen)
    rel = p[:, None] - p[None, :]
    pair = ok[:, None] & ok[None, :] & ((r[:, None] // seg) == (r[None, :] // seg)) & (rel >= 0)
    intra = jnp.where(pair[None], jnp.exp(log_gamma[:, None, None] * jnp.maximum(rel, 0.0)[None]), 0.0)
    from_state = jnp.exp(log_gamma[:, None] * (p + 1.0))
    to_state = jnp.where(ok[None], jnp.exp(log_gamma[:, None] * (vlen - 1.0 - p)), 0.0)
    chunk_decay = jnp.exp(log_gamma * vlen)
    fs = jnp.repeat(from_state.T, RET_DK, axis=1)
    ts = jnp.repeat(to_state.T, RET_DK, axis=1)
    cd = jnp.broadcast_to(chunk_decay[:, None, None], (N_HEADS, 8, 128))
    return intra.astype(f32), fs.astype(f32), ts.astype(f32), cd.astype(f32)


def _pack_w_in(w_in):
    depth = w_in.shape[0]
    o = 0
    sizes = (QKV_W, N_HEADS, N_HEADS, GROUP_W, 256, 256, GROUP_W, GLA_RANK, 256, 256, GROUP_W, D_MODEL)
    offs = []
    for s in sizes:
        offs.append(o)
        o += s
    seg = lambda i: w_in[:, :, offs[i]:offs[i] + sizes[i]]
    pad = jnp.zeros((depth, D_MODEL, N_PACK - OFF_SMALL - (2 * N_HEADS + GLA_RANK)), w_in.dtype)
    packed = jnp.concatenate(
        [seg(0), seg(3), seg(4), seg(5), seg(6), seg(8), seg(9), seg(10), seg(11),
         seg(1), seg(2), seg(7), pad], axis=-1)
    return packed.astype(bf16)


def _lanes(vec, off, width=128):
    return jnp.pad(vec[None, :], ((0, 0), (off, width - off - vec.shape[0])))


def kernel(x_prompt, x_sample, state_delta, state_delta_conv, state_lru, state_lru_conv, state_gla,
           state_ret, meta_tokens, norm_w, w_in, conv_a, a_log, dt_bias, norm_a, conv_b, conv_b_bias,
           lru_wa, lru_ba, lru_wx, lru_bx, lru_lambda, gla_w2, gla_b2, norm_c, w_out, final_norm):
    depth = w_in.shape[0]
    bp, lp = x_prompt.shape[0], x_prompt.shape[1]
    bs, ls = x_sample.shape[0], x_sample.shape[1]
    assert lp % CHUNK == 0 and ls == CONV_K and TILE_OFF + ls <= TILE
    nc_main = lp // CHUNK
    bb = 8
    assert bs % bb == 0

    w_in_p = _pack_w_in(w_in)
    w_out_b = w_out.astype(bf16)

    h_main = x_prompt.reshape(bp * lp, D_MODEL)
    tiles = jnp.pad(x_sample, ((0, 0), (TILE_OFF, TILE - TILE_OFF - ls), (0, 0)))
    h_small = jnp.concatenate([tiles.reshape(bs * TILE, D_MODEL), meta_tokens.astype(x_prompt.dtype)], axis=0)
    n_tile_rows = bs * TILE
    tm_small = h_small.shape[0] // 2
    assert tm_small % 8 == 0

    cfg_main = _Cfg(CHUNK, CHUNK, 0, CHUNK, nc_main, bp, False, 0)
    cfg_meta = _Cfg(N_META, N_META, 0, N_META, 1, 1, False, n_tile_rows // N_META)
    cfg_dec = _Cfg(bb * TILE, TILE, TILE_OFF, ls, 1, bs // bb, True, 0)

    pos_main = N_META + jnp.arange(lp)
    pos_meta = jnp.arange(N_META)
    pos_dec = jnp.tile(PAST_LEN + jnp.arange(TILE) - TILE_OFF, bb)
    tab_main = _rope_tables(pos_main) + _ret_tables(CHUNK, CHUNK, 0, CHUNK)
    tab_meta = _rope_tables(pos_meta) + _ret_tables(N_META, N_META, 0, N_META)
    tab_dec = _rope_tables(pos_dec) + _ret_tables(bb * TILE, TILE, TILE_OFF, ls)

    zeros_meta = (
        jnp.zeros((1, N_HEADS, DN_DK, HEAD_V), f32), jnp.zeros((1, 1, GROUP_W), f32),
        jnp.zeros((1, N_HEADS, GLA_DK, HEAD_V), f32), jnp.zeros((1, N_HEADS, RET_DK, HEAD_V), f32))
    zero_hist = (jnp.zeros((8, QKV_W), f32), jnp.zeros((8, GROUP_W), f32))

    p_out = [[] for _ in range(6)]
    s_out = [[] for _ in range(6)]
    for l in range(depth):
        weights = (
            conv_a[l], _lanes(a_log[l], SM_ALPHA), _lanes(dt_bias[l], SM_ALPHA),
            jnp.tile(norm_a[l], N_HEADS)[None], conv_b[l], conv_b_bias[l][None],
            lru_wa[l], lru_wx[l], lru_ba[l][None], lru_bx[l][None], lru_lambda[l][None],
            jnp.pad(gla_w2[l], ((SM_RC, 128 - SM_RC - GLA_RANK), (0, 0))), gla_b2[l][None],
            jnp.tile(norm_c[l], N_HEADS)[None],
        )
        nw = norm_w[l][None]
        proj_main = _inproj(h_main, nw, w_in_p[l], 512, 1280)
        proj_small = _inproj(h_small, nw, w_in_p[l], tm_small, 1280)

        mx_meta, sd_m, ca_m, h_m, cb_m, sg_m, sr_m = _mixer(
            cfg_meta, proj_small, zero_hist, zeros_meta, tab_meta, weights)
        hist_main = (jnp.pad(ca_m[0], ((8 - (CONV_K - 1), 0), (0, 0))),
                     jnp.pad(cb_m[0], ((8 - (CONV_K - 1), 0), (0, 0))))
        mx_main, sd_p, ca_p, h_p, cb_p, sg_p, sr_p = _mixer(
            cfg_main, proj_main, hist_main, (sd_m, h_m, sg_m, sr_m), tab_main, weights)

        hist_dec = (
            jnp.pad(state_delta_conv[l], ((0, 0), (0, TILE - (CONV_K - 1)), (0, 0))).reshape(n_tile_rows, QKV_W),
            jnp.pad(state_lru_conv[l], ((0, 0), (0, TILE - (CONV_K - 1)), (0, 0))).reshape(n_tile_rows, GROUP_W))
        st_dec = (state_delta[l], state_lru[l].reshape(bs // bb, bb, GROUP_W), state_gla[l], state_ret[l])
        mx_dec, sd_s, ca_s, h_s, cb_s, sg_s, sr_s = _mixer(
            cfg_dec, proj_small, hist_dec, st_dec, tab_dec, weights)

        h_main = _outproj(mx_main, w_out_b[l], h_main, 512)
        mx_small = jnp.concatenate([mx_dec, mx_meta], axis=0)
        h_small = _outproj(mx_small, w_out_b[l], h_small, tm_small)

        for acc, val in zip(p_out, (sd_p, ca_p, h_p.reshape(bp, GROUP_W), cb_p, sg_p, sr_p)):
            acc.append(val)
        ca_s = ca_s.reshape(bs, TILE, QKV_W)[:, :CONV_K - 1]
        cb_s = cb_s.reshape(bs, TILE, GROUP_W)[:, :CONV_K - 1]
        for acc, val in zip(s_out, (sd_s, ca_s, h_s.reshape(bs, GROUP_W), cb_s, sg_s, sr_s)):
            acc.append(val)

    fn = final_norm[None]
    y_prompt = _final_norm(h_main, fn, 512).reshape(bp, lp, D_MODEL)
    y_small = _final_norm(h_small, fn, tm_small)
    y_sample = y_small[:n_tile_rows].reshape(bs, TILE, D_MODEL)[:, TILE_OFF:TILE_OFF + ls]
    return (y_prompt, y_sample, *[jnp.stack(a) for a in p_out], *[jnp.stack(a) for a in s_out])
```

```python
import functools

import jax
import jax.numpy as jnp
from jax import lax
from jax.experimental import pallas as pl
from jax.experimental.pallas import tpu as pltpu

f32 = jnp.float32
bf16 = jnp.bfloat16

D_MODEL = 2048
N_META = 16
CONV_K = 4
CHUNK = 64
N_HEADS = 4
HEAD_V = 128
GROUP_W = N_HEADS * HEAD_V
DN_DK = 128
GLA_DK = 64
RET_DK = 64
GLA_RANK = 16
GLA_TAU = 16.0
LRU_C = 8.0
ROPE_BASE = 10000.0
EPS = 1e-6
PAST_LEN = 16384
QKV_W = 3 * N_HEADS * DN_DK

OFF_QKV = 0
OFF_XB = 1536
OFF_QC = 2048
OFF_KC = 2304
OFF_VC = 2560
OFF_QD = 3072
OFF_KD = 3328
OFF_VD = 3584
OFF_GATE = 4096
OFF_SMALL = 6144
N_PACK = 6400
SM_ALPHA = 0
SM_BETA = 4
SM_RC = 8

TILE = 8
TILE_OFF = CONV_K - 1
DEC_ROWS = 64

VMEM_LIMIT = 52 * 1024 * 1024

NN = (((1,), (0,)), ((), ()))
NT = (((1,), (1,)), ((), ()))
TN = (((0,), (0,)), ((), ()))


def _split(a):
    hi = a.astype(bf16)
    lo = (a - hi.astype(f32)).astype(bf16)
    return hi, lo


def _mm(a, b, dims=NN, passes=3):
    if passes == 6:
        return lax.dot_general(a, b, dims, precision=lax.Precision.HIGHEST, preferred_element_type=f32)
    if passes == 1:
        return lax.dot_general(a.astype(bf16), b.astype(bf16), dims, preferred_element_type=f32)
    ah, al = _split(a)
    bh, bl = _split(b)
    d = lambda x, y: lax.dot_general(x, y, dims, preferred_element_type=f32)
    return d(ah, bh) + (d(ah, bl) + d(al, bh))


def _softplus(x):
    return jnp.maximum(x, 0.0) + jnp.log1p(jnp.exp(-jnp.abs(x)))


def _sigmoid(x):
    return 1.0 / (1.0 + jnp.exp(-x))


def _silu(x):
    return x * _sigmoid(x)


def _rms(x):
    return x * lax.rsqrt(jnp.mean(x * x, axis=-1, keepdims=True) + EPS)


def _rows(parts):
    return parts[0] if len(parts) == 1 else jnp.concatenate(parts, axis=0)


def _inproj_kernel(x_ref, nw_ref, w_ref, o_ref, xn_ref):
    @pl.when(pl.program_id(1) == 0)
    def _():
        xn_ref[...] = (_rms(x_ref[...]) * nw_ref[...]).astype(bf16)

    o_ref[...] = jnp.dot(xn_ref[...], w_ref[...], preferred_element_type=f32)


def _inproj(x, nw, w, tm, tn):
    m = x.shape[0]
    return pl.pallas_call(
        _inproj_kernel,
        grid=(pl.cdiv(m, tm), N_PACK // tn),
        in_specs=[
            pl.BlockSpec((tm, D_MODEL), lambda i, j: (i, 0)),
            pl.BlockSpec((1, D_MODEL), lambda i, j: (0, 0)),
            pl.BlockSpec((D_MODEL, tn), lambda i, j: (0, j)),
        ],
        out_specs=pl.BlockSpec((tm, tn), lambda i, j: (i, j)),
        out_shape=jax.ShapeDtypeStruct((m, N_PACK), f32),
        scratch_shapes=[pltpu.VMEM((tm, D_MODEL), bf16)],
        compiler_params=pltpu.CompilerParams(
            dimension_semantics=("arbitrary", "arbitrary"), vmem_limit_bytes=VMEM_LIMIT),
        name="inproj",
    )(x, nw, w)


def _outproj_kernel(m_ref, w_ref, x_ref, o_ref):
    o_ref[...] = x_ref[...] + jnp.dot(m_ref[...], w_ref[...], preferred_element_type=f32)


def _outproj(mixed, w, x, tm):
    m = x.shape[0]
    return pl.pallas_call(
        _outproj_kernel,
        grid=(pl.cdiv(m, tm),),
        in_specs=[
            pl.BlockSpec((tm, D_MODEL), lambda i: (i, 0)),
            pl.BlockSpec((D_MODEL, D_MODEL), lambda i: (0, 0)),
            pl.BlockSpec((tm, D_MODEL), lambda i: (i, 0)),
        ],
        out_specs=pl.BlockSpec((tm, D_MODEL), lambda i: (i, 0)),
        out_shape=jax.ShapeDtypeStruct((m, D_MODEL), f32),
        compiler_params=pltpu.CompilerParams(
            dimension_semantics=("arbitrary",), vmem_limit_bytes=VMEM_LIMIT),
        name="outproj",
    )(mixed, w, x)


def _final_norm_kernel(x_ref, w_ref, o_ref):
    o_ref[...] = _rms(x_ref[...]) * w_ref[...]


def _final_norm(x, w, tm):
    m = x.shape[0]
    return pl.pallas_call(
        _final_norm_kernel,
        grid=(pl.cdiv(m, tm),),
        in_specs=[pl.BlockSpec((tm, D_MODEL), lambda i: (i, 0)),
                  pl.BlockSpec((1, D_MODEL), lambda i: (0, 0))],
        out_specs=pl.BlockSpec((tm, D_MODEL), lambda i: (i, 0)),
        out_shape=jax.ShapeDtypeStruct((m, D_MODEL), f32),
        compiler_params=pltpu.CompilerParams(dimension_semantics=("arbitrary",)),
        name="final_norm",
    )(x, w)


class _Cfg:
    def __init__(self, rows, seg, voff, vlen, nc, ng, gblocks, embedded, blk_off):
        self.rows, self.seg, self.voff, self.vlen = rows, seg, voff, vlen
        self.nc, self.ng, self.gblocks, self.embedded, self.blk_off = nc, ng, gblocks, embedded, blk_off
        self.nseg = rows // seg
        self.nstate = ng * self.nseg
        self.all_valid = (voff == 0 and vlen == seg)


def _tri_inverse(a_list, seg, row, col, eye):
    def blk(s):
        return (row // s) == (col // s)

    b8 = blk(8)
    n = [-jnp.where(b8, a, 0.0) for a in a_list]
    n2 = [_mm(x, x) for x in n]
    n4 = [_mm(x, x) for x in n2]
    t = [_mm(eye + x, eye + y) for x, y in zip(n, n2)]
    t = [_mm(x, eye + y) for x, y in zip(t, n4)]
    s = 8
    while s < seg:
        mask = blk(2 * s) & jnp.logical_not(blk(s))
        off = [jnp.where(mask, a, 0.0) for a in a_list]
        tb = [_mm(x, o) for x, o in zip(t, off)]
        t = [x - _mm(y, x) for x, y in zip(t, tb)]
        s *= 2
    return t


def _mixer_kernel(cfg, *refs):
    R, SEG, NSEG, NG, NST = cfg.rows, cfg.seg, cfg.nseg, cfg.ng, cfg.nstate
    it = iter(refs)
    proj = next(it)
    if cfg.embedded:
        buf_a, buf_b = next(it), next(it)
    else:
        ic_a, ic_b = next(it), next(it)
    sd_in, h_in, sg_in, sr_in = next(it), next(it), next(it), next(it)
    cos_t, sin_t, intra_t, fs_t, ts_t, cd_t = (next(it) for _ in range(6))
    (conv_a, alogv, dtbv, norm_a, conv_b, cbb, wa, wx, ba, bx, lam, w2p, b2, norm_c) = (
        next(it) for _ in range(14))
    mixed, sd, conv_a_o, h_o, conv_b_o, sg, sr = (next(it) for _ in range(7))
    xp_a, xp_b, l_a, l_b, l_o = (next(it) for _ in range(5))

    c = pl.program_id(1)
    groups = range(NG)
    heads = range(N_HEADS)
    units = [(g, hd) for g in groups for hd in heads]
    seg_rows = [slice(s * SEG, (s + 1) * SEG) for s in range(NSEG)]

    @pl.when(c == 0)
    def _init():
        shared = sd_in.shape[0] != NST
        for i in range(NST):
            j = 0 if shared else i
            sd[i] = sd_in[j]
            sg[i] = sg_in[j]
            sr[i] = sr_in[j]
        h_o[...] = jnp.broadcast_to(h_in[...], (NST, GROUP_W))
        for g in groups:
            if cfg.embedded:
                xp_a[g, 0:8, :] = jnp.zeros((8, QKV_W), f32)
                xp_b[g, 0:8, :] = jnp.zeros((8, GROUP_W), f32)
            else:
                xp_a[g, 0:8, :] = ic_a[...]
                xp_b[g, 0:8, :] = ic_b[...]
            xp_a[g, 8 + R:16 + R, :] = jnp.zeros((8, QKV_W), f32)
            xp_b[g, 8 + R:16 + R, :] = jnp.zeros((8, GROUP_W), f32)

    row = lax.broadcasted_iota(jnp.int32, (R, R), 0)
    col = lax.broadcasted_iota(jnp.int32, (R, R), 1)
    same = (row // SEG) == (col // SEG)
    incl = same & (col <= row)
    strict = same & (col < row)
    eye = (row == col).astype(f32)
    l_incl = incl.astype(f32)
    m_same = same.astype(f32)
    rmod = lax.broadcasted_iota(jnp.int32, (R, 1), 0) % SEG
    valid = (rmod >= cfg.voff) & (rmod < cfg.voff + cfg.vlen)
    is_hist = rmod < TILE_OFF

    def conv(xp, g, x, buf, w_ref):
        if cfg.embedded:
            x = jnp.where(is_hist, buf[g], x)
        xp[g, 8:8 + R, :] = x
        y = w_ref[CONV_K - 1:CONV_K, :] * x
        for s in range(1, CONV_K):
            y = y + w_ref[CONV_K - 1 - s:CONV_K - s, :] * xp[g, 8 - s:8 - s + R, :]
        return y

    ya = [conv(xp_a, g, proj[g, :, OFF_QKV:OFF_QKV + QKV_W], buf_a if cfg.embedded else None, conv_a)
          for g in groups]
    yb = [conv(xp_b, g, proj[g, :, OFF_XB:OFF_XB + GROUP_W], buf_b if cfg.embedded else None, conv_b)
          + cbb[...] for g in groups]
    for g in groups:
        if cfg.embedded:
            conv_a_o[g] = xp_a[g, 8 + TILE_OFF + 1:8 + TILE_OFF + 1 + R, :]
            conv_b_o[g] = xp_b[g, 8 + TILE_OFF + 1:8 + TILE_OFF + 1 + R, :]
        else:
            xp_a[g, 0:8, :] = xp_a[g, R:R + 8, :]
            xp_b[g, 0:8, :] = xp_b[g, R:R + 8, :]
    if not cfg.embedded:
        @pl.when(c == cfg.nc - 1)
        def _():
            for g in groups:
                conv_a_o[g] = xp_a[g, 5:8, :]
                conv_b_o[g] = xp_b[g, 5:8, :]

    small = [proj[g, :, OFF_SMALL:OFF_SMALL + 128] for g in groups]

    def put(g, k, o):
        gate = proj[g, :, OFF_GATE + k * GROUP_W:OFF_GATE + (k + 1) * GROUP_W]
        mixed[g, :, k * GROUP_W:(k + 1) * GROUP_W] = (o * _silu(gate)).astype(mixed.dtype)

    sp_lam = _softplus(-lam[...])
    r_pre = [[_mm(yb[g][:, n * 128:(n + 1) * 128], wa[n]) for n in heads] for g in groups]
    i_pre = [[_mm(yb[g][:, n * 128:(n + 1) * 128], wx[n]) for n in heads] for g in groups]
    h_all = h_o[...]
    h_cur = {}
    for g, n in units:
        ls = slice(n * 128, (n + 1) * 128)
        x_n = yb[g][:, ls]
        log_a = -LRU_C * _sigmoid(r_pre[g][n] + ba[:, ls]) * sp_lam[:, ls]
        a_t = jnp.exp(log_a)
        b_t = jnp.sqrt(-jnp.tanh(log_a) * (a_t * a_t + 1.0)) * (_sigmoid(i_pre[g][n] + bx[:, ls]) * x_n)
        l_a[g * N_HEADS + n] = a_t
        l_b[g * N_HEADS + n] = b_t
        if not cfg.all_valid:
            l_o[g * N_HEADS + n] = jnp.zeros((R, 128), f32)
        h_cur[g, n] = h_all[g * NSEG:(g + 1) * NSEG, ls]
    for t in range(cfg.vlen):
        idx = pl.ds(cfg.voff + t, 1) if NSEG == 1 else pl.ds(cfg.voff + t, NSEG, stride=SEG)
        for g, n in units:
            k = g * N_HEADS + n
            h_cur[g, n] = l_a[k, idx, :] * h_cur[g, n] + l_b[k, idx, :]
            l_o[k, idx, :] = h_cur[g, n]
    h_o[...] = _rows([jnp.concatenate([h_cur[g, n] for n in heads], axis=1) for g in groups])
    for g in groups:
        put(g, 1, jnp.concatenate([l_o[g * N_HEADS + n] for n in heads], axis=1))

    qkv = [_silu(y) for y in ya]
    g_all, beta_all = [], []
    for g in groups:
        ga = -jnp.exp(alogv[...]) * _softplus(small[g] + dtbv[...])
        be = _sigmoid(small[g])
        if not cfg.all_valid:
            ga = jnp.where(valid, ga, 0.0)
            be = jnp.where(valid, be, 0.0)
        g_all.append(ga)
        beta_all.append(be)
    gcum = [_mm(l_incl, x, passes=6) for x in g_all]
    gtot = [_mm(m_same, x, passes=6) for x in g_all]
    gcum_t = [x.T for x in gcum]

    q_l, k_l, v_l, be_l, gc_l, gt_l, dec_l, eg_l = ([] for _ in range(8))
    for g, hd in units:
        q = qkv[g][:, hd * 128:(hd + 1) * 128]
        k = qkv[g][:, 512 + hd * 128:512 + (hd + 1) * 128]
        q_l.append(q * lax.rsqrt(jnp.sum(q * q, axis=-1, keepdims=True) + EPS) * (DN_DK ** -0.5))
        k_l.append(k * lax.rsqrt(jnp.sum(k * k, axis=-1, keepdims=True) + EPS))
        v_l.append(qkv[g][:, 1024 + hd * 128:1024 + (hd + 1) * 128])
        be_l.append(beta_all[g][:, SM_BETA + hd:SM_BETA + hd + 1])
        gc = gcum[g][:, SM_ALPHA + hd:SM_ALPHA + hd + 1]
        gr = gcum_t[g][SM_ALPHA + hd:SM_ALPHA + hd + 1, :]
        gc_l.append(gc)
        gt_l.append(gtot[g][:, SM_ALPHA + hd:SM_ALPHA + hd + 1])
        dec_l.append(jnp.where(incl, jnp.exp(jnp.where(incl, gc - gr, 0.0)), 0.0))
        eg_l.append(jnp.exp(gc))
    nu = len(units)
    kb_l = [k_l[u] * be_l[u] for u in range(nu)]
    a_l = [jnp.where(strict, _mm(kb_l[u], k_l[u], NT) * dec_l[u], 0.0) for u in range(nu)]
    attn_l = [jnp.where(incl, _mm(q_l[u], k_l[u], NT) * dec_l[u], 0.0) for u in range(nu)]
    t_l = _tri_inverse(a_l, SEG, row, col, eye)
    uw_l = [_mm(t_l[u], jnp.concatenate([v_l[u] * be_l[u], kb_l[u] * eg_l[u]], axis=1)) for u in range(nu)]
    st_l = []
    for u, (g, hd) in enumerate(units):
        qe = q_l[u] * eg_l[u]
        w = uw_l[u][:, 128:]
        st_l.append([_mm(jnp.concatenate([qe[rs], w[rs]], axis=0), sd[g * NSEG + s, hd])
                     for s, rs in enumerate(seg_rows)])
    vn_l = [uw_l[u][:, :128] - _rows([b[SEG:] for b in st_l[u]]) for u in range(nu)]
    o_l = [_rows([b[:SEG] for b in st_l[u]]) + _mm(attn_l[u], vn_l[u]) for u in range(nu)]
    for u, (g, hd) in enumerate(units):
        kd = k_l[u] * jnp.exp(gt_l[u] - gc_l[u])
        for s, rs in enumerate(seg_rows):
            g_last = jnp.exp(gt_l[u][s * SEG:s * SEG + 1, :])
            i = g * NSEG + s
            sd[i, hd] = sd[i, hd] * g_last + _mm(kd[rs], vn_l[u][rs], TN)
    for g in groups:
        put(g, 0, jnp.concatenate(
            [_rms(o_l[g * N_HEADS + hd]) * norm_a[:, hd * 128:(hd + 1) * 128] for hd in heads], axis=1))

    lg = []
    for g in groups:
        x_gate = _mm(small[g], w2p[...]) + b2[...]
        z = -_softplus(-x_gate) * (1.0 / GLA_TAU)
        lg.append(z if cfg.all_valid else jnp.where(valid, z, 0.0))
    bcum = [_mm(l_incl, x, passes=6) for x in lg]
    btot = [_mm(m_same, x, passes=6) for x in lg]
    ones = jnp.ones((SEG, 128), f32)
    dec_s = [[jnp.exp(_mm(lg[g][rs], ones, TN, passes=6)) for rs in seg_rows] for g in groups]
    qe_c, ke_c, kd_c = [], [], []
    for g in groups:
        q_c = proj[g, :, OFF_QC:OFF_QC + 256] * (GLA_DK ** -0.5)
        k_c = proj[g, :, OFF_KC:OFF_KC + 256]
        if not cfg.all_valid:
            k_c = jnp.where(valid, k_c, 0.0)
        qe_c.append(q_c * jnp.exp(bcum[g]))
        ke_c.append(k_c * jnp.exp(jnp.minimum(-bcum[g], 80.0)))
        kd_c.append(k_c * jnp.exp(btot[g] - bcum[g]))
    ks_l = [slice(hd * GLA_DK, (hd + 1) * GLA_DK) for hd in heads]
    vc_l = [proj[g, :, OFF_VC + hd * 128:OFF_VC + (hd + 1) * 128] for g, hd in units]
    attn_l = [jnp.where(incl, _mm(qe_c[g][:, ks_l[hd]], ke_c[g][:, ks_l[hd]], NT), 0.0) for g, hd in units]
    o_l = [_mm(attn_l[u], vc_l[u]) for u in range(nu)]
    for u, (g, hd) in enumerate(units):
        parts = []
        for s, rs in enumerate(seg_rows):
            i = g * NSEG + s
            parts.append(_mm(qe_c[g][rs, ks_l[hd]], sg[i, hd]))
            sg[i, hd] = sg[i, hd] * dec_s[g][s][ks_l[hd], :] + _mm(kd_c[g][rs, ks_l[hd]], vc_l[u][rs], TN)
        o_l[u] = o_l[u] + _rows(parts)
    for g in groups:
        put(g, 2, jnp.concatenate(
            [_rms(o_l[g * N_HEADS + hd]) * norm_c[:, hd * 128:(hd + 1) * 128] for hd in heads], axis=1))

    half = RET_DK // 2
    lane = lax.broadcasted_iota(jnp.int32, (R, 256), 1)
    first_half = (lane % RET_DK) < half

    def rotary(x):
        rot = jnp.where(first_half, pltpu.roll(x, 256 - half, 1), pltpu.roll(x, half, 1))
        return x * cos_t[...] + rot * sin_t[...]

    q_d = [rotary(proj[g, :, OFF_QD:OFF_QD + 256]) for g in groups]
    k_d = [rotary(proj[g, :, OFF_KD:OFF_KD + 256]) * (RET_DK ** -0.5) for g in groups]
    qf = [x * fs_t[...] for x in q_d]
    kt = [x * ts_t[...] for x in k_d]
    ks_l = [slice(hd * RET_DK, (hd + 1) * RET_DK) for hd in heads]
    vd_l = [proj[g, :, OFF_VD + hd * 128:OFF_VD + (hd + 1) * 128] for g, hd in units]
    attn_l = [_mm(q_d[g][:, ks_l[hd]], k_d[g][:, ks_l[hd]], NT) * intra_t[hd] for g, hd in units]
    o_l = [_mm(attn_l[u], vd_l[u]) for u in range(nu)]
    for u, (g, hd) in enumerate(units):
        parts = []
        for s, rs in enumerate(seg_rows):
            i = g * NSEG + s
            parts.append(_mm(qf[g][rs, ks_l[hd]], sr[i, hd]))
            sr[i, hd] = sr[i, hd] * cd_t[hd, 0:1, :] + _mm(kt[g][rs, ks_l[hd]], vd_l[u][rs], TN)
        o_l[u] = o_l[u] + _rows(parts)
    for g in groups:
        put(g, 3, jnp.concatenate([_rms(o_l[g * N_HEADS + hd]) for hd in heads], axis=1))


def _mixer(cfg, proj, hist, states, tables, weights):
    R, NG, GB, NC, NST = cfg.rows, cfg.ng, cfg.gblocks, cfg.nc, cfg.nstate
    sd_in, h_in, sg_in, sr_in = states
    shared = sd_in.shape[0] != GB * NST
    nin = 1 if shared else NST

    def st(i):
        return 0 if shared else i

    if cfg.embedded:
        hist_specs = [pl.BlockSpec((NG, R, QKV_W), lambda gb, c: (gb, c, 0)),
                      pl.BlockSpec((NG, R, GROUP_W), lambda gb, c: (gb, c, 0))]
    else:
        hist_specs = [pl.BlockSpec((8, QKV_W), lambda gb, c: (0, 0)),
                      pl.BlockSpec((8, GROUP_W), lambda gb, c: (0, 0))]
    in_specs = [pl.BlockSpec((NG, R, N_PACK), lambda gb, c: (cfg.blk_off + gb, c, 0))] + hist_specs + [
        pl.BlockSpec((nin, N_HEADS, DN_DK, HEAD_V), lambda gb, c: (st(gb), 0, 0, 0)),
        pl.BlockSpec((None, nin, GROUP_W), lambda gb, c: (st(gb), 0, 0)),
        pl.BlockSpec((nin, N_HEADS, GLA_DK, HEAD_V), lambda gb, c: (st(gb), 0, 0, 0)),
        pl.BlockSpec((nin, N_HEADS, RET_DK, HEAD_V), lambda gb, c: (st(gb), 0, 0, 0)),
        pl.BlockSpec((R, 256), lambda gb, c: (c, 0)),
        pl.BlockSpec((R, 256), lambda gb, c: (c, 0)),
        pl.BlockSpec((N_HEADS, R, R), lambda gb, c: (0, 0, 0)),
        pl.BlockSpec((R, 256), lambda gb, c: (0, 0)),
        pl.BlockSpec((R, 256), lambda gb, c: (0, 0)),
        pl.BlockSpec((N_HEADS, 8, 128), lambda gb, c: (0, 0, 0)),
    ]
    for w in weights:
        in_specs.append(pl.BlockSpec(w.shape, lambda gb, c, nd=w.ndim: (0,) * nd))

    n_seq_rows = NC * R
    conv_rows = n_seq_rows if cfg.embedded else CONV_K - 1
    out_shape = [
        jax.ShapeDtypeStruct((GB * NG, n_seq_rows, D_MODEL), bf16),
        jax.ShapeDtypeStruct((GB * NST, N_HEADS, DN_DK, HEAD_V), f32),
        jax.ShapeDtypeStruct((GB * NG, conv_rows, QKV_W), f32),
        jax.ShapeDtypeStruct((GB, NST, GROUP_W), f32),
        jax.ShapeDtypeStruct((GB * NG, conv_rows, GROUP_W), f32),
        jax.ShapeDtypeStruct((GB * NST, N_HEADS, GLA_DK, HEAD_V), f32),
        jax.ShapeDtypeStruct((GB * NST, N_HEADS, RET_DK, HEAD_V), f32),
    ]
    if cfg.embedded:
        conv_specs = [pl.BlockSpec((NG, R, QKV_W), lambda gb, c: (gb, c, 0)),
                      pl.BlockSpec((NG, R, GROUP_W), lambda gb, c: (gb, c, 0))]
    else:
        conv_specs = [pl.BlockSpec((NG, CONV_K - 1, QKV_W), lambda gb, c: (gb, 0, 0)),
                      pl.BlockSpec((NG, CONV_K - 1, GROUP_W), lambda gb, c: (gb, 0, 0))]
    out_specs = [
        pl.BlockSpec((NG, R, D_MODEL), lambda gb, c: (gb, c, 0)),
        pl.BlockSpec((NST, N_HEADS, DN_DK, HEAD_V), lambda gb, c: (gb, 0, 0, 0)),
        conv_specs[0],
        pl.BlockSpec((None, NST, GROUP_W), lambda gb, c: (gb, 0, 0)),
        conv_specs[1],
        pl.BlockSpec((NST, N_HEADS, GLA_DK, HEAD_V), lambda gb, c: (gb, 0, 0, 0)),
        pl.BlockSpec((NST, N_HEADS, RET_DK, HEAD_V), lambda gb, c: (gb, 0, 0, 0)),
    ]
    scratch = [
        pltpu.VMEM((NG, R + 16, QKV_W), f32),
        pltpu.VMEM((NG, R + 16, GROUP_W), f32),
        pltpu.VMEM((NG * N_HEADS, R, 128), f32),
        pltpu.VMEM((NG * N_HEADS, R, 128), f32),
        pltpu.VMEM((NG * N_HEADS, R, 128), f32),
    ]
    return pl.pallas_call(
        functools.partial(_mixer_kernel, cfg),
        grid=(GB, NC),
        in_specs=in_specs,
        out_specs=out_specs,
        out_shape=out_shape,
        scratch_shapes=scratch,
        compiler_params=pltpu.CompilerParams(
            dimension_semantics=("arbitrary", "arbitrary"), vmem_limit_bytes=VMEM_LIMIT),
        name="mixer_r%d_s%d_g%d" % (R, cfg.seg, NG),
    )(proj, *hist, sd_in, h_in, sg_in, sr_in, *tables, *weights)


def _rope_tables(pos):
    half = RET_DK // 2
    freqs = ROPE_BASE ** (-jnp.arange(half, dtype=f32) / half)
    ang = pos.astype(f32)[:, None] * freqs
    cos, sin = jnp.cos(ang), jnp.sin(ang)
    cos_h = jnp.concatenate([cos, cos], axis=1)
    sin_h = jnp.concatenate([-sin, sin], axis=1)
    return jnp.tile(cos_h, (1, N_HEADS)), jnp.tile(sin_h, (1, N_HEADS))


def _ret_tables(rows, seg, voff, vlen):
    log_gamma = jnp.log(1.0 - 2.0 ** (-5.0 - jnp.arange(N_HEADS, dtype=f32)))
    r = jnp.arange(rows)
    p = (r % seg - voff).astype(f32)
    ok = ((r % seg) >= voff) & ((r % seg) < voff + vlen)
    rel = p[:, None] - p[None, :]
    pair = ok[:, None] & ok[None, :] & ((r[:, None] // seg) == (r[None, :] // seg)) & (rel >= 0)
    intra = jnp.where(pair[None], jnp.exp(log_gamma[:, None, None] * jnp.maximum(rel, 0.0)[None]), 0.0)
    from_state = jnp.exp(log_gamma[:, None] * (p + 1.0))
    to_state = jnp.where(ok[None], jnp.exp(log_gamma[:, None] * (vlen - 1.0 - p)), 0.0)
    chunk_decay = jnp.exp(log_gamma * vlen)
    fs = jnp.repeat(from_state.T, RET_DK, axis=1)
    ts = jnp.repeat(to_state.T, RET_DK, axis=1)
    cd = jnp.broadcast_to(chunk_decay[:, None, None], (N_HEADS, 8, 128))
    return intra.astype(f32), fs.astype(f32), ts.astype(f32), cd.astype(f32)


def _pack_w_in(w_in):
    depth = w_in.shape[0]
    o = 0
    sizes = (QKV_W, N_HEADS, N_HEADS, GROUP_W, 256, 256, GROUP_W, GLA_RANK, 256, 256, GROUP_W, D_MODEL)
    offs = []
    for s in sizes:
        offs.append(o)
        o += s
    seg = lambda i: w_in[:, :, offs[i]:offs[i] + sizes[i]]
    pad = jnp.zeros((depth, D_MODEL, N_PACK - OFF_SMALL - (2 * N_HEADS + GLA_RANK)), w_in.dtype)
    packed = jnp.concatenate(
        [seg(0), seg(3), seg(4), seg(5), seg(6), seg(8), seg(9), seg(10), seg(11),
         seg(1), seg(2), seg(7), pad], axis=-1)
    return packed.astype(bf16)


def _lanes(vec, off, width=128):
    return jnp.pad(vec[None, :], ((0, 0), (off, width - off - vec.shape[0])))


def kernel(x_prompt, x_sample, state_delta, state_delta_conv, state_lru, state_lru_conv, state_gla,
           state_ret, meta_tokens, norm_w, w_in, conv_a, a_log, dt_bias, norm_a, conv_b, conv_b_bias,
           lru_wa, lru_ba, lru_wx, lru_bx, lru_lambda, gla_w2, gla_b2, norm_c, w_out, final_norm):
    depth = w_in.shape[0]
    bp, lp = x_prompt.shape[0], x_prompt.shape[1]
    bs, ls = x_sample.shape[0], x_sample.shape[1]
    assert lp % CHUNK == 0 and ls == CONV_K and TILE_OFF + ls <= TILE
    nc_main = lp // CHUNK
    n_tile_rows = bs * TILE
    assert n_tile_rows % DEC_ROWS == 0 and DEC_ROWS % N_META == 0
    n_dec_blocks = n_tile_rows // DEC_ROWS
    seq_per_block = DEC_ROWS // TILE

    w_in_p = _pack_w_in(w_in)
    w_out_b = w_out.astype(bf16)

    h_main = x_prompt.reshape(bp * lp, D_MODEL)
    tiles = jnp.pad(x_sample, ((0, 0), (TILE_OFF, TILE - TILE_OFF - ls), (0, 0)))
    h_small = jnp.concatenate(
        [tiles.reshape(n_tile_rows, D_MODEL), meta_tokens.astype(x_prompt.dtype),
         jnp.zeros((DEC_ROWS - N_META, D_MODEL), x_prompt.dtype)], axis=0)
    n_small = h_small.shape[0]
    tm_small = n_small // 2
    assert tm_small % 8 == 0

    cfg_main = _Cfg(CHUNK, CHUNK, 0, CHUNK, nc_main, bp, 1, False, 0)
    cfg_meta = _Cfg(N_META, N_META, 0, N_META, 1, 1, 1, False, n_tile_rows // N_META)
    cfg_dec = _Cfg(DEC_ROWS, TILE, TILE_OFF, ls, 1, 1, n_dec_blocks, True, 0)

    pos_main = N_META + jnp.arange(lp)
    pos_meta = jnp.arange(N_META)
    pos_dec = jnp.tile(PAST_LEN + jnp.arange(TILE) - TILE_OFF, seq_per_block)
    tab_main = _rope_tables(pos_main) + _ret_tables(CHUNK, CHUNK, 0, CHUNK)
    tab_meta = _rope_tables(pos_meta) + _ret_tables(N_META, N_META, 0, N_META)
    tab_dec = _rope_tables(pos_dec) + _ret_tables(DEC_ROWS, TILE, TILE_OFF, ls)

    zeros_meta = (
        jnp.zeros((1, N_HEADS, DN_DK, HEAD_V), f32), jnp.zeros((1, 1, GROUP_W), f32),
        jnp.zeros((1, N_HEADS, GLA_DK, HEAD_V), f32), jnp.zeros((1, N_HEADS, RET_DK, HEAD_V), f32))
    zero_hist = (jnp.zeros((8, QKV_W), f32), jnp.zeros((8, GROUP_W), f32))

    p_out = [[] for _ in range(6)]
    s_out = [[] for _ in range(6)]
    for l in range(depth):
        weights = (
            conv_a[l], _lanes(a_log[l], SM_ALPHA), _lanes(dt_bias[l], SM_ALPHA),
            jnp.tile(norm_a[l], N_HEADS)[None], conv_b[l], conv_b_bias[l][None],
            lru_wa[l], lru_wx[l], lru_ba[l][None], lru_bx[l][None], lru_lambda[l][None],
            jnp.pad(gla_w2[l], ((SM_RC, 128 - SM_RC - GLA_RANK), (0, 0))), gla_b2[l][None],
            jnp.tile(norm_c[l], N_HEADS)[None],
        )
        nw = norm_w[l][None]
        proj_main = _inproj(h_main, nw, w_in_p[l], 512, 1280)
        proj_small = _inproj(h_small, nw, w_in_p[l], tm_small, 1280)

        mx_meta, sd_m, ca_m, h_m, cb_m, sg_m, sr_m = _mixer(
            cfg_meta, proj_small.reshape(n_small // N_META, N_META, N_PACK), zero_hist, zeros_meta,
            tab_meta, weights)
        hist_main = (jnp.pad(ca_m[0], ((8 - (CONV_K - 1), 0), (0, 0))),
                     jnp.pad(cb_m[0], ((8 - (CONV_K - 1), 0), (0, 0))))
        mx_main, sd_p, ca_p, h_p, cb_p, sg_p, sr_p = _mixer(
            cfg_main, proj_main.reshape(bp, lp, N_PACK), hist_main, (sd_m, h_m, sg_m, sr_m),
            tab_main, weights)

        hist_dec = (
            jnp.pad(state_delta_conv[l], ((0, 0), (0, TILE - (CONV_K - 1)), (0, 0))).reshape(
                n_dec_blocks, DEC_ROWS, QKV_W),
            jnp.pad(state_lru_conv[l], ((0, 0), (0, TILE - (CONV_K - 1)), (0, 0))).reshape(
                n_dec_blocks, DEC_ROWS, GROUP_W))
        st_dec = (state_delta[l], state_lru[l].reshape(n_dec_blocks, seq_per_block, GROUP_W),
                  state_gla[l], state_ret[l])
        mx_dec, sd_s, ca_s, h_s, cb_s, sg_s, sr_s = _mixer(
            cfg_dec, proj_small.reshape(n_small // DEC_ROWS, DEC_ROWS, N_PACK), hist_dec, st_dec,
            tab_dec, weights)

        h_main = _outproj(mx_main.reshape(bp * lp, D_MODEL), w_out_b[l], h_main, 512)
        mx_small = jnp.concatenate(
            [mx_dec.reshape(n_tile_rows, D_MODEL), mx_meta[0],
             jnp.zeros((DEC_ROWS - N_META, D_MODEL), bf16)], axis=0)
        h_small = _outproj(mx_small, w_out_b[l], h_small, tm_small)

        for acc, val in zip(p_out, (sd_p, ca_p, h_p.reshape(bp, GROUP_W), cb_p, sg_p, sr_p)):
            acc.append(val)
        ca_s = ca_s.reshape(bs, TILE, QKV_W)[:, :CONV_K - 1]
        cb_s = cb_s.reshape(bs, TILE, GROUP_W)[:, :CONV_K - 1]
        for acc, val in zip(s_out, (sd_s, ca_s, h_s.reshape(bs, GROUP_W), cb_s, sg_s, sr_s)):
            acc.append(val)

    fn = final_norm[None]
    y_prompt = _final_norm(h_main, fn, 512).reshape(bp, lp, D_MODEL)
    y_small = _final_norm(h_small, fn, tm_small)
    y_sample = y_small[:n_tile_rows].reshape(bs, TILE, D_MODEL)[:, TILE_OFF:TILE_OFF + ls]
    return (y_prompt, y_sample, *[jnp.stack(a) for a in p_out], *[jnp.stack(a) for a in s_out])
```

```python
import functools

import jax
import jax.numpy as jnp
from jax import lax
from jax.experimental import pallas as pl
from jax.experimental.pallas import tpu as pltpu

f32 = jnp.float32
bf16 = jnp.bfloat16

D_MODEL = 2048
N_META = 16
CONV_K = 4
CHUNK = 64
N_HEADS = 4
HEAD_V = 128
GROUP_W = N_HEADS * HEAD_V
DN_DK = 128
GLA_DK = 64
RET_DK = 64
GLA_RANK = 16
GLA_TAU = 16.0
LRU_C = 8.0
ROPE_BASE = 10000.0
EPS = 1e-6
PAST_LEN = 16384
QKV_W = 3 * N_HEADS * DN_DK

OFF_QKV = 0
OFF_XB = 1536
OFF_QC = 2048
OFF_KC = 2304
OFF_VC = 2560
OFF_QD = 3072
OFF_KD = 3328
OFF_VD = 3584
OFF_GATE = 4096
OFF_SMALL = 6144
N_PACK = 6400
SM_ALPHA = 0
SM_BETA = 4
SM_RC = 8

TILE = 8
TILE_OFF = CONV_K - 1
DEC_ROWS = 64

VMEM_LIMIT = 52 * 1024 * 1024
TM_MAIN_IN = 1024
TN_IN = 1280
TM_MAIN_OUT = 512

NN = (((1,), (0,)), ((), ()))
NT = (((1,), (1,)), ((), ()))
TN = (((0,), (0,)), ((), ()))


def _split(a):
    hi = a.astype(bf16)
    lo = (a - hi.astype(f32)).astype(bf16)
    return hi, lo


P_SOLVE = 3


def _mm(a, b, dims=NN, passes=1):
    if passes == 6:
        return lax.dot_general(a, b, dims, precision=lax.Precision.HIGHEST, preferred_element_type=f32)
    if passes == 1:
        return lax.dot_general(a.astype(bf16), b.astype(bf16), dims, preferred_element_type=f32)
    ah, al = _split(a)
    bh, bl = _split(b)
    d = lambda x, y: lax.dot_general(x, y, dims, preferred_element_type=f32)
    return d(ah, bh) + (d(ah, bl) + d(al, bh))


def _softplus(x):
    return jnp.maximum(x, 0.0) + jnp.log1p(jnp.exp(-jnp.abs(x)))


def _sigmoid(x):
    return 1.0 / (1.0 + jnp.exp(-x))


def _silu(x):
    return x * _sigmoid(x)


def _rms(x):
    return x * lax.rsqrt(jnp.mean(x * x, axis=-1, keepdims=True) + EPS)


def _rows(parts):
    return parts[0] if len(parts) == 1 else jnp.concatenate(parts, axis=0)


def _inproj_kernel(x_ref, nw_ref, w_ref, o_ref, xn_ref):
    @pl.when(pl.program_id(1) == 0)
    def _():
        xn_ref[...] = (_rms(x_ref[...]) * nw_ref[...]).astype(bf16)

    o_ref[...] = jnp.dot(xn_ref[...], w_ref[...], preferred_element_type=f32)


def _inproj(x, nw, w, layer, tm, tn):
    m = x.shape[0]
    return pl.pallas_call(
        _inproj_kernel,
        grid=(pl.cdiv(m, tm), N_PACK // tn),
        in_specs=[
            pl.BlockSpec((tm, D_MODEL), lambda i, j: (i, 0)),
            pl.BlockSpec((None, 1, D_MODEL), lambda i, j: (layer, 0, 0)),
            pl.BlockSpec((None, D_MODEL, tn), lambda i, j: (layer, 0, j)),
        ],
        out_specs=pl.BlockSpec((tm, tn), lambda i, j: (i, j)),
        out_shape=jax.ShapeDtypeStruct((m, N_PACK), f32),
        scratch_shapes=[pltpu.VMEM((tm, D_MODEL), bf16)],
        compiler_params=pltpu.CompilerParams(
            dimension_semantics=("arbitrary", "arbitrary"), vmem_limit_bytes=VMEM_LIMIT),
        name="inproj",
    )(x, nw, w)


def _outproj_kernel(final, m_ref, w_ref, x_ref, fn_ref, o_ref):
    y = x_ref[...] + jnp.dot(m_ref[...], w_ref[...], preferred_element_type=f32)
    if final:
        y = _rms(y) * fn_ref[...]
    o_ref[...] = y


def _outproj(mixed, w, layer, x, fn, final, tm):
    m = x.shape[0]
    return pl.pallas_call(
        functools.partial(_outproj_kernel, final),
        grid=(pl.cdiv(m, tm),),
        in_specs=[
            pl.BlockSpec((tm, D_MODEL), lambda i: (i, 0)),
            pl.BlockSpec((None, D_MODEL, D_MODEL), lambda i: (layer, 0, 0)),
            pl.BlockSpec((tm, D_MODEL), lambda i: (i, 0)),
            pl.BlockSpec((1, D_MODEL), lambda i: (0, 0)),
        ],
        out_specs=pl.BlockSpec((tm, D_MODEL), lambda i: (i, 0)),
        out_shape=jax.ShapeDtypeStruct((m, D_MODEL), f32),
        compiler_params=pltpu.CompilerParams(
            dimension_semantics=("arbitrary",), vmem_limit_bytes=VMEM_LIMIT),
        name="outproj",
    )(mixed, w, x, fn)


class _Cfg:
    def __init__(self, rows, seg, voff, vlen, nc, ng, gblocks, embedded, blk_off):
        self.rows, self.seg, self.voff, self.vlen = rows, seg, voff, vlen
        self.nc, self.ng, self.gblocks, self.embedded, self.blk_off = nc, ng, gblocks, embedded, blk_off
        self.nseg = rows // seg
        self.nstate = ng * self.nseg
        self.all_valid = (voff == 0 and vlen == seg)


def _tri_inverse(a_list, seg, row, col, eye):
    def blk(s):
        return (row // s) == (col // s)

    b8 = blk(8)
    n = [-jnp.where(b8, a, 0.0) for a in a_list]
    mm = functools.partial(_mm, passes=P_SOLVE)
    n2 = [mm(x, x) for x in n]
    n4 = [mm(x, x) for x in n2]
    t = [mm(eye + x, eye + y) for x, y in zip(n, n2)]
    t = [mm(x, eye + y) for x, y in zip(t, n4)]
    s = 8
    while s < seg:
        mask = blk(2 * s) & jnp.logical_not(blk(s))
        off = [jnp.where(mask, a, 0.0) for a in a_list]
        tb = [mm(x, o) for x, o in zip(t, off)]
        t = [x - mm(y, x) for x, y in zip(t, tb)]
        s *= 2
    return t


def _mixer_kernel(cfg, n_alias, *refs):
    R, SEG, NSEG, NG, NST = cfg.rows, cfg.seg, cfg.nseg, cfg.ng, cfg.nstate
    it = iter(refs)
    proj = next(it)
    if cfg.embedded:
        buf_a, buf_b = next(it), next(it)
    else:
        ic_a, ic_b = next(it), next(it)
    sd_in, h_in, sg_in, sr_in = next(it), next(it), next(it), next(it)
    cos_t, sin_t, intra_t, fs_t, ts_t, cd_t = (next(it) for _ in range(6))
    (conv_a, alogv, dtbv, norm_a, conv_b, cbb, wa, wx, ba, bx, lam, w2p, b2, norm_c) = (
        next(it) for _ in range(14))
    for _ in range(n_alias):
        next(it)
    mixed, sd, conv_a_o, h_o, conv_b_o, sg, sr = (next(it) for _ in range(7))
    xp_a, xp_b, l_a, l_b, l_o = (next(it) for _ in range(5))

    c = pl.program_id(1)
    groups = range(NG)
    heads = range(N_HEADS)
    units = [(g, hd) for g in groups for hd in heads]
    seg_rows = [slice(s * SEG, (s + 1) * SEG) for s in range(NSEG)]

    @pl.when(c == 0)
    def _init():
        shared = sd_in.shape[0] != NST
        for i in range(NST):
            j = 0 if shared else i
            sd[i] = sd_in[j]
            sg[i] = sg_in[j]
            sr[i] = sr_in[j]
        h_o[...] = jnp.broadcast_to(h_in[...], (NST, GROUP_W))
        for g in groups:
            if cfg.embedded:
                xp_a[g, 0:8, :] = jnp.zeros((8, QKV_W), f32)
                xp_b[g, 0:8, :] = jnp.zeros((8, GROUP_W), f32)
            else:
                xp_a[g, 0:8, :] = ic_a[...]
                xp_b[g, 0:8, :] = ic_b[...]
            xp_a[g, 8 + R:16 + R, :] = jnp.zeros((8, QKV_W), f32)
            xp_b[g, 8 + R:16 + R, :] = jnp.zeros((8, GROUP_W), f32)

    row = lax.broadcasted_iota(jnp.int32, (R, R), 0)
    col = lax.broadcasted_iota(jnp.int32, (R, R), 1)
    same = (row // SEG) == (col // SEG)
    incl = same & (col <= row)
    strict = same & (col < row)
    eye = (row == col).astype(f32)
    l_incl = incl.astype(f32)
    m_same = same.astype(f32)
    rmod = lax.broadcasted_iota(jnp.int32, (R, 1), 0) % SEG
    valid = (rmod >= cfg.voff) & (rmod < cfg.voff + cfg.vlen)
    is_hist = rmod < TILE_OFF

    def conv(xp, g, x, buf, w_ref):
        if cfg.embedded:
            x = jnp.where(is_hist, buf[g], x)
        xp[g, 8:8 + R, :] = x
        y = w_ref[CONV_K - 1:CONV_K, :] * x
        for s in range(1, CONV_K):
            y = y + w_ref[CONV_K - 1 - s:CONV_K - s, :] * xp[g, 8 - s:8 - s + R, :]
        return y

    ya = [conv(xp_a, g, proj[g, :, OFF_QKV:OFF_QKV + QKV_W], buf_a if cfg.embedded else None, conv_a)
          for g in groups]
    yb = [conv(xp_b, g, proj[g, :, OFF_XB:OFF_XB + GROUP_W], buf_b if cfg.embedded else None, conv_b)
          + cbb[...] for g in groups]
    for g in groups:
        if cfg.embedded:
            conv_a_o[g] = xp_a[g, 8 + TILE_OFF + 1:8 + TILE_OFF + 1 + R, :]
            conv_b_o[g] = xp_b[g, 8 + TILE_OFF + 1:8 + TILE_OFF + 1 + R, :]
        else:
            xp_a[g, 0:8, :] = xp_a[g, R:R + 8, :]
            xp_b[g, 0:8, :] = xp_b[g, R:R + 8, :]
    if not cfg.embedded:
        @pl.when(c == cfg.nc - 1)
        def _():
            for g in groups:
                conv_a_o[g] = xp_a[g, 5:8, :]
                conv_b_o[g] = xp_b[g, 5:8, :]

    small = [proj[g, :, OFF_SMALL:OFF_SMALL + 128] for g in groups]

    def put(g, k, o):
        gate = proj[g, :, OFF_GATE + k * GROUP_W:OFF_GATE + (k + 1) * GROUP_W]
        mixed[g, :, k * GROUP_W:(k + 1) * GROUP_W] = (o * _silu(gate)).astype(mixed.dtype)

    sp_lam = _softplus(-lam[...])
    r_pre = [[_mm(yb[g][:, n * 128:(n + 1) * 128], wa[n]) for n in heads] for g in groups]
    i_pre = [[_mm(yb[g][:, n * 128:(n + 1) * 128], wx[n]) for n in heads] for g in groups]
    h_all = h_o[...]
    h_cur = {}
    for g, n in units:
        ls = slice(n * 128, (n + 1) * 128)
        x_n = yb[g][:, ls]
        log_a = -LRU_C * _sigmoid(r_pre[g][n] + ba[:, ls]) * sp_lam[:, ls]
        a_t = jnp.exp(log_a)
        b_t = jnp.sqrt(-jnp.tanh(log_a) * (a_t * a_t + 1.0)) * (_sigmoid(i_pre[g][n] + bx[:, ls]) * x_n)
        l_a[g * N_HEADS + n] = a_t
        l_b[g * N_HEADS + n] = b_t
        if not cfg.all_valid:
            l_o[g * N_HEADS + n] = jnp.zeros((R, 128), f32)
        h_cur[g, n] = h_all[g * NSEG:(g + 1) * NSEG, ls]
    for t in range(cfg.vlen):
        idx = pl.ds(cfg.voff + t, 1) if NSEG == 1 else pl.ds(cfg.voff + t, NSEG, stride=SEG)
        for g, n in units:
            k = g * N_HEADS + n
            h_cur[g, n] = l_a[k, idx, :] * h_cur[g, n] + l_b[k, idx, :]
            l_o[k, idx, :] = h_cur[g, n]
    h_o[...] = _rows([jnp.concatenate([h_cur[g, n] for n in heads], axis=1) for g in groups])
    for g in groups:
        put(g, 1, jnp.concatenate([l_o[g * N_HEADS + n] for n in heads], axis=1))

    qkv = [_silu(y) for y in ya]
    g_all, beta_all = [], []
    for g in groups:
        ga = -jnp.exp(alogv[...]) * _softplus(small[g] + dtbv[...])
        be = _sigmoid(small[g])
        if not cfg.all_valid:
            ga = jnp.where(valid, ga, 0.0)
            be = jnp.where(valid, be, 0.0)
        g_all.append(ga)
        beta_all.append(be)
    gcum = [_mm(l_incl, x, passes=6) for x in g_all]
    gtot = [_mm(m_same, x, passes=6) for x in g_all]
    gcum_t = [x.T for x in gcum]

    q_l, k_l, v_l, be_l, gc_l, gt_l, dec_l, eg_l = ([] for _ in range(8))
    for g, hd in units:
        q = qkv[g][:, hd * 128:(hd + 1) * 128]
        k = qkv[g][:, 512 + hd * 128:512 + (hd + 1) * 128]
        q_l.append(q * lax.rsqrt(jnp.sum(q * q, axis=-1, keepdims=True) + EPS) * (DN_DK ** -0.5))
        k_l.append(k * lax.rsqrt(jnp.sum(k * k, axis=-1, keepdims=True) + EPS))
        v_l.append(qkv[g][:, 1024 + hd * 128:1024 + (hd + 1) * 128])
        be_l.append(beta_all[g][:, SM_BETA + hd:SM_BETA + hd + 1])
        gc = gcum[g][:, SM_ALPHA + hd:SM_ALPHA + hd + 1]
        gr = gcum_t[g][SM_ALPHA + hd:SM_ALPHA + hd + 1, :]
        gc_l.append(gc)
        gt_l.append(gtot[g][:, SM_ALPHA + hd:SM_ALPHA + hd + 1])
        dec_l.append(jnp.where(incl, jnp.exp(jnp.where(incl, gc - gr, 0.0)), 0.0))
        eg_l.append(jnp.exp(gc))
    nu = len(units)
    kb_l = [k_l[u] * be_l[u] for u in range(nu)]
    a_l = [jnp.where(strict, _mm(kb_l[u], k_l[u], NT) * dec_l[u], 0.0) for u in range(nu)]
    attn_l = [jnp.where(incl, _mm(q_l[u], k_l[u], NT) * dec_l[u], 0.0) for u in range(nu)]
    t_l = _tri_inverse(a_l, SEG, row, col, eye)
    uw_l = [_mm(t_l[u], jnp.concatenate([v_l[u] * be_l[u], kb_l[u] * eg_l[u]], axis=1), passes=P_SOLVE)
            for u in range(nu)]
    st_l = []
    for u, (g, hd) in enumerate(units):
        qe = q_l[u] * eg_l[u]
        w = uw_l[u][:, 128:]
        st_l.append([_mm(jnp.concatenate([qe[rs], w[rs]], axis=0), sd[g * NSEG + s, hd])
                     for s, rs in enumerate(seg_rows)])
    vn_l = [uw_l[u][:, :128] - _rows([b[SEG:] for b in st_l[u]]) for u in range(nu)]
    o_l = [_rows([b[:SEG] for b in st_l[u]]) + _mm(attn_l[u], vn_l[u]) for u in range(nu)]
    for u, (g, hd) in enumerate(units):
        kd = k_l[u] * jnp.exp(gt_l[u] - gc_l[u])
        for s, rs in enumerate(seg_rows):
            g_last = jnp.exp(gt_l[u][s * SEG:s * SEG + 1, :])
            i = g * NSEG + s
            sd[i, hd] = sd[i, hd] * g_last + _mm(kd[rs], vn_l[u][rs], TN)
    for g in groups:
        put(g, 0, jnp.concatenate(
            [_rms(o_l[g * N_HEADS + hd]) * norm_a[:, hd * 128:(hd + 1) * 128] for hd in heads], axis=1))

    lg = []
    for g in groups:
        x_gate = _mm(small[g], w2p[...]) + b2[...]
        z = -_softplus(-x_gate) * (1.0 / GLA_TAU)
        lg.append(z if cfg.all_valid else jnp.where(valid, z, 0.0))
    bcum = [_mm(l_incl, x, passes=6) for x in lg]
    btot = [_mm(m_same, x, passes=6) for x in lg]
    ones = jnp.ones((SEG, 128), f32)
    dec_s = [[jnp.exp(_mm(lg[g][rs], ones, TN, passes=6)) for rs in seg_rows] for g in groups]
    qe_c, ke_c, kd_c = [], [], []
    for g in groups:
        q_c = proj[g, :, OFF_QC:OFF_QC + 256] * (GLA_DK ** -0.5)
        k_c = proj[g, :, OFF_KC:OFF_KC + 256]
        if not cfg.all_valid:
            k_c = jnp.where(valid, k_c, 0.0)
        qe_c.append(q_c * jnp.exp(bcum[g]))
        ke_c.append(k_c * jnp.exp(jnp.minimum(-bcum[g], 80.0)))
        kd_c.append(k_c * jnp.exp(btot[g] - bcum[g]))
    ks_l = [slice(hd * GLA_DK, (hd + 1) * GLA_DK) for hd in heads]
    vc_l = [proj[g, :, OFF_VC + hd * 128:OFF_VC + (hd + 1) * 128] for g, hd in units]
    attn_l = [jnp.where(incl, _mm(qe_c[g][:, ks_l[hd]], ke_c[g][:, ks_l[hd]], NT), 0.0) for g, hd in units]
    o_l = [_mm(attn_l[u], vc_l[u]) for u in range(nu)]
    for u, (g, hd) in enumerate(units):
        parts = []
        for s, rs in enumerate(seg_rows):
            i = g * NSEG + s
            parts.append(_mm(qe_c[g][rs, ks_l[hd]], sg[i, hd]))
            sg[i, hd] = sg[i, hd] * dec_s[g][s][ks_l[hd], :] + _mm(kd_c[g][rs, ks_l[hd]], vc_l[u][rs], TN)
        o_l[u] = o_l[u] + _rows(parts)
    for g in groups:
        put(g, 2, jnp.concatenate(
            [_rms(o_l[g * N_HEADS + hd]) * norm_c[:, hd * 128:(hd + 1) * 128] for hd in heads], axis=1))

    half = RET_DK // 2
    lane = lax.broadcasted_iota(jnp.int32, (R, 256), 1)
    first_half = (lane % RET_DK) < half

    def rotary(x):
        rot = jnp.where(first_half, pltpu.roll(x, 256 - half, 1), pltpu.roll(x, half, 1))
        return x * cos_t[...] + rot * sin_t[...]

    q_d = [rotary(proj[g, :, OFF_QD:OFF_QD + 256]) for g in groups]
    k_d = [rotary(proj[g, :, OFF_KD:OFF_KD + 256]) * (RET_DK ** -0.5) for g in groups]
    qf = [x * fs_t[...] for x in q_d]
    kt = [x * ts_t[...] for x in k_d]
    ks_l = [slice(hd * RET_DK, (hd + 1) * RET_DK) for hd in heads]
    vd_l = [proj[g, :, OFF_VD + hd * 128:OFF_VD + (hd + 1) * 128] for g, hd in units]
    attn_l = [_mm(q_d[g][:, ks_l[hd]], k_d[g][:, ks_l[hd]], NT) * intra_t[hd] for g, hd in units]
    o_l = [_mm(attn_l[u], vd_l[u]) for u in range(nu)]
    for u, (g, hd) in enumerate(units):
        parts = []
        for s, rs in enumerate(seg_rows):
            i = g * NSEG + s
            parts.append(_mm(qf[g][rs, ks_l[hd]], sr[i, hd]))
            sr[i, hd] = sr[i, hd] * cd_t[hd, 0:1, :] + _mm(kt[g][rs, ks_l[hd]], vd_l[u][rs], TN)
        o_l[u] = o_l[u] + _rows(parts)
    for g in groups:
        put(g, 3, jnp.concatenate([_rms(o_l[g * N_HEADS + hd]) for hd in heads], axis=1))


def _mixer(cfg, layer, proj, hist, states, lin, tables, weights, n_out_layers, lout, prev):
    R, NG, GB, NC, NST = cfg.rows, cfg.ng, cfg.gblocks, cfg.nc, cfg.nstate
    sd_in, h_in, sg_in, sr_in = states
    shared = sd_in.shape[1] != GB * NST
    nin = 1 if shared else NST

    def st(i):
        return 0 if shared else i

    if cfg.embedded:
        hist_specs = [pl.BlockSpec((None, NG, R, QKV_W), lambda gb, c: (layer, gb, c, 0)),
                      pl.BlockSpec((None, NG, R, GROUP_W), lambda gb, c: (layer, gb, c, 0))]
    else:
        hist_specs = [pl.BlockSpec((8, QKV_W), lambda gb, c: (0, 0)),
                      pl.BlockSpec((8, GROUP_W), lambda gb, c: (0, 0))]
    in_specs = [pl.BlockSpec((NG, R, N_PACK), lambda gb, c: (cfg.blk_off + gb, c, 0))] + hist_specs + [
        pl.BlockSpec((None, nin, N_HEADS, DN_DK, HEAD_V), lambda gb, c: (lin, st(gb), 0, 0, 0)),
        pl.BlockSpec((None, None, nin, GROUP_W), lambda gb, c: (lin, st(gb), 0, 0)),
        pl.BlockSpec((None, nin, N_HEADS, GLA_DK, HEAD_V), lambda gb, c: (lin, st(gb), 0, 0, 0)),
        pl.BlockSpec((None, nin, N_HEADS, RET_DK, HEAD_V), lambda gb, c: (lin, st(gb), 0, 0, 0)),
        pl.BlockSpec((R, 256), lambda gb, c: (c, 0)),
        pl.BlockSpec((R, 256), lambda gb, c: (c, 0)),
        pl.BlockSpec((N_HEADS, R, R), lambda gb, c: (0, 0, 0)),
        pl.BlockSpec((R, 256), lambda gb, c: (0, 0)),
        pl.BlockSpec((R, 256), lambda gb, c: (0, 0)),
        pl.BlockSpec((N_HEADS, 8, 128), lambda gb, c: (0, 0, 0)),
    ]
    for w in weights:
        in_specs.append(pl.BlockSpec((None,) + w.shape[1:], lambda gb, c, nd=w.ndim: (layer,) + (0,) * (nd - 1)))
    aliases = {}
    if prev is not None:
        for k, p in enumerate(prev):
            aliases[len(in_specs)] = 1 + k
            in_specs.append(pl.BlockSpec(memory_space=pl.ANY))

    n_seq_rows = NC * R
    conv_rows = n_seq_rows if cfg.embedded else CONV_K - 1
    nl = n_out_layers
    out_shape = [
        jax.ShapeDtypeStruct((GB * NG, n_seq_rows, D_MODEL), bf16),
        jax.ShapeDtypeStruct((nl, GB * NST, N_HEADS, DN_DK, HEAD_V), f32),
        jax.ShapeDtypeStruct((nl, GB * NG, conv_rows, QKV_W), f32),
        jax.ShapeDtypeStruct((nl, GB, NST, GROUP_W), f32),
        jax.ShapeDtypeStruct((nl, GB * NG, conv_rows, GROUP_W), f32),
        jax.ShapeDtypeStruct((nl, GB * NST, N_HEADS, GLA_DK, HEAD_V), f32),
        jax.ShapeDtypeStruct((nl, GB * NST, N_HEADS, RET_DK, HEAD_V), f32),
    ]
    if cfg.embedded:
        conv_specs = [pl.BlockSpec((None, NG, R, QKV_W), lambda gb, c: (lout, gb, c, 0)),
                      pl.BlockSpec((None, NG, R, GROUP_W), lambda gb, c: (lout, gb, c, 0))]
    else:
        conv_specs = [pl.BlockSpec((None, NG, CONV_K - 1, QKV_W), lambda gb, c: (lout, gb, 0, 0)),
                      pl.BlockSpec((None, NG, CONV_K - 1, GROUP_W), lambda gb, c: (lout, gb, 0, 0))]
    out_specs = [
        pl.BlockSpec((NG, R, D_MODEL), lambda gb, c: (gb, c, 0)),
        pl.BlockSpec((None, NST, N_HEADS, DN_DK, HEAD_V), lambda gb, c: (lout, gb, 0, 0, 0)),
        conv_specs[0],
        pl.BlockSpec((None, None, NST, GROUP_W), lambda gb, c: (lout, gb, 0, 0)),
        conv_specs[1],
        pl.BlockSpec((None, NST, N_HEADS, GLA_DK, HEAD_V), lambda gb, c: (lout, gb, 0, 0, 0)),
        pl.BlockSpec((None, NST, N_HEADS, RET_DK, HEAD_V), lambda gb, c: (lout, gb, 0, 0, 0)),
    ]
    scratch = [
        pltpu.VMEM((NG, R + 16, QKV_W), f32),
        pltpu.VMEM((NG, R + 16, GROUP_W), f32),
        pltpu.VMEM((NG * N_HEADS, R, 128), f32),
        pltpu.VMEM((NG * N_HEADS, R, 128), f32),
        pltpu.VMEM((NG * N_HEADS, R, 128), f32),
    ]
    return pl.pallas_call(
        functools.partial(_mixer_kernel, cfg, len(aliases)),
        grid=(GB, NC),
        in_specs=in_specs,
        out_specs=out_specs,
        out_shape=out_shape,
        scratch_shapes=scratch,
        input_output_aliases=aliases,
        compiler_params=pltpu.CompilerParams(
            dimension_semantics=("arbitrary", "arbitrary"), vmem_limit_bytes=VMEM_LIMIT),
        name="mixer_r%d_s%d_g%d" % (R, cfg.seg, NG),
    )(proj, *hist, sd_in, h_in, sg_in, sr_in, *tables, *weights, *(prev or ()))


def _rope_tables(pos):
    half = RET_DK // 2
    freqs = ROPE_BASE ** (-jnp.arange(half, dtype=f32) / half)
    ang = pos.astype(f32)[:, None] * freqs
    cos, sin = jnp.cos(ang), jnp.sin(ang)
    cos_h = jnp.concatenate([cos, cos], axis=1)
    sin_h = jnp.concatenate([-sin, sin], axis=1)
    return jnp.tile(cos_h, (1, N_HEADS)), jnp.tile(sin_h, (1, N_HEADS))


def _ret_tables(rows, seg, voff, vlen):
    log_gamma = jnp.log(1.0 - 2.0 ** (-5.0 - jnp.arange(N_HEADS, dtype=f32)))
    r = jnp.arange(rows)
    p = (r % seg - voff).astype(f32)
    ok = ((r % seg) >= voff) & ((r % seg) < voff + vlen)
    rel = p[:, None] - p[None, :]
    pair = ok[:, None] & ok[None, :] & ((r[:, None] // seg) == (r[None, :] // seg)) & (rel >= 0)
    intra = jnp.where(pair[None], jnp.exp(log_gamma[:, None, None] * jnp.maximum(rel, 0.0)[None]), 0.0)
    from_state = jnp.exp(log_gamma[:, None] * (p + 1.0))
    to_state = jnp.where(ok[None], jnp.exp(log_gamma[:, None] * (vlen - 1.0 - p)), 0.0)
    chunk_decay = jnp.exp(log_gamma * vlen)
    fs = jnp.repeat(from_state.T, RET_DK, axis=1)
    ts = jnp.repeat(to_state.T, RET_DK, axis=1)
    cd = jnp.broadcast_to(chunk_decay[:, None, None], (N_HEADS, 8, 128))
    return intra.astype(f32), fs.astype(f32), ts.astype(f32), cd.astype(f32)


def _pack_w_in(w_in):
    depth = w_in.shape[0]
    o = 0
    sizes = (QKV_W, N_HEADS, N_HEADS, GROUP_W, 256, 256, GROUP_W, GLA_RANK, 256, 256, GROUP_W, D_MODEL)
    offs = []
    for s in sizes:
        offs.append(o)
        o += s
    seg = lambda i: w_in[:, :, offs[i]:offs[i] + sizes[i]].astype(bf16)
    pad = jnp.zeros((depth, D_MODEL, N_PACK - OFF_SMALL - (2 * N_HEADS + GLA_RANK)), bf16)
    return jnp.concatenate(
        [seg(0), seg(3), seg(4), seg(5), seg(6), seg(8), seg(9), seg(10), seg(11),
         seg(1), seg(2), seg(7), pad], axis=-1)


def _lanes(vec, off, width=128):
    return jnp.pad(vec[None, :], ((0, 0), (off, width - off - vec.shape[0])))


def kernel(x_prompt, x_sample, state_delta, state_delta_conv, state_lru, state_lru_conv, state_gla,
           state_ret, meta_tokens, norm_w, w_in, conv_a, a_log, dt_bias, norm_a, conv_b, conv_b_bias,
           lru_wa, lru_ba, lru_wx, lru_bx, lru_lambda, gla_w2, gla_b2, norm_c, w_out, final_norm):
    depth = w_in.shape[0]
    bp, lp = x_prompt.shape[0], x_prompt.shape[1]
    bs, ls = x_sample.shape[0], x_sample.shape[1]
    assert lp % CHUNK == 0 and ls == CONV_K and TILE_OFF + ls <= TILE
    nc_main = lp // CHUNK
    n_tile_rows = bs * TILE
    assert n_tile_rows % DEC_ROWS == 0 and DEC_ROWS % N_META == 0
    n_dec_blocks = n_tile_rows // DEC_ROWS
    seq_per_block = DEC_ROWS // TILE

    w_in_p = _pack_w_in(w_in)
    w_out_b = w_out.astype(bf16)

    h_main = x_prompt.reshape(bp * lp, D_MODEL)
    tiles = jnp.pad(x_sample, ((0, 0), (TILE_OFF, TILE - TILE_OFF - ls), (0, 0)))
    h_small = jnp.concatenate(
        [tiles.reshape(n_tile_rows, D_MODEL), meta_tokens.astype(x_prompt.dtype),
         jnp.zeros((DEC_ROWS - N_META, D_MODEL), x_prompt.dtype)], axis=0)
    n_small = h_small.shape[0]
    tm_small = n_small // 2
    assert tm_small % 8 == 0

    cfg_main = _Cfg(CHUNK, CHUNK, 0, CHUNK, nc_main, bp, 1, False, 0)
    cfg_meta = _Cfg(N_META, N_META, 0, N_META, 1, 1, 1, False, n_tile_rows // N_META)
    cfg_dec = _Cfg(DEC_ROWS, TILE, TILE_OFF, ls, 1, 1, n_dec_blocks, True, 0)

    pos_main = N_META + jnp.arange(lp)
    pos_meta = jnp.arange(N_META)
    pos_dec = jnp.tile(PAST_LEN + jnp.arange(TILE) - TILE_OFF, seq_per_block)
    tab_main = _rope_tables(pos_main) + _ret_tables(CHUNK, CHUNK, 0, CHUNK)
    tab_meta = _rope_tables(pos_meta) + _ret_tables(N_META, N_META, 0, N_META)
    tab_dec = _rope_tables(pos_dec) + _ret_tables(DEC_ROWS, TILE, TILE_OFF, ls)

    zeros_meta = (
        jnp.zeros((1, 1, N_HEADS, DN_DK, HEAD_V), f32), jnp.zeros((1, 1, 1, GROUP_W), f32),
        jnp.zeros((1, 1, N_HEADS, GLA_DK, HEAD_V), f32), jnp.zeros((1, 1, N_HEADS, RET_DK, HEAD_V), f32))
    zero_hist = (jnp.zeros((8, QKV_W), f32), jnp.zeros((8, GROUP_W), f32))

    weights = (
        conv_a,
        jnp.pad(a_log[:, None, :], ((0, 0), (0, 0), (SM_ALPHA, 128 - SM_ALPHA - N_HEADS))),
        jnp.pad(dt_bias[:, None, :], ((0, 0), (0, 0), (SM_ALPHA, 128 - SM_ALPHA - N_HEADS))),
        jnp.tile(norm_a, (1, N_HEADS))[:, None, :], conv_b, conv_b_bias[:, None, :],
        lru_wa, lru_wx, lru_ba[:, None, :], lru_bx[:, None, :], lru_lambda[:, None, :],
        jnp.pad(gla_w2, ((0, 0), (SM_RC, 128 - SM_RC - GLA_RANK), (0, 0))), gla_b2[:, None, :],
        jnp.tile(norm_c, (1, N_HEADS))[:, None, :],
    )
    nw = norm_w[:, None, :]
    fn = final_norm[None]

    pad_tile = ((0, 0), (0, 0), (0, TILE - (CONV_K - 1)), (0, 0))
    hist_dec = (jnp.pad(state_delta_conv, pad_tile).reshape(depth, n_dec_blocks, DEC_ROWS, QKV_W),
                jnp.pad(state_lru_conv, pad_tile).reshape(depth, n_dec_blocks, DEC_ROWS, GROUP_W))
    st_dec = (state_delta, state_lru.reshape(depth, n_dec_blocks, seq_per_block, GROUP_W), state_gla, state_ret)

    p_st, s_st = None, None
    for l in range(depth):
        last = l == depth - 1
        proj_main = _inproj(h_main, nw, w_in_p, l, TM_MAIN_IN, TN_IN)
        proj_small = _inproj(h_small, nw, w_in_p, l, n_small, TN_IN)

        mx_meta, sd_m, ca_m, h_m, cb_m, sg_m, sr_m = _mixer(
            cfg_meta, l, proj_small.reshape(n_small // N_META, N_META, N_PACK), zero_hist, zeros_meta, 0,
            tab_meta, weights, 1, 0, None)
        hist_main = (jnp.pad(ca_m[0, 0], ((8 - (CONV_K - 1), 0), (0, 0))),
                     jnp.pad(cb_m[0, 0], ((8 - (CONV_K - 1), 0), (0, 0))))
        mx_main, *p_st = _mixer(
            cfg_main, l, proj_main.reshape(bp, lp, N_PACK), hist_main, (sd_m, h_m, sg_m, sr_m), 0,
            tab_main, weights, depth, l, p_st)

        mx_dec, *s_st = _mixer(
            cfg_dec, l, proj_small.reshape(n_small // DEC_ROWS, DEC_ROWS, N_PACK), hist_dec, st_dec, l,
            tab_dec, weights, depth, l, s_st)

        h_main = _outproj(mx_main.reshape(bp * lp, D_MODEL), w_out_b, l, h_main, fn, last, TM_MAIN_OUT)
        mx_small = jnp.concatenate(
            [mx_dec.reshape(n_tile_rows, D_MODEL), mx_meta[0],
             jnp.zeros((DEC_ROWS - N_META, D_MODEL), bf16)], axis=0)
        h_small = _outproj(mx_small, w_out_b, l, h_small, fn, last, tm_small)

    y_prompt = h_main.reshape(bp, lp, D_MODEL)
    y_sample = h_small[:n_tile_rows].reshape(bs, TILE, D_MODEL)[:, TILE_OFF:TILE_OFF + ls]
    sd_p, ca_p, h_p, cb_p, sg_p, sr_p = p_st
    sd_s, ca_s, h_s, cb_s, sg_s, sr_s = s_st
    ca_s = ca_s.reshape(depth, bs, TILE, QKV_W)[:, :, :CONV_K - 1]
    cb_s = cb_s.reshape(depth, bs, TILE, GROUP_W)[:, :, :CONV_K - 1]
    return (y_prompt, y_sample,
            sd_p, ca_p, h_p.reshape(depth, bp, GROUP_W), cb_p, sg_p, sr_p,
            sd_s, ca_s, h_s.reshape(depth, bs, GROUP_W), cb_s, sg_s, sr_s)
```

```python
import functools
import itertools

import jax
import jax.numpy as jnp
from jax import lax
from jax.experimental import pallas as pl
from jax.experimental.pallas import tpu as pltpu

f32 = jnp.float32
bf16 = jnp.bfloat16

D_MODEL = 2048
N_META = 16
CONV_K = 4
CHUNK = 64
N_HEADS = 4
HEAD_V = 128
GROUP_W = N_HEADS * HEAD_V
DN_DK = 128
GLA_DK = 64
RET_DK = 64
GLA_RANK = 16
GLA_TAU = 16.0
LRU_C = 8.0
ROPE_BASE = 10000.0
EPS = 1e-6
PAST_LEN = 16384
QKV_W = 3 * N_HEADS * DN_DK

OFF_QKV = 0
OFF_XB = 1536
OFF_QC = 2048
OFF_KC = 2304
OFF_VC = 2560
OFF_QD = 3072
OFF_KD = 3328
OFF_VD = 3584
OFF_GATE = 4096
OFF_SMALL = 6144
N_PACK = 6400
SM_ALPHA = 0
SM_BETA = 4
SM_RC = 8

TILE = 8
TILE_OFF = CONV_K - 1
DEC_ROWS = 64
DEC_NG = 1

VMEM_LIMIT = 52 * 1024 * 1024
TM_MAIN_IN = 1024
TN_IN = 1280
TM_MAIN_OUT = 512
SCAN_STEPS_PER_SLOT = 16

_DONE = object()

NN = (((1,), (0,)), ((), ()))
NT = (((1,), (1,)), ((), ()))
TN = (((0,), (0,)), ((), ()))


def _split(a):
    hi = a.astype(bf16)
    lo = (a - hi.astype(f32)).astype(bf16)
    return hi, lo


P_SOLVE = 3


def _mm(a, b, dims=NN, passes=1):
    if passes == 6:
        return lax.dot_general(a, b, dims, precision=lax.Precision.HIGHEST, preferred_element_type=f32)
    if passes == 1:
        return lax.dot_general(a.astype(bf16), b.astype(bf16), dims, preferred_element_type=f32)
    ah, al = _split(a)
    bh, bl = _split(b)
    d = lambda x, y: lax.dot_general(x, y, dims, preferred_element_type=f32)
    return d(ah, bh) + (d(ah, bl) + d(al, bh))


def _mm_sel(sel, x, dims=NN, sel_first=True):
    sel = sel.astype(bf16)
    x1 = x.astype(bf16)
    r1 = x - x1.astype(f32)
    x2 = r1.astype(bf16)
    x3 = (r1 - x2.astype(f32)).astype(bf16)
    if sel_first:
        d = lambda y: lax.dot_general(sel, y, dims, preferred_element_type=f32)
    else:
        d = lambda y: lax.dot_general(y, sel, dims, preferred_element_type=f32)
    return d(x1) + (d(x2) + d(x3))


def _softplus(x):
    return jnp.maximum(x, 0.0) + jnp.log1p(jnp.exp(-jnp.abs(x)))


def _sigmoid(x):
    return 0.5 * jnp.tanh(0.5 * x) + 0.5


def _silu(x):
    return x * _sigmoid(x)


def _rms(x):
    return x * lax.rsqrt(jnp.mean(x * x, axis=-1, keepdims=True) + EPS)


def _rows(parts):
    return parts[0] if len(parts) == 1 else jnp.concatenate(parts, axis=0)


def _inproj_kernel(x_ref, nw_ref, w_ref, o_ref, xn_ref):
    @pl.when(pl.program_id(1) == 0)
    def _():
        xn_ref[...] = (_rms(x_ref[...]) * nw_ref[...]).astype(bf16)

    o_ref[...] = jnp.dot(xn_ref[...], w_ref[...], preferred_element_type=f32)


def _inproj(x, nw, w, layer, tm, tn):
    m = x.shape[0]
    return pl.pallas_call(
        _inproj_kernel,
        grid=(pl.cdiv(m, tm), N_PACK // tn),
        in_specs=[
            pl.BlockSpec((tm, D_MODEL), lambda i, j: (i, 0)),
            pl.BlockSpec((None, 1, D_MODEL), lambda i, j: (layer, 0, 0)),
            pl.BlockSpec((None, D_MODEL, tn), lambda i, j: (layer, 0, j)),
        ],
        out_specs=pl.BlockSpec((tm, tn), lambda i, j: (i, j)),
        out_shape=jax.ShapeDtypeStruct((m, N_PACK), f32),
        scratch_shapes=[pltpu.VMEM((tm, D_MODEL), bf16)],
        compiler_params=pltpu.CompilerParams(
            dimension_semantics=("arbitrary", "arbitrary"), vmem_limit_bytes=VMEM_LIMIT),
        name="inproj",
    )(x, nw, w)


def _outproj_kernel(final, m_ref, w_ref, x_ref, fn_ref, o_ref):
    y = x_ref[...] + jnp.dot(m_ref[...], w_ref[...], preferred_element_type=f32)
    if final:
        y = _rms(y) * fn_ref[...]
    o_ref[...] = y


def _outproj(mixed, w, layer, x, fn, final, tm):
    m = x.shape[0]
    return pl.pallas_call(
        functools.partial(_outproj_kernel, final),
        grid=(pl.cdiv(m, tm),),
        in_specs=[
            pl.BlockSpec((tm, D_MODEL), lambda i: (i, 0)),
            pl.BlockSpec((None, D_MODEL, D_MODEL), lambda i: (layer, 0, 0)),
            pl.BlockSpec((tm, D_MODEL), lambda i: (i, 0)),
            pl.BlockSpec((1, D_MODEL), lambda i: (0, 0)),
        ],
        out_specs=pl.BlockSpec((tm, D_MODEL), lambda i: (i, 0)),
        out_shape=jax.ShapeDtypeStruct((m, D_MODEL), f32),
        compiler_params=pltpu.CompilerParams(
            dimension_semantics=("arbitrary",), vmem_limit_bytes=VMEM_LIMIT),
        name="outproj",
    )(mixed, w, x, fn)


class _Cfg:
    def __init__(self, rows, seg, voff, vlen, nc, ng, gblocks, embedded, blk_off):
        self.rows, self.seg, self.voff, self.vlen = rows, seg, voff, vlen
        self.nc, self.ng, self.gblocks, self.embedded, self.blk_off = nc, ng, gblocks, embedded, blk_off
        self.nseg = rows // seg
        self.nstate = ng * self.nseg
        self.all_valid = (voff == 0 and vlen == seg)


def _tri_inverse(a_list, seg, row, col, eye):
    def blk(s):
        return (row // s) == (col // s)

    b8 = blk(8)
    n = [-jnp.where(b8, a, 0.0) for a in a_list]
    mm = _mm
    n2 = [mm(x, x) for x in n]
    yield
    n4 = [mm(x, x) for x in n2]
    t = [mm(eye + x, eye + y) for x, y in zip(n, n2)]
    yield
    t = [mm(x, eye + y) for x, y in zip(t, n4)]
    s = 8
    while s < seg:
        yield
        mask = blk(2 * s) & jnp.logical_not(blk(s))
        off = [jnp.where(mask, a, 0.0) for a in a_list]
        tb = [mm(x, o) for x, o in zip(t, off)]
        yield
        t = [x - mm(y, x) for x, y in zip(t, tb)]
        s *= 2
    yield
    resid = [eye - x - _mm(a, x, passes=P_SOLVE) for a, x in zip(a_list, t)]
    yield
    return [x + mm(x, r) for x, r in zip(t, resid)]


def _mixer_kernel(cfg, n_alias, *refs):
    R, SEG, NSEG, NG, NST = cfg.rows, cfg.seg, cfg.nseg, cfg.ng, cfg.nstate
    it = iter(refs)
    proj = next(it)
    if cfg.embedded:
        buf_a, buf_b = next(it), next(it)
    else:
        ic_a, ic_b = next(it), next(it)
    sd_in, h_in, sg_in, sr_in = next(it), next(it), next(it), next(it)
    cos_t, sin_t, intra_t, fs_t, ts_t, cd_t = (next(it) for _ in range(6))
    (conv_a, alogv, dtbv, norm_a, conv_b, cbb, wa, wx, ba, bx, lam, w2p, b2, norm_c) = (
        next(it) for _ in range(14))
    for _ in range(n_alias):
        next(it)
    mixed, sd, conv_a_o, h_o, conv_b_o, sg, sr = (next(it) for _ in range(7))
    xp_a, xp_b, l_a, l_b, l_o = (next(it) for _ in range(5))

    c = pl.program_id(1)
    groups = range(NG)
    heads = range(N_HEADS)
    units = [(g, hd) for g in groups for hd in heads]
    nu = len(units)
    seg_rows = [slice(s * SEG, (s + 1) * SEG) for s in range(NSEG)]

    @pl.when(c == 0)
    def _init():
        shared = sd_in.shape[0] != NST
        for i in range(NST):
            j = 0 if shared else i
            sd[i] = sd_in[j]
            sg[i] = sg_in[j]
            sr[i] = sr_in[j]
        h_o[...] = jnp.broadcast_to(h_in[...], (NST, GROUP_W))
        for g in groups:
            if cfg.embedded:
                xp_a[g, 0:8, :] = jnp.zeros((8, QKV_W), f32)
                xp_b[g, 0:8, :] = jnp.zeros((8, GROUP_W), f32)
            else:
                xp_a[g, 0:8, :] = ic_a[...]
                xp_b[g, 0:8, :] = ic_b[...]
            xp_a[g, 8 + R:16 + R, :] = jnp.zeros((8, QKV_W), f32)
            xp_b[g, 8 + R:16 + R, :] = jnp.zeros((8, GROUP_W), f32)

    row = lax.broadcasted_iota(jnp.int32, (R, R), 0)
    col = lax.broadcasted_iota(jnp.int32, (R, R), 1)
    same = (row // SEG) == (col // SEG)
    incl = same & (col <= row)
    strict = same & (col < row)
    eye = (row == col).astype(f32)
    l_incl = incl.astype(f32)
    m_same = same.astype(f32)
    rmod = lax.broadcasted_iota(jnp.int32, (R, 1), 0) % SEG
    valid = (rmod >= cfg.voff) & (rmod < cfg.voff + cfg.vlen)
    is_hist = rmod < TILE_OFF

    def conv(xp, g, x, buf, w_ref):
        if cfg.embedded:
            x = jnp.where(is_hist, buf[g], x)
        xp[g, 8:8 + R, :] = x
        full = xp[g, 0:8 + R, :]
        y = w_ref[CONV_K - 1:CONV_K, :] * x
        for s in range(1, CONV_K):
            y = y + w_ref[CONV_K - 1 - s:CONV_K - s, :] * pltpu.roll(full, s, 0)[8:8 + R]
        return y

    def conv_with_history(xp, x_of, buf, w_ref, hist_out):
        ys = [conv(xp, g, x_of(g), buf, w_ref) for g in groups]
        for g in groups:
            if cfg.embedded:
                hist_out[g] = xp[g, 8 + TILE_OFF + 1:8 + TILE_OFF + 1 + R, :]
            else:
                xp[g, 0:8, :] = xp[g, R:R + 8, :]
        return ys

    small = [proj[g, :, OFF_SMALL:OFF_SMALL + 128] for g in groups]

    def put(g, k, o):
        gate = proj[g, :, OFF_GATE + k * GROUP_W:OFF_GATE + (k + 1) * GROUP_W]
        mixed[g, :, k * GROUP_W:(k + 1) * GROUP_W] = (o * _silu(gate)).astype(mixed.dtype)


    def lru_stream():
        yb = conv_with_history(xp_b, lambda g: proj[g, :, OFF_XB:OFF_XB + GROUP_W],
                               buf_b if cfg.embedded else None, conv_b, conv_b_o)
        yb = [y + cbb[...] for y in yb]
        sp_lam = _softplus(-lam[...])
        r_pre = [[_mm(yb[g][:, n * 128:(n + 1) * 128], wa[n]) for n in heads] for g in groups]
        i_pre = [[_mm(yb[g][:, n * 128:(n + 1) * 128], wx[n]) for n in heads] for g in groups]
        h_all = h_o[...]
        h_cur = {}
        for g, n in units:
            ls = slice(n * 128, (n + 1) * 128)
            x_n = yb[g][:, ls]
            log_a = -LRU_C * _sigmoid(r_pre[g][n] + ba[:, ls]) * sp_lam[:, ls]
            a_t = jnp.exp(log_a)
            b_t = jnp.sqrt(-jnp.tanh(log_a) * (a_t * a_t + 1.0)) * (_sigmoid(i_pre[g][n] + bx[:, ls]) * x_n)
            l_a[g * N_HEADS + n] = a_t
            l_b[g * N_HEADS + n] = b_t
            if not cfg.all_valid:
                l_o[g * N_HEADS + n] = jnp.zeros((R, 128), f32)
            h_cur[g, n] = h_all[g * NSEG:(g + 1) * NSEG, ls]
        yield
        for t in range(cfg.vlen):
            idx = pl.ds(cfg.voff + t, 1) if NSEG == 1 else pl.ds(cfg.voff + t, NSEG, stride=SEG)
            for g, n in units:
                k = g * N_HEADS + n
                h_cur[g, n] = l_a[k, idx, :] * h_cur[g, n] + l_b[k, idx, :]
                l_o[k, idx, :] = h_cur[g, n]
            if t % SCAN_STEPS_PER_SLOT == SCAN_STEPS_PER_SLOT - 1:
                yield
        h_o[...] = _rows([jnp.concatenate([h_cur[g, n] for n in heads], axis=1) for g in groups])
        for g in groups:
            put(g, 1, jnp.concatenate([l_o[g * N_HEADS + n] for n in heads], axis=1))

    def delta_stream():
        ya = conv_with_history(xp_a, lambda g: proj[g, :, OFF_QKV:OFF_QKV + QKV_W],
                               buf_a if cfg.embedded else None, conv_a, conv_a_o)
        qkv = [_silu(y) for y in ya]
        g_all, beta_all = [], []
        for g in groups:
            ga = -jnp.exp(alogv[...]) * _softplus(small[g] + dtbv[...])
            be = _sigmoid(small[g])
            if not cfg.all_valid:
                ga = jnp.where(valid, ga, 0.0)
                be = jnp.where(valid, be, 0.0)
            g_all.append(ga)
            beta_all.append(be)
        gcum = [_mm_sel(l_incl, x) for x in g_all]
        gtot = [_mm_sel(m_same, x) for x in g_all]
        gcum_t = [x.T for x in gcum]

        q_l, k_l, v_l, be_l, gc_l, gt_l, dec_l, eg_l = ([] for _ in range(8))
        for g, hd in units:
            q = qkv[g][:, hd * 128:(hd + 1) * 128]
            k = qkv[g][:, 512 + hd * 128:512 + (hd + 1) * 128]
            q_l.append(q * lax.rsqrt(jnp.sum(q * q, axis=-1, keepdims=True) + EPS) * (DN_DK ** -0.5))
            k_l.append(k * lax.rsqrt(jnp.sum(k * k, axis=-1, keepdims=True) + EPS))
            v_l.append(qkv[g][:, 1024 + hd * 128:1024 + (hd + 1) * 128])
            be_l.append(beta_all[g][:, SM_BETA + hd:SM_BETA + hd + 1])
            gc = gcum[g][:, SM_ALPHA + hd:SM_ALPHA + hd + 1]
            gr = gcum_t[g][SM_ALPHA + hd:SM_ALPHA + hd + 1, :]
            gc_l.append(gc)
            gt_l.append(gtot[g][:, SM_ALPHA + hd:SM_ALPHA + hd + 1])
            dec_l.append(jnp.where(incl, jnp.exp(jnp.where(incl, gc - gr, 0.0)), 0.0))
            eg_l.append(jnp.exp(gc))
        kb_l = [k_l[u] * be_l[u] for u in range(nu)]
        a_l = [jnp.where(strict, _mm(kb_l[u], k_l[u], NT) * dec_l[u], 0.0) for u in range(nu)]
        attn_l = [jnp.where(incl, _mm(q_l[u], k_l[u], NT) * dec_l[u], 0.0) for u in range(nu)]
        t_l = yield from _tri_inverse(a_l, SEG, row, col, eye)
        uw_l = [_mm(t_l[u], jnp.concatenate([v_l[u] * be_l[u], kb_l[u] * eg_l[u]], axis=1), passes=P_SOLVE)
                for u in range(nu)]
        st_l = []
        for u, (g, hd) in enumerate(units):
            qe = q_l[u] * eg_l[u]
            w = uw_l[u][:, 128:]
            st_l.append([_mm(jnp.concatenate([qe[rs], w[rs]], axis=0), sd[g * NSEG + s, hd])
                         for s, rs in enumerate(seg_rows)])
        vn_l = [uw_l[u][:, :128] - _rows([b[SEG:] for b in st_l[u]]) for u in range(nu)]
        o_l = [_rows([b[:SEG] for b in st_l[u]]) + _mm(attn_l[u], vn_l[u]) for u in range(nu)]
        for u, (g, hd) in enumerate(units):
            kd = k_l[u] * jnp.exp(gt_l[u] - gc_l[u])
            for s, rs in enumerate(seg_rows):
                g_last = jnp.exp(gt_l[u][s * SEG:s * SEG + 1, :])
                i = g * NSEG + s
                sd[i, hd] = sd[i, hd] * g_last + _mm(kd[rs], vn_l[u][rs], TN)
        for g in groups:
            put(g, 0, jnp.concatenate(
                [_rms(o_l[g * N_HEADS + hd]) * norm_a[:, hd * 128:(hd + 1) * 128] for hd in heads], axis=1))

    def gla_stream():
        lg = []
        for g in groups:
            x_gate = _mm(small[g], w2p[...]) + b2[...]
            z = -_softplus(-x_gate) * (1.0 / GLA_TAU)
            lg.append(z if cfg.all_valid else jnp.where(valid, z, 0.0))
        bcum = [_mm_sel(l_incl, x) for x in lg]
        btot = [_mm_sel(m_same, x) for x in lg]
        ones = jnp.ones((SEG, 128), f32)
        dec_s = [[jnp.exp(_mm_sel(ones, lg[g][rs], TN, sel_first=False)) for rs in seg_rows]
                 for g in groups]
        qe_c, ke_c, kd_c = [], [], []
        for g in groups:
            q_c = proj[g, :, OFF_QC:OFF_QC + 256] * (GLA_DK ** -0.5)
            k_c = proj[g, :, OFF_KC:OFF_KC + 256]
            if not cfg.all_valid:
                k_c = jnp.where(valid, k_c, 0.0)
            qe_c.append(q_c * jnp.exp(bcum[g]))
            ke_c.append(k_c * jnp.exp(jnp.minimum(-bcum[g], 80.0)))
            kd_c.append(k_c * jnp.exp(btot[g] - bcum[g]))
        yield
        ks_l = [slice(hd * GLA_DK, (hd + 1) * GLA_DK) for hd in heads]
        vc_l = [proj[g, :, OFF_VC + hd * 128:OFF_VC + (hd + 1) * 128] for g, hd in units]
        attn_l = [jnp.where(incl, _mm(qe_c[g][:, ks_l[hd]], ke_c[g][:, ks_l[hd]], NT), 0.0) for g, hd in units]
        o_l = [_mm(attn_l[u], vc_l[u]) for u in range(nu)]
        for u, (g, hd) in enumerate(units):
            parts = []
            for s, rs in enumerate(seg_rows):
                i = g * NSEG + s
                parts.append(_mm(qe_c[g][rs, ks_l[hd]], sg[i, hd]))
                sg[i, hd] = sg[i, hd] * dec_s[g][s][ks_l[hd], :] + _mm(kd_c[g][rs, ks_l[hd]], vc_l[u][rs], TN)
            o_l[u] = o_l[u] + _rows(parts)
        for g in groups:
            put(g, 2, jnp.concatenate(
                [_rms(o_l[g * N_HEADS + hd]) * norm_c[:, hd * 128:(hd + 1) * 128] for hd in heads], axis=1))

    def ret_stream():
        half = RET_DK // 2
        lane = lax.broadcasted_iota(jnp.int32, (R, 256), 1)
        first_half = (lane % RET_DK) < half

        def rotary(x):
            rot = jnp.where(first_half, pltpu.roll(x, 256 - half, 1), pltpu.roll(x, half, 1))
            return x * cos_t[...] + rot * sin_t[...]

        q_d = [rotary(proj[g, :, OFF_QD:OFF_QD + 256]) for g in groups]
        k_d = [rotary(proj[g, :, OFF_KD:OFF_KD + 256]) * (RET_DK ** -0.5) for g in groups]
        qf = [x * fs_t[...] for x in q_d]
        kt = [x * ts_t[...] for x in k_d]
        yield
        ks_l = [slice(hd * RET_DK, (hd + 1) * RET_DK) for hd in heads]
        vd_l = [proj[g, :, OFF_VD + hd * 128:OFF_VD + (hd + 1) * 128] for g, hd in units]
        attn_l = [_mm(q_d[g][:, ks_l[hd]], k_d[g][:, ks_l[hd]], NT) * intra_t[hd] for g, hd in units]
        o_l = [_mm(attn_l[u], vd_l[u]) for u in range(nu)]
        for u, (g, hd) in enumerate(units):
            parts = []
            for s, rs in enumerate(seg_rows):
                i = g * NSEG + s
                parts.append(_mm(qf[g][rs, ks_l[hd]], sr[i, hd]))
                sr[i, hd] = sr[i, hd] * cd_t[hd, 0:1, :] + _mm(kt[g][rs, ks_l[hd]], vd_l[u][rs], TN)
            o_l[u] = o_l[u] + _rows(parts)
        for g in groups:
            put(g, 3, jnp.concatenate([_rms(o_l[g * N_HEADS + hd]) for hd in heads], axis=1))

    others = itertools.chain(lru_stream(), gla_stream(), ret_stream())
    streams = [delta_stream(), others]
    while streams:
        for stream in list(streams):
            if next(stream, _DONE) is _DONE:
                streams.remove(stream)

    if not cfg.embedded:
        @pl.when(c == cfg.nc - 1)
        def _():
            for g in groups:
                conv_a_o[g] = xp_a[g, 5:8, :]
                conv_b_o[g] = xp_b[g, 5:8, :]


def _mixer(cfg, layer, proj, hist, states, lin, tables, weights, n_out_layers, lout, prev):
    R, NG, GB, NC, NST = cfg.rows, cfg.ng, cfg.gblocks, cfg.nc, cfg.nstate
    sd_in, h_in, sg_in, sr_in = states
    shared = sd_in.shape[1] != GB * NST
    nin = 1 if shared else NST

    def st(i):
        return 0 if shared else i

    if cfg.embedded:
        hist_specs = [pl.BlockSpec((None, NG, R, QKV_W), lambda gb, c: (layer, gb, c, 0)),
                      pl.BlockSpec((None, NG, R, GROUP_W), lambda gb, c: (layer, gb, c, 0))]
    else:
        hist_specs = [pl.BlockSpec((8, QKV_W), lambda gb, c: (0, 0)),
                      pl.BlockSpec((8, GROUP_W), lambda gb, c: (0, 0))]
    in_specs = [pl.BlockSpec((NG, R, N_PACK), lambda gb, c: (cfg.blk_off + gb, c, 0))] + hist_specs + [
        pl.BlockSpec((None, nin, N_HEADS, DN_DK, HEAD_V), lambda gb, c: (lin, st(gb), 0, 0, 0)),
        pl.BlockSpec((None, None, nin, GROUP_W), lambda gb, c: (lin, st(gb), 0, 0)),
        pl.BlockSpec((None, nin, N_HEADS, GLA_DK, HEAD_V), lambda gb, c: (lin, st(gb), 0, 0, 0)),
        pl.BlockSpec((None, nin, N_HEADS, RET_DK, HEAD_V), lambda gb, c: (lin, st(gb), 0, 0, 0)),
        pl.BlockSpec((R, 256), lambda gb, c: (c, 0)),
        pl.BlockSpec((R, 256), lambda gb, c: (c, 0)),
        pl.BlockSpec((N_HEADS, R, R), lambda gb, c: (0, 0, 0)),
        pl.BlockSpec((R, 256), lambda gb, c: (0, 0)),
        pl.BlockSpec((R, 256), lambda gb, c: (0, 0)),
        pl.BlockSpec((N_HEADS, 8, 128), lambda gb, c: (0, 0, 0)),
    ]
    for w in weights:
        in_specs.append(pl.BlockSpec((None,) + w.shape[1:], lambda gb, c, nd=w.ndim: (layer,) + (0,) * (nd - 1)))
    aliases = {}
    if prev is not None:
        for k, p in enumerate(prev):
            aliases[len(in_specs)] = 1 + k
            in_specs.append(pl.BlockSpec(memory_space=pl.ANY))

    n_seq_rows = NC * R
    conv_rows = n_seq_rows if cfg.embedded else CONV_K - 1
    nl = n_out_layers
    out_shape = [
        jax.ShapeDtypeStruct((GB * NG, n_seq_rows, D_MODEL), bf16),
        jax.ShapeDtypeStruct((nl, GB * NST, N_HEADS, DN_DK, HEAD_V), f32),
        jax.ShapeDtypeStruct((nl, GB * NG, conv_rows, QKV_W), f32),
        jax.ShapeDtypeStruct((nl, GB, NST, GROUP_W), f32),
        jax.ShapeDtypeStruct((nl, GB * NG, conv_rows, GROUP_W), f32),
        jax.ShapeDtypeStruct((nl, GB * NST, N_HEADS, GLA_DK, HEAD_V), f32),
        jax.ShapeDtypeStruct((nl, GB * NST, N_HEADS, RET_DK, HEAD_V), f32),
    ]
    if cfg.embedded:
        conv_specs = [pl.BlockSpec((None, NG, R, QKV_W), lambda gb, c: (lout, gb, c, 0)),
                      pl.BlockSpec((None, NG, R, GROUP_W), lambda gb, c: (lout, gb, c, 0))]
    else:
        conv_specs = [pl.BlockSpec((None, NG, CONV_K - 1, QKV_W), lambda gb, c: (lout, gb, 0, 0)),
                      pl.BlockSpec((None, NG, CONV_K - 1, GROUP_W), lambda gb, c: (lout, gb, 0, 0))]
    out_specs = [
        pl.BlockSpec((NG, R, D_MODEL), lambda gb, c: (gb, c, 0)),
        pl.BlockSpec((None, NST, N_HEADS, DN_DK, HEAD_V), lambda gb, c: (lout, gb, 0, 0, 0)),
        conv_specs[0],
        pl.BlockSpec((None, None, NST, GROUP_W), lambda gb, c: (lout, gb, 0, 0)),
        conv_specs[1],
        pl.BlockSpec((None, NST, N_HEADS, GLA_DK, HEAD_V), lambda gb, c: (lout, gb, 0, 0, 0)),
        pl.BlockSpec((None, NST, N_HEADS, RET_DK, HEAD_V), lambda gb, c: (lout, gb, 0, 0, 0)),
    ]
    scratch = [
        pltpu.VMEM((NG, R + 16, QKV_W), f32),
        pltpu.VMEM((NG, R + 16, GROUP_W), f32),
        pltpu.VMEM((NG * N_HEADS, R, 128), f32),
        pltpu.VMEM((NG * N_HEADS, R, 128), f32),
        pltpu.VMEM((NG * N_HEADS, R, 128), f32),
    ]
    return pl.pallas_call(
        functools.partial(_mixer_kernel, cfg, len(aliases)),
        grid=(GB, NC),
        in_specs=in_specs,
        out_specs=out_specs,
        out_shape=out_shape,
        scratch_shapes=scratch,
        input_output_aliases=aliases,
        compiler_params=pltpu.CompilerParams(
            dimension_semantics=("arbitrary", "arbitrary"), vmem_limit_bytes=VMEM_LIMIT),
        name="mixer_r%d_s%d_g%d" % (R, cfg.seg, NG),
    )(proj, *hist, sd_in, h_in, sg_in, sr_in, *tables, *weights, *(prev or ()))


def _rope_tables(pos):
    half = RET_DK // 2
    freqs = ROPE_BASE ** (-jnp.arange(half, dtype=f32) / half)
    ang = pos.astype(f32)[:, None] * freqs
    cos, sin = jnp.cos(ang), jnp.sin(ang)
    cos_h = jnp.concatenate([cos, cos], axis=1)
    sin_h = jnp.concatenate([-sin, sin], axis=1)
    return jnp.tile(cos_h, (1, N_HEADS)), jnp.tile(sin_h, (1, N_HEADS))


def _ret_tables(rows, seg, voff, vlen):
    log_gamma = jnp.log(1.0 - 2.0 ** (-5.0 - jnp.arange(N_HEADS, dtype=f32)))
    r = jnp.arange(rows)
    p = (r % seg - voff).astype(f32)
    ok = ((r % seg) >= voff) & ((r % seg) < voff + vlen)
    rel = p[:, None] - p[None, :]
    pair = ok[:, None] & ok[None, :] & ((r[:, None] // seg) == (r[None, :] // seg)) & (rel >= 0)
    intra = jnp.where(pair[None], jnp.exp(log_gamma[:, None, None] * jnp.maximum(rel, 0.0)[None]), 0.0)
    from_state = jnp.exp(log_gamma[:, None] * (p + 1.0))
    to_state = jnp.where(ok[None], jnp.exp(log_gamma[:, None] * (vlen - 1.0 - p)), 0.0)
    chunk_decay = jnp.exp(log_gamma * vlen)
    fs = jnp.repeat(from_state.T, RET_DK, axis=1)
    ts = jnp.repeat(to_state.T, RET_DK, axis=1)
    cd = jnp.broadcast_to(chunk_decay[:, None, None], (N_HEADS, 8, 128))
    return intra.astype(f32), fs.astype(f32), ts.astype(f32), cd.astype(f32)


SRC_AB = QKV_W
SRC_RUN1 = SRC_AB + 2 * N_HEADS
SRC_RC = SRC_RUN1 + (OFF_QD - OFF_XB)
SRC_RUN2 = SRC_RC + GLA_RANK
SRC_W = SRC_RUN2 + (OFF_SMALL - OFF_QD)
PACK_ROWS = 128


def _pack_kernel(x_ref, o_ref):
    x = x_ref[...]
    cast = lambda v: v.astype(bf16)
    o_ref[:, OFF_QKV:OFF_XB] = cast(x[:, 0:QKV_W])
    o_ref[:, OFF_XB:OFF_QD] = cast(x[:, SRC_RUN1:SRC_RC])
    o_ref[:, OFF_QD:OFF_SMALL] = cast(x[:, SRC_RUN2:SRC_W])
    narrow = jnp.concatenate(
        [x[:, SRC_AB:SRC_RUN1], x[:, SRC_RC:SRC_RUN2],
         jnp.zeros((x.shape[0], N_PACK - OFF_SMALL - 2 * N_HEADS - GLA_RANK), x.dtype)], axis=1)
    o_ref[:, OFF_SMALL:N_PACK] = cast(narrow)


def _pack_w_in(w_in):
    depth, d, n = w_in.shape
    assert n == SRC_W and d % PACK_ROWS == 0
    return pl.pallas_call(
        _pack_kernel,
        grid=(depth, d // PACK_ROWS),
        in_specs=[pl.BlockSpec((None, PACK_ROWS, n), lambda l, i: (l, i, 0))],
        out_specs=pl.BlockSpec((None, PACK_ROWS, N_PACK), lambda l, i: (l, i, 0)),
        out_shape=jax.ShapeDtypeStruct((depth, d, N_PACK), bf16),
        compiler_params=pltpu.CompilerParams(dimension_semantics=("arbitrary", "arbitrary")),
        name="pack_w_in",
    )(w_in)


def _lanes(vec, off, width=128):
    return jnp.pad(vec[None, :], ((0, 0), (off, width - off - vec.shape[0])))


def kernel(x_prompt, x_sample, state_delta, state_delta_conv, state_lru, state_lru_conv, state_gla,
           state_ret, meta_tokens, norm_w, w_in, conv_a, a_log, dt_bias, norm_a, conv_b, conv_b_bias,
           lru_wa, lru_ba, lru_wx, lru_bx, lru_lambda, gla_w2, gla_b2, norm_c, w_out, final_norm):
    depth = w_in.shape[0]
    bp, lp = x_prompt.shape[0], x_prompt.shape[1]
    bs, ls = x_sample.shape[0], x_sample.shape[1]
    assert lp % CHUNK == 0 and ls == CONV_K and TILE_OFF + ls <= TILE
    nc_main = lp // CHUNK
    n_tile_rows = bs * TILE
    assert n_tile_rows % DEC_ROWS == 0 and DEC_ROWS % N_META == 0
    n_dec_blocks = n_tile_rows // DEC_ROWS
    seq_per_block = DEC_ROWS // TILE

    w_in_p = _pack_w_in(w_in)
    w_out_b = w_out.astype(bf16)

    h_main = x_prompt.reshape(bp * lp, D_MODEL)
    tiles = jnp.pad(x_sample, ((0, 0), (TILE_OFF, TILE - TILE_OFF - ls), (0, 0)))
    h_small = jnp.concatenate(
        [tiles.reshape(n_tile_rows, D_MODEL), meta_tokens.astype(x_prompt.dtype),
         jnp.zeros((DEC_ROWS - N_META, D_MODEL), x_prompt.dtype)], axis=0)
    n_small = h_small.shape[0]
    tm_small = n_small // 2
    assert tm_small % 8 == 0

    cfg_main = _Cfg(CHUNK, CHUNK, 0, CHUNK, nc_main, bp, 1, False, 0)
    cfg_meta = _Cfg(N_META, N_META, 0, N_META, 1, 1, 1, False, n_tile_rows // N_META)
    assert n_dec_blocks % DEC_NG == 0
    cfg_dec = _Cfg(DEC_ROWS, TILE, TILE_OFF, ls, 1, DEC_NG, n_dec_blocks // DEC_NG, True, 0)

    pos_main = N_META + jnp.arange(lp)
    pos_meta = jnp.arange(N_META)
    pos_dec = jnp.tile(PAST_LEN + jnp.arange(TILE) - TILE_OFF, seq_per_block)
    tab_main = _rope_tables(pos_main) + _ret_tables(CHUNK, CHUNK, 0, CHUNK)
    tab_meta = _rope_tables(pos_meta) + _ret_tables(N_META, N_META, 0, N_META)
    tab_dec = _rope_tables(pos_dec) + _ret_tables(DEC_ROWS, TILE, TILE_OFF, ls)

    zeros_meta = (
        jnp.zeros((1, 1, N_HEADS, DN_DK, HEAD_V), f32), jnp.zeros((1, 1, 1, GROUP_W), f32),
        jnp.zeros((1, 1, N_HEADS, GLA_DK, HEAD_V), f32), jnp.zeros((1, 1, N_HEADS, RET_DK, HEAD_V), f32))
    zero_hist = (jnp.zeros((8, QKV_W), f32), jnp.zeros((8, GROUP_W), f32))

    weights = (
        conv_a,
        jnp.pad(a_log[:, None, :], ((0, 0), (0, 0), (SM_ALPHA, 128 - SM_ALPHA - N_HEADS))),
        jnp.pad(dt_bias[:, None, :], ((0, 0), (0, 0), (SM_ALPHA, 128 - SM_ALPHA - N_HEADS))),
        jnp.tile(norm_a, (1, N_HEADS))[:, None, :], conv_b, conv_b_bias[:, None, :],
        lru_wa, lru_wx, lru_ba[:, None, :], lru_bx[:, None, :], lru_lambda[:, None, :],
        jnp.pad(gla_w2, ((0, 0), (SM_RC, 128 - SM_RC - GLA_RANK), (0, 0))), gla_b2[:, None, :],
        jnp.tile(norm_c, (1, N_HEADS))[:, None, :],
    )
    nw = norm_w[:, None, :]
    fn = final_norm[None]

    pad_tile = ((0, 0), (0, 0), (0, TILE - (CONV_K - 1)), (0, 0))
    hist_dec = (jnp.pad(state_delta_conv, pad_tile).reshape(depth, n_dec_blocks, DEC_ROWS, QKV_W),
                jnp.pad(state_lru_conv, pad_tile).reshape(depth, n_dec_blocks, DEC_ROWS, GROUP_W))
    st_dec = (state_delta, state_lru.reshape(depth, cfg_dec.gblocks, cfg_dec.nstate, GROUP_W), state_gla, state_ret)

    p_st, s_st = None, None
    for l in range(depth):
        last = l == depth - 1
        proj_main = _inproj(h_main, nw, w_in_p, l, TM_MAIN_IN, TN_IN)
        proj_small = _inproj(h_small, nw, w_in_p, l, n_small, TN_IN)

        mx_meta, sd_m, ca_m, h_m, cb_m, sg_m, sr_m = _mixer(
            cfg_meta, l, proj_small.reshape(n_small // N_META, N_META, N_PACK), zero_hist, zeros_meta, 0,
            tab_meta, weights, 1, 0, None)
        hist_main = (jnp.pad(ca_m[0, 0], ((8 - (CONV_K - 1), 0), (0, 0))),
                     jnp.pad(cb_m[0, 0], ((8 - (CONV_K - 1), 0), (0, 0))))
        mx_main, *p_st = _mixer(
            cfg_main, l, proj_main.reshape(bp, lp, N_PACK), hist_main, (sd_m, h_m, sg_m, sr_m), 0,
            tab_main, weights, depth, l, p_st)

        mx_dec, *s_st = _mixer(
            cfg_dec, l, proj_small.reshape(n_small // DEC_ROWS, DEC_ROWS, N_PACK), hist_dec, st_dec, l,
            tab_dec, weights, depth, l, s_st)

        h_main = _outproj(mx_main.reshape(bp * lp, D_MODEL), w_out_b, l, h_main, fn, last, TM_MAIN_OUT)
        mx_small = jnp.concatenate(
            [mx_dec.reshape(n_tile_rows, D_MODEL), mx_meta[0],
             jnp.zeros((DEC_ROWS - N_META, D_MODEL), bf16)], axis=0)
        h_small = _outproj(mx_small, w_out_b, l, h_small, fn, last, tm_small)

    y_prompt = h_main.reshape(bp, lp, D_MODEL)
    y_sample = h_small[:n_tile_rows].reshape(bs, TILE, D_MODEL)[:, TILE_OFF:TILE_OFF + ls]
    sd_p, ca_p, h_p, cb_p, sg_p, sr_p = p_st
    sd_s, ca_s, h_s, cb_s, sg_s, sr_s = s_st
    ca_s = ca_s.reshape(depth, bs, TILE, QKV_W)[:, :, :CONV_K - 1]
    cb_s = cb_s.reshape(depth, bs, TILE, GROUP_W)[:, :, :CONV_K - 1]
    return (y_prompt, y_sample,
            sd_p, ca_p, h_p.reshape(depth, bp, GROUP_W), cb_p, sg_p, sr_p,
            sd_s, ca_s, h_s.reshape(depth, bs, GROUP_W), cb_s, sg_s, sr_s)
```

```python
import functools
import itertools

import jax
import jax.numpy as jnp
from jax import lax
from jax.experimental import pallas as pl
from jax.experimental.pallas import tpu as pltpu

f32 = jnp.float32
bf16 = jnp.bfloat16

D_MODEL = 2048
N_META = 16
CONV_K = 4
CHUNK = 64
N_HEADS = 4
HEAD_V = 128
GROUP_W = N_HEADS * HEAD_V
DN_DK = 128
GLA_DK = 64
RET_DK = 64
GLA_RANK = 16
GLA_TAU = 16.0
LRU_C = 8.0
ROPE_BASE = 10000.0
EPS = 1e-6
PAST_LEN = 16384
QKV_W = 3 * N_HEADS * DN_DK

OFF_QKV = 0
OFF_XB = 1536
OFF_QC = 2048
OFF_KC = 2304
OFF_VC = 2560
OFF_QD = 3072
OFF_KD = 3328
OFF_VD = 3584
OFF_GATE = 4096
OFF_SMALL = 6144
N_PACK = 6400
SM_ALPHA = 0
SM_BETA = 4
SM_RC = 8

TILE = 8
TILE_OFF = CONV_K - 1
DEC_ROWS = 64
DEC_NG = 1

VMEM_LIMIT = 52 * 1024 * 1024
TM_MAIN_IN = 1024
TN_IN = 1280
TM_MAIN_OUT = 512
SCAN_STEPS_PER_SLOT = 16

_DONE = object()

NN = (((1,), (0,)), ((), ()))
NT = (((1,), (1,)), ((), ()))
TN = (((0,), (0,)), ((), ()))


def _split(a):
    hi = a.astype(bf16)
    lo = (a - hi.astype(f32)).astype(bf16)
    return hi, lo


P_SOLVE = 3


def _mm(a, b, dims=NN, passes=1):
    if passes == 6:
        return lax.dot_general(a, b, dims, precision=lax.Precision.HIGHEST, preferred_element_type=f32)
    if passes == 1:
        return lax.dot_general(a.astype(bf16), b.astype(bf16), dims, preferred_element_type=f32)
    ah, al = _split(a)
    bh, bl = _split(b)
    d = lambda x, y: lax.dot_general(x, y, dims, preferred_element_type=f32)
    return d(ah, bh) + (d(ah, bl) + d(al, bh))


def _mm_sel(sel, x, dims=NN, sel_first=True):
    sel = sel.astype(bf16)
    x1 = x.astype(bf16)
    r1 = x - x1.astype(f32)
    x2 = r1.astype(bf16)
    x3 = (r1 - x2.astype(f32)).astype(bf16)
    if sel_first:
        d = lambda y: lax.dot_general(sel, y, dims, preferred_element_type=f32)
    else:
        d = lambda y: lax.dot_general(y, sel, dims, preferred_element_type=f32)
    return d(x1) + (d(x2) + d(x3))


def _softplus(x):
    return jnp.maximum(x, 0.0) + jnp.log1p(jnp.exp(-jnp.abs(x)))


def _sigmoid(x):
    return 0.5 * jnp.tanh(0.5 * x) + 0.5


def _silu(x):
    return x * _sigmoid(x)


def _rms(x):
    return x * lax.rsqrt(jnp.mean(x * x, axis=-1, keepdims=True) + EPS)


def _rows(parts):
    return parts[0] if len(parts) == 1 else jnp.concatenate(parts, axis=0)


def _inproj_kernel(x_ref, nw_ref, w_ref, o_ref, xn_ref):
    @pl.when(pl.program_id(1) == 0)
    def _():
        xn_ref[...] = (_rms(x_ref[...]) * nw_ref[...]).astype(bf16)

    o_ref[...] = lax.dot_general(xn_ref[...], w_ref[...], NT, preferred_element_type=f32)


def _inproj(x, nw, w, layer, tm, tn):
    m = x.shape[0]
    return pl.pallas_call(
        _inproj_kernel,
        grid=(pl.cdiv(m, tm), N_PACK // tn),
        in_specs=[
            pl.BlockSpec((tm, D_MODEL), lambda i, j: (i, 0)),
            pl.BlockSpec((None, 1, D_MODEL), lambda i, j: (layer, 0, 0)),
            pl.BlockSpec((None, tn, D_MODEL), lambda i, j: (layer, j, 0)),
        ],
        out_specs=pl.BlockSpec((tm, tn), lambda i, j: (i, j)),
        out_shape=jax.ShapeDtypeStruct((m, N_PACK), f32),
        scratch_shapes=[pltpu.VMEM((tm, D_MODEL), bf16)],
        compiler_params=pltpu.CompilerParams(
            dimension_semantics=("arbitrary", "arbitrary"), vmem_limit_bytes=VMEM_LIMIT),
        name="inproj",
    )(x, nw, w)


def _outproj_kernel(final, m_ref, w_ref, x_ref, fn_ref, o_ref):
    y = x_ref[...] + jnp.dot(m_ref[...], w_ref[...], preferred_element_type=f32)
    if final:
        y = _rms(y) * fn_ref[...]
    o_ref[...] = y


def _outproj(mixed, w, layer, x, fn, final, tm):
    m = x.shape[0]
    return pl.pallas_call(
        functools.partial(_outproj_kernel, final),
        grid=(pl.cdiv(m, tm),),
        in_specs=[
            pl.BlockSpec((tm, D_MODEL), lambda i: (i, 0)),
            pl.BlockSpec((None, D_MODEL, D_MODEL), lambda i: (layer, 0, 0)),
            pl.BlockSpec((tm, D_MODEL), lambda i: (i, 0)),
            pl.BlockSpec((1, D_MODEL), lambda i: (0, 0)),
        ],
        out_specs=pl.BlockSpec((tm, D_MODEL), lambda i: (i, 0)),
        out_shape=jax.ShapeDtypeStruct((m, D_MODEL), f32),
        compiler_params=pltpu.CompilerParams(
            dimension_semantics=("arbitrary",), vmem_limit_bytes=VMEM_LIMIT),
        name="outproj",
    )(mixed, w, x, fn)


class _Cfg:
    def __init__(self, rows, seg, voff, vlen, nc, ng, gblocks, embedded, blk_off):
        self.rows, self.seg, self.voff, self.vlen = rows, seg, voff, vlen
        self.nc, self.ng, self.gblocks, self.embedded, self.blk_off = nc, ng, gblocks, embedded, blk_off
        self.nseg = rows // seg
        self.nstate = ng * self.nseg
        self.all_valid = (voff == 0 and vlen == seg)


def _tri_inverse(a_list, seg, row, col, eye):
    def blk(s):
        return (row // s) == (col // s)

    b8 = blk(8)
    n = [-jnp.where(b8, a, 0.0) for a in a_list]
    mm = _mm
    n2 = [mm(x, x) for x in n]
    yield
    n4 = [mm(x, x) for x in n2]
    t = [mm(eye + x, eye + y) for x, y in zip(n, n2)]
    yield
    t = [mm(x, eye + y) for x, y in zip(t, n4)]
    s = 8
    while s < seg:
        yield
        mask = blk(2 * s) & jnp.logical_not(blk(s))
        off = [jnp.where(mask, a, 0.0) for a in a_list]
        tb = [mm(x, o) for x, o in zip(t, off)]
        yield
        t = [x - mm(y, x) for x, y in zip(t, tb)]
        s *= 2
    yield
    resid = [eye - x - _mm(a, x, passes=P_SOLVE) for a, x in zip(a_list, t)]
    yield
    return [x + mm(x, r) for x, r in zip(t, resid)]


def _mixer_kernel(cfg, n_alias, *refs):
    R, SEG, NSEG, NG, NST = cfg.rows, cfg.seg, cfg.nseg, cfg.ng, cfg.nstate
    it = iter(refs)
    proj = next(it)
    if cfg.embedded:
        buf_a, buf_b = next(it), next(it)
    else:
        ic_a, ic_b = next(it), next(it)
    sd_in, h_in, sg_in, sr_in = next(it), next(it), next(it), next(it)
    cos_t, sin_t, intra_t, fs_t, ts_t, cd_t = (next(it) for _ in range(6))
    (conv_a, alogv, dtbv, norm_a, conv_b, cbb, wa, wx, ba, bx, lam, w2p, b2, norm_c) = (
        next(it) for _ in range(14))
    for _ in range(n_alias):
        next(it)
    mixed, sd, conv_a_o, h_o, conv_b_o, sg, sr = (next(it) for _ in range(7))
    xp_a, xp_b, l_a, l_b, l_o = (next(it) for _ in range(5))

    c = pl.program_id(1)
    groups = range(NG)
    heads = range(N_HEADS)
    units = [(g, hd) for g in groups for hd in heads]
    nu = len(units)
    seg_rows = [slice(s * SEG, (s + 1) * SEG) for s in range(NSEG)]

    @pl.when(c == 0)
    def _init():
        shared = sd_in.shape[0] != NST
        for i in range(NST):
            j = 0 if shared else i
            sd[i] = sd_in[j]
            sg[i] = sg_in[j]
            sr[i] = sr_in[j]
        h_o[...] = jnp.broadcast_to(h_in[...], (NST, GROUP_W))
        for g in groups:
            if cfg.embedded:
                xp_a[g, 0:8, :] = jnp.zeros((8, QKV_W), f32)
                xp_b[g, 0:8, :] = jnp.zeros((8, GROUP_W), f32)
            else:
                xp_a[g, 0:8, :] = ic_a[...]
                xp_b[g, 0:8, :] = ic_b[...]
            xp_a[g, 8 + R:16 + R, :] = jnp.zeros((8, QKV_W), f32)
            xp_b[g, 8 + R:16 + R, :] = jnp.zeros((8, GROUP_W), f32)

    row = lax.broadcasted_iota(jnp.int32, (R, R), 0)
    col = lax.broadcasted_iota(jnp.int32, (R, R), 1)
    same = (row // SEG) == (col // SEG)
    incl = same & (col <= row)
    strict = same & (col < row)
    eye = (row == col).astype(f32)
    l_incl = incl.astype(f32)
    m_same = same.astype(f32)
    rmod = lax.broadcasted_iota(jnp.int32, (R, 1), 0) % SEG
    valid = (rmod >= cfg.voff) & (rmod < cfg.voff + cfg.vlen)
    is_hist = rmod < TILE_OFF

    def conv(xp, g, x, buf, w_ref):
        if cfg.embedded:
            x = jnp.where(is_hist, buf[g], x)
        xp[g, 8:8 + R, :] = x
        full = xp[g, 0:8 + R, :]
        y = w_ref[CONV_K - 1:CONV_K, :] * x
        for s in range(1, CONV_K):
            y = y + w_ref[CONV_K - 1 - s:CONV_K - s, :] * pltpu.roll(full, s, 0)[8:8 + R]
        return y

    def conv_with_history(xp, x_of, buf, w_ref, hist_out):
        ys = [conv(xp, g, x_of(g), buf, w_ref) for g in groups]
        for g in groups:
            if cfg.embedded:
                hist_out[g] = xp[g, 8 + TILE_OFF + 1:8 + TILE_OFF + 1 + R, :]
            else:
                xp[g, 0:8, :] = xp[g, R:R + 8, :]
        return ys

    small = [proj[g, :, OFF_SMALL:OFF_SMALL + 128] for g in groups]

    def put(g, k, o):
        gate = proj[g, :, OFF_GATE + k * GROUP_W:OFF_GATE + (k + 1) * GROUP_W]
        mixed[g, :, k * GROUP_W:(k + 1) * GROUP_W] = (o * _silu(gate)).astype(mixed.dtype)


    def lru_stream():
        yb = conv_with_history(xp_b, lambda g: proj[g, :, OFF_XB:OFF_XB + GROUP_W],
                               buf_b if cfg.embedded else None, conv_b, conv_b_o)
        yb = [y + cbb[...] for y in yb]
        sp_lam = _softplus(-lam[...])
        r_pre = [[_mm(yb[g][:, n * 128:(n + 1) * 128], wa[n]) for n in heads] for g in groups]
        i_pre = [[_mm(yb[g][:, n * 128:(n + 1) * 128], wx[n]) for n in heads] for g in groups]
        h_all = h_o[...]
        h_cur = {}
        for g, n in units:
            ls = slice(n * 128, (n + 1) * 128)
            x_n = yb[g][:, ls]
            log_a = -LRU_C * _sigmoid(r_pre[g][n] + ba[:, ls]) * sp_lam[:, ls]
            a_t = jnp.exp(log_a)
            b_t = jnp.sqrt(-jnp.tanh(log_a) * (a_t * a_t + 1.0)) * (_sigmoid(i_pre[g][n] + bx[:, ls]) * x_n)
            l_a[g * N_HEADS + n] = a_t
            l_b[g * N_HEADS + n] = b_t
            if not cfg.all_valid:
                l_o[g * N_HEADS + n] = jnp.zeros((R, 128), f32)
            h_cur[g, n] = h_all[g * NSEG:(g + 1) * NSEG, ls]
        yield
        for t in range(cfg.vlen):
            idx = pl.ds(cfg.voff + t, 1) if NSEG == 1 else pl.ds(cfg.voff + t, NSEG, stride=SEG)
            for g, n in units:
                k = g * N_HEADS + n
                h_cur[g, n] = l_a[k, idx, :] * h_cur[g, n] + l_b[k, idx, :]
                l_o[k, idx, :] = h_cur[g, n]
            if t % SCAN_STEPS_PER_SLOT == SCAN_STEPS_PER_SLOT - 1:
                yield
        h_o[...] = _rows([jnp.concatenate([h_cur[g, n] for n in heads], axis=1) for g in groups])
        for g in groups:
            put(g, 1, jnp.concatenate([l_o[g * N_HEADS + n] for n in heads], axis=1))

    def delta_stream():
        ya = conv_with_history(xp_a, lambda g: proj[g, :, OFF_QKV:OFF_QKV + QKV_W],
                               buf_a if cfg.embedded else None, conv_a, conv_a_o)
        qkv = [_silu(y) for y in ya]
        g_all, beta_all = [], []
        for g in groups:
            ga = -jnp.exp(alogv[...]) * _softplus(small[g] + dtbv[...])
            be = _sigmoid(small[g])
            if not cfg.all_valid:
                ga = jnp.where(valid, ga, 0.0)
                be = jnp.where(valid, be, 0.0)
            g_all.append(ga)
            beta_all.append(be)
        gcum = [_mm_sel(l_incl, x) for x in g_all]
        gtot = [_mm_sel(m_same, x) for x in g_all]
        gcum_t = [x.T for x in gcum]

        q_l, k_l, v_l, be_l, gc_l, gt_l, dec_l, eg_l = ([] for _ in range(8))
        for g, hd in units:
            q = qkv[g][:, hd * 128:(hd + 1) * 128]
            k = qkv[g][:, 512 + hd * 128:512 + (hd + 1) * 128]
            q_l.append(q * lax.rsqrt(jnp.sum(q * q, axis=-1, keepdims=True) + EPS) * (DN_DK ** -0.5))
            k_l.append(k * lax.rsqrt(jnp.sum(k * k, axis=-1, keepdims=True) + EPS))
            v_l.append(qkv[g][:, 1024 + hd * 128:1024 + (hd + 1) * 128])
            be_l.append(beta_all[g][:, SM_BETA + hd:SM_BETA + hd + 1])
            gc = gcum[g][:, SM_ALPHA + hd:SM_ALPHA + hd + 1]
            gr = gcum_t[g][SM_ALPHA + hd:SM_ALPHA + hd + 1, :]
            gc_l.append(gc)
            gt_l.append(gtot[g][:, SM_ALPHA + hd:SM_ALPHA + hd + 1])
            dec_l.append(jnp.where(incl, jnp.exp(jnp.where(incl, gc - gr, 0.0)), 0.0))
            eg_l.append(jnp.exp(gc))
        kb_l = [k_l[u] * be_l[u] for u in range(nu)]
        a_l = [jnp.where(strict, _mm(kb_l[u], k_l[u], NT) * dec_l[u], 0.0) for u in range(nu)]
        attn_l = [jnp.where(incl, _mm(q_l[u], k_l[u], NT) * dec_l[u], 0.0) for u in range(nu)]
        t_l = yield from _tri_inverse(a_l, SEG, row, col, eye)
        uw_l = [_mm(t_l[u], jnp.concatenate([v_l[u] * be_l[u], kb_l[u] * eg_l[u]], axis=1), passes=P_SOLVE)
                for u in range(nu)]
        st_l = []
        for u, (g, hd) in enumerate(units):
            qe = q_l[u] * eg_l[u]
            w = uw_l[u][:, 128:]
            st_l.append([_mm(jnp.concatenate([qe[rs], w[rs]], axis=0), sd[g * NSEG + s, hd])
                         for s, rs in enumerate(seg_rows)])
        vn_l = [uw_l[u][:, :128] - _rows([b[SEG:] for b in st_l[u]]) for u in range(nu)]
        o_l = [_rows([b[:SEG] for b in st_l[u]]) + _mm(attn_l[u], vn_l[u]) for u in range(nu)]
        for u, (g, hd) in enumerate(units):
            kd = k_l[u] * jnp.exp(gt_l[u] - gc_l[u])
            for s, rs in enumerate(seg_rows):
                g_last = jnp.exp(gt_l[u][s * SEG:s * SEG + 1, :])
                i = g * NSEG + s
                sd[i, hd] = sd[i, hd] * g_last + _mm(kd[rs], vn_l[u][rs], TN)
        for g in groups:
            put(g, 0, jnp.concatenate(
                [_rms(o_l[g * N_HEADS + hd]) * norm_a[:, hd * 128:(hd + 1) * 128] for hd in heads], axis=1))

    def gla_stream():
        lg = []
        for g in groups:
            x_gate = _mm(small[g], w2p[...]) + b2[...]
            z = -_softplus(-x_gate) * (1.0 / GLA_TAU)
            lg.append(z if cfg.all_valid else jnp.where(valid, z, 0.0))
        bcum = [_mm_sel(l_incl, x) for x in lg]
        btot = [_mm_sel(m_same, x) for x in lg]
        ones = jnp.ones((SEG, 128), f32)
        dec_s = [[jnp.exp(_mm_sel(ones, lg[g][rs], TN, sel_first=False)) for rs in seg_rows]
                 for g in groups]
        qe_c, ke_c, kd_c = [], [], []
        for g in groups:
            q_c = proj[g, :, OFF_QC:OFF_QC + 256] * (GLA_DK ** -0.5)
            k_c = proj[g, :, OFF_KC:OFF_KC + 256]
            if not cfg.all_valid:
                k_c = jnp.where(valid, k_c, 0.0)
            qe_c.append(q_c * jnp.exp(bcum[g]))
            ke_c.append(k_c * jnp.exp(jnp.minimum(-bcum[g], 80.0)))
            kd_c.append(k_c * jnp.exp(btot[g] - bcum[g]))
        yield
        ks_l = [slice(hd * GLA_DK, (hd + 1) * GLA_DK) for hd in heads]
        vc_l = [proj[g, :, OFF_VC + hd * 128:OFF_VC + (hd + 1) * 128] for g, hd in units]
        attn_l = [jnp.where(incl, _mm(qe_c[g][:, ks_l[hd]], ke_c[g][:, ks_l[hd]], NT), 0.0) for g, hd in units]
        o_l = [_mm(attn_l[u], vc_l[u]) for u in range(nu)]
        for u, (g, hd) in enumerate(units):
            parts = []
            for s, rs in enumerate(seg_rows):
                i = g * NSEG + s
                parts.append(_mm(qe_c[g][rs, ks_l[hd]], sg[i, hd]))
                sg[i, hd] = sg[i, hd] * dec_s[g][s][ks_l[hd], :] + _mm(kd_c[g][rs, ks_l[hd]], vc_l[u][rs], TN)
            o_l[u] = o_l[u] + _rows(parts)
        for g in groups:
            put(g, 2, jnp.concatenate(
                [_rms(o_l[g * N_HEADS + hd]) * norm_c[:, hd * 128:(hd + 1) * 128] for hd in heads], axis=1))

    def ret_stream():
        half = RET_DK // 2
        lane = lax.broadcasted_iota(jnp.int32, (R, 256), 1)
        first_half = (lane % RET_DK) < half

        def rotary(x):
            rot = jnp.where(first_half, pltpu.roll(x, 256 - half, 1), pltpu.roll(x, half, 1))
            return x * cos_t[...] + rot * sin_t[...]

        q_d = [rotary(proj[g, :, OFF_QD:OFF_QD + 256]) for g in groups]
        k_d = [rotary(proj[g, :, OFF_KD:OFF_KD + 256]) * (RET_DK ** -0.5) for g in groups]
        qf = [x * fs_t[...] for x in q_d]
        kt = [x * ts_t[...] for x in k_d]
        yield
        ks_l = [slice(hd * RET_DK, (hd + 1) * RET_DK) for hd in heads]
        vd_l = [proj[g, :, OFF_VD + hd * 128:OFF_VD + (hd + 1) * 128] for g, hd in units]
        attn_l = [_mm(q_d[g][:, ks_l[hd]], k_d[g][:, ks_l[hd]], NT) * intra_t[hd] for g, hd in units]
        o_l = [_mm(attn_l[u], vd_l[u]) for u in range(nu)]
        for u, (g, hd) in enumerate(units):
            parts = []
            for s, rs in enumerate(seg_rows):
                i = g * NSEG + s
                parts.append(_mm(qf[g][rs, ks_l[hd]], sr[i, hd]))
                sr[i, hd] = sr[i, hd] * cd_t[hd, 0:1, :] + _mm(kt[g][rs, ks_l[hd]], vd_l[u][rs], TN)
            o_l[u] = o_l[u] + _rows(parts)
        for g in groups:
            put(g, 3, jnp.concatenate([_rms(o_l[g * N_HEADS + hd]) for hd in heads], axis=1))

    others = itertools.chain(lru_stream(), gla_stream(), ret_stream())
    streams = [delta_stream(), others]
    while streams:
        for stream in list(streams):
            if next(stream, _DONE) is _DONE:
                streams.remove(stream)

    if not cfg.embedded:
        @pl.when(c == cfg.nc - 1)
        def _():
            for g in groups:
                conv_a_o[g] = xp_a[g, 5:8, :]
                conv_b_o[g] = xp_b[g, 5:8, :]


def _mixer(cfg, layer, proj, hist, states, lin, tables, weights, n_out_layers, lout, prev):
    R, NG, GB, NC, NST = cfg.rows, cfg.ng, cfg.gblocks, cfg.nc, cfg.nstate
    sd_in, h_in, sg_in, sr_in = states
    shared = sd_in.shape[1] != GB * NST
    nin = 1 if shared else NST

    def st(i):
        return 0 if shared else i

    if cfg.embedded:
        hist_specs = [pl.BlockSpec((None, NG, R, QKV_W), lambda gb, c: (layer, gb, c, 0)),
                      pl.BlockSpec((None, NG, R, GROUP_W), lambda gb, c: (layer, gb, c, 0))]
    else:
        hist_specs = [pl.BlockSpec((8, QKV_W), lambda gb, c: (0, 0)),
                      pl.BlockSpec((8, GROUP_W), lambda gb, c: (0, 0))]
    in_specs = [pl.BlockSpec((NG, R, N_PACK), lambda gb, c: (cfg.blk_off + gb, c, 0))] + hist_specs + [
        pl.BlockSpec((None, nin, N_HEADS, DN_DK, HEAD_V), lambda gb, c: (lin, st(gb), 0, 0, 0)),
        pl.BlockSpec((None, None, nin, GROUP_W), lambda gb, c: (lin, st(gb), 0, 0)),
        pl.BlockSpec((None, nin, N_HEADS, GLA_DK, HEAD_V), lambda gb, c: (lin, st(gb), 0, 0, 0)),
        pl.BlockSpec((None, nin, N_HEADS, RET_DK, HEAD_V), lambda gb, c: (lin, st(gb), 0, 0, 0)),
        pl.BlockSpec((R, 256), lambda gb, c: (c, 0)),
        pl.BlockSpec((R, 256), lambda gb, c: (c, 0)),
        pl.BlockSpec((N_HEADS, R, R), lambda gb, c: (0, 0, 0)),
        pl.BlockSpec((R, 256), lambda gb, c: (0, 0)),
        pl.BlockSpec((R, 256), lambda gb, c: (0, 0)),
        pl.BlockSpec((N_HEADS, 8, 128), lambda gb, c: (0, 0, 0)),
    ]
    for w in weights:
        in_specs.append(pl.BlockSpec((None,) + w.shape[1:], lambda gb, c, nd=w.ndim: (layer,) + (0,) * (nd - 1)))
    aliases = {}
    if prev is not None:
        for k, p in enumerate(prev):
            aliases[len(in_specs)] = 1 + k
            in_specs.append(pl.BlockSpec(memory_space=pl.ANY))

    n_seq_rows = NC * R
    conv_rows = n_seq_rows if cfg.embedded else CONV_K - 1
    nl = n_out_layers
    out_shape = [
        jax.ShapeDtypeStruct((GB * NG, n_seq_rows, D_MODEL), bf16),
        jax.ShapeDtypeStruct((nl, GB * NST, N_HEADS, DN_DK, HEAD_V), f32),
        jax.ShapeDtypeStruct((nl, GB * NG, conv_rows, QKV_W), f32),
        jax.ShapeDtypeStruct((nl, GB, NST, GROUP_W), f32),
        jax.ShapeDtypeStruct((nl, GB * NG, conv_rows, GROUP_W), f32),
        jax.ShapeDtypeStruct((nl, GB * NST, N_HEADS, GLA_DK, HEAD_V), f32),
        jax.ShapeDtypeStruct((nl, GB * NST, N_HEADS, RET_DK, HEAD_V), f32),
    ]
    if cfg.embedded:
        conv_specs = [pl.BlockSpec((None, NG, R, QKV_W), lambda gb, c: (lout, gb, c, 0)),
                      pl.BlockSpec((None, NG, R, GROUP_W), lambda gb, c: (lout, gb, c, 0))]
    else:
        conv_specs = [pl.BlockSpec((None, NG, CONV_K - 1, QKV_W), lambda gb, c: (lout, gb, 0, 0)),
                      pl.BlockSpec((None, NG, CONV_K - 1, GROUP_W), lambda gb, c: (lout, gb, 0, 0))]
    out_specs = [
        pl.BlockSpec((NG, R, D_MODEL), lambda gb, c: (gb, c, 0)),
        pl.BlockSpec((None, NST, N_HEADS, DN_DK, HEAD_V), lambda gb, c: (lout, gb, 0, 0, 0)),
        conv_specs[0],
        pl.BlockSpec((None, None, NST, GROUP_W), lambda gb, c: (lout, gb, 0, 0)),
        conv_specs[1],
        pl.BlockSpec((None, NST, N_HEADS, GLA_DK, HEAD_V), lambda gb, c: (lout, gb, 0, 0, 0)),
        pl.BlockSpec((None, NST, N_HEADS, RET_DK, HEAD_V), lambda gb, c: (lout, gb, 0, 0, 0)),
    ]
    scratch = [
        pltpu.VMEM((NG, R + 16, QKV_W), f32),
        pltpu.VMEM((NG, R + 16, GROUP_W), f32),
        pltpu.VMEM((NG * N_HEADS, R, 128), f32),
        pltpu.VMEM((NG * N_HEADS, R, 128), f32),
        pltpu.VMEM((NG * N_HEADS, R, 128), f32),
    ]
    return pl.pallas_call(
        functools.partial(_mixer_kernel, cfg, len(aliases)),
        grid=(GB, NC),
        in_specs=in_specs,
        out_specs=out_specs,
        out_shape=out_shape,
        scratch_shapes=scratch,
        input_output_aliases=aliases,
        compiler_params=pltpu.CompilerParams(
            dimension_semantics=("arbitrary", "arbitrary"), vmem_limit_bytes=VMEM_LIMIT),
        name="mixer_r%d_s%d_g%d" % (R, cfg.seg, NG),
    )(proj, *hist, sd_in, h_in, sg_in, sr_in, *tables, *weights, *(prev or ()))


def _rope_tables(pos):
    half = RET_DK // 2
    freqs = ROPE_BASE ** (-jnp.arange(half, dtype=f32) / half)
    ang = pos.astype(f32)[:, None] * freqs
    cos, sin = jnp.cos(ang), jnp.sin(ang)
    cos_h = jnp.concatenate([cos, cos], axis=1)
    sin_h = jnp.concatenate([-sin, sin], axis=1)
    return jnp.tile(cos_h, (1, N_HEADS)), jnp.tile(sin_h, (1, N_HEADS))


def _ret_tables(rows, seg, voff, vlen):
    log_gamma = jnp.log(1.0 - 2.0 ** (-5.0 - jnp.arange(N_HEADS, dtype=f32)))
    r = jnp.arange(rows)
    p = (r % seg - voff).astype(f32)
    ok = ((r % seg) >= voff) & ((r % seg) < voff + vlen)
    rel = p[:, None] - p[None, :]
    pair = ok[:, None] & ok[None, :] & ((r[:, None] // seg) == (r[None, :] // seg)) & (rel >= 0)
    intra = jnp.where(pair[None], jnp.exp(log_gamma[:, None, None] * jnp.maximum(rel, 0.0)[None]), 0.0)
    from_state = jnp.exp(log_gamma[:, None] * (p + 1.0))
    to_state = jnp.where(ok[None], jnp.exp(log_gamma[:, None] * (vlen - 1.0 - p)), 0.0)
    chunk_decay = jnp.exp(log_gamma * vlen)
    fs = jnp.repeat(from_state.T, RET_DK, axis=1)
    ts = jnp.repeat(to_state.T, RET_DK, axis=1)
    cd = jnp.broadcast_to(chunk_decay[:, None, None], (N_HEADS, 8, 128))
    return intra.astype(f32), fs.astype(f32), ts.astype(f32), cd.astype(f32)


SRC_AB = QKV_W
SRC_RUN1 = SRC_AB + 2 * N_HEADS
SRC_RC = SRC_RUN1 + (OFF_QD - OFF_XB)
SRC_RUN2 = SRC_RC + GLA_RANK
SRC_W = SRC_RUN2 + (OFF_SMALL - OFF_QD)
PACK_ROWS = 256
N_WIDE_BLOCKS = OFF_SMALL // PACK_ROWS


def _pack_kernel(src_ref, ab_ref, rc_ref, o_ref):
    i = pl.program_id(1)

    @pl.when(i < N_WIDE_BLOCKS)
    def _():
        o_ref[...] = src_ref[0].astype(bf16)

    @pl.when(i == N_WIDE_BLOCKS)
    def _():
        pad = jnp.zeros((N_PACK - OFF_SMALL - 2 * N_HEADS - GLA_RANK, D_MODEL), f32)
        o_ref[...] = jnp.concatenate([ab_ref[0], rc_ref[0], pad], axis=0).astype(bf16)


def _pack_src_row(i):
    shift = jnp.where(i < OFF_XB // PACK_ROWS, 0,
                      jnp.where(i < OFF_QD // PACK_ROWS, SRC_RUN1 - OFF_XB, SRC_RUN2 - OFF_QD))
    return pl.multiple_of(jnp.minimum(i * PACK_ROWS + shift, SRC_W - PACK_ROWS), 8)


def _pack_w_in(w_in_t):
    depth, n, d = w_in_t.shape
    assert n == SRC_W and d == D_MODEL and OFF_XB % PACK_ROWS == 0 and OFF_QD % PACK_ROWS == 0
    assert N_PACK - OFF_SMALL == PACK_ROWS
    return pl.pallas_call(
        _pack_kernel,
        grid=(depth, N_PACK // PACK_ROWS),
        in_specs=[
            pl.BlockSpec((pl.Element(1), pl.Element(PACK_ROWS), pl.Element(d)),
                         lambda l, i: (l, _pack_src_row(i), 0)),
            pl.BlockSpec((pl.Element(1), pl.Element(2 * N_HEADS), pl.Element(d)), lambda l, i: (l, SRC_AB, 0)),
            pl.BlockSpec((pl.Element(1), pl.Element(GLA_RANK), pl.Element(d)), lambda l, i: (l, SRC_RC, 0)),
        ],
        out_specs=pl.BlockSpec((None, PACK_ROWS, d), lambda l, i: (l, i, 0)),
        out_shape=jax.ShapeDtypeStruct((depth, N_PACK, d), bf16),
        compiler_params=pltpu.CompilerParams(dimension_semantics=("arbitrary", "arbitrary")),
        name="pack_w_in",
    )(w_in_t, w_in_t, w_in_t)


def _lanes(vec, off, width=128):
    return jnp.pad(vec[None, :], ((0, 0), (off, width - off - vec.shape[0])))


def kernel(x_prompt, x_sample, state_delta, state_delta_conv, state_lru, state_lru_conv, state_gla,
           state_ret, meta_tokens, norm_w, w_in, conv_a, a_log, dt_bias, norm_a, conv_b, conv_b_bias,
           lru_wa, lru_ba, lru_wx, lru_bx, lru_lambda, gla_w2, gla_b2, norm_c, w_out, final_norm):
    depth = w_in.shape[0]
    bp, lp = x_prompt.shape[0], x_prompt.shape[1]
    bs, ls = x_sample.shape[0], x_sample.shape[1]
    assert lp % CHUNK == 0 and ls == CONV_K and TILE_OFF + ls <= TILE
    nc_main = lp // CHUNK
    n_tile_rows = bs * TILE
    assert n_tile_rows % DEC_ROWS == 0 and DEC_ROWS % N_META == 0
    n_dec_blocks = n_tile_rows // DEC_ROWS
    seq_per_block = DEC_ROWS // TILE

    w_in_p = _pack_w_in(jnp.swapaxes(w_in, 1, 2))
    w_out_b = w_out.astype(bf16)

    h_main = x_prompt.reshape(bp * lp, D_MODEL)
    tiles = jnp.pad(x_sample, ((0, 0), (TILE_OFF, TILE - TILE_OFF - ls), (0, 0)))
    h_small = jnp.concatenate(
        [tiles.reshape(n_tile_rows, D_MODEL), meta_tokens.astype(x_prompt.dtype),
         jnp.zeros((DEC_ROWS - N_META, D_MODEL), x_prompt.dtype)], axis=0)
    n_small = h_small.shape[0]
    tm_small = n_small // 2
    assert tm_small % 8 == 0

    cfg_main = _Cfg(CHUNK, CHUNK, 0, CHUNK, nc_main, bp, 1, False, 0)
    cfg_meta = _Cfg(N_META, N_META, 0, N_META, 1, 1, 1, False, n_tile_rows // N_META)
    assert n_dec_blocks % DEC_NG == 0
    cfg_dec = _Cfg(DEC_ROWS, TILE, TILE_OFF, ls, 1, DEC_NG, n_dec_blocks // DEC_NG, True, 0)

    pos_main = N_META + jnp.arange(lp)
    pos_meta = jnp.arange(N_META)
    pos_dec = jnp.tile(PAST_LEN + jnp.arange(TILE) - TILE_OFF, seq_per_block)
    tab_main = _rope_tables(pos_main) + _ret_tables(CHUNK, CHUNK, 0, CHUNK)
    tab_meta = _rope_tables(pos_meta) + _ret_tables(N_META, N_META, 0, N_META)
    tab_dec = _rope_tables(pos_dec) + _ret_tables(DEC_ROWS, TILE, TILE_OFF, ls)

    zeros_meta = (
        jnp.zeros((1, 1, N_HEADS, DN_DK, HEAD_V), f32), jnp.zeros((1, 1, 1, GROUP_W), f32),
        jnp.zeros((1, 1, N_HEADS, GLA_DK, HEAD_V), f32), jnp.zeros((1, 1, N_HEADS, RET_DK, HEAD_V), f32))
    zero_hist = (jnp.zeros((8, QKV_W), f32), jnp.zeros((8, GROUP_W), f32))

    weights = (
        conv_a,
        jnp.pad(a_log[:, None, :], ((0, 0), (0, 0), (SM_ALPHA, 128 - SM_ALPHA - N_HEADS))),
        jnp.pad(dt_bias[:, None, :], ((0, 0), (0, 0), (SM_ALPHA, 128 - SM_ALPHA - N_HEADS))),
        jnp.tile(norm_a, (1, N_HEADS))[:, None, :], conv_b, conv_b_bias[:, None, :],
        lru_wa, lru_wx, lru_ba[:, None, :], lru_bx[:, None, :], lru_lambda[:, None, :],
        jnp.pad(gla_w2, ((0, 0), (SM_RC, 128 - SM_RC - GLA_RANK), (0, 0))), gla_b2[:, None, :],
        jnp.tile(norm_c, (1, N_HEADS))[:, None, :],
    )
    nw = norm_w[:, None, :]
    fn = final_norm[None]

    pad_tile = ((0, 0), (0, 0), (0, TILE - (CONV_K - 1)), (0, 0))
    hist_dec = (jnp.pad(state_delta_conv, pad_tile).reshape(depth, n_dec_blocks, DEC_ROWS, QKV_W),
                jnp.pad(state_lru_conv, pad_tile).reshape(depth, n_dec_blocks, DEC_ROWS, GROUP_W))
    st_dec = (state_delta, state_lru.reshape(depth, cfg_dec.gblocks, cfg_dec.nstate, GROUP_W), state_gla, state_ret)

    p_st, s_st = None, None
    for l in range(depth):
        last = l == depth - 1
        proj_main = _inproj(h_main, nw, w_in_p, l, TM_MAIN_IN, TN_IN)
        proj_small = _inproj(h_small, nw, w_in_p, l, n_small, TN_IN)

        mx_meta, sd_m, ca_m, h_m, cb_m, sg_m, sr_m = _mixer(
            cfg_meta, l, proj_small.reshape(n_small // N_META, N_META, N_PACK), zero_hist, zeros_meta, 0,
            tab_meta, weights, 1, 0, None)
        hist_main = (jnp.pad(ca_m[0, 0], ((8 - (CONV_K - 1), 0), (0, 0))),
                     jnp.pad(cb_m[0, 0], ((8 - (CONV_K - 1), 0), (0, 0))))
        mx_main, *p_st = _mixer(
            cfg_main, l, proj_main.reshape(bp, lp, N_PACK), hist_main, (sd_m, h_m, sg_m, sr_m), 0,
            tab_main, weights, depth, l, p_st)

        mx_dec, *s_st = _mixer(
            cfg_dec, l, proj_small.reshape(n_small // DEC_ROWS, DEC_ROWS, N_PACK), hist_dec, st_dec, l,
            tab_dec, weights, depth, l, s_st)

        h_main = _outproj(mx_main.reshape(bp * lp, D_MODEL), w_out_b, l, h_main, fn, last, TM_MAIN_OUT)
        mx_small = jnp.concatenate(
            [mx_dec.reshape(n_tile_rows, D_MODEL), mx_meta[0],
             jnp.zeros((DEC_ROWS - N_META, D_MODEL), bf16)], axis=0)
        h_small = _outproj(mx_small, w_out_b, l, h_small, fn, last, tm_small)

    y_prompt = h_main.reshape(bp, lp, D_MODEL)
    y_sample = h_small[:n_tile_rows].reshape(bs, TILE, D_MODEL)[:, TILE_OFF:TILE_OFF + ls]
    sd_p, ca_p, h_p, cb_p, sg_p, sr_p = p_st
    sd_s, ca_s, h_s, cb_s, sg_s, sr_s = s_st
    ca_s = ca_s.reshape(depth, bs, TILE, QKV_W)[:, :, :CONV_K - 1]
    cb_s = cb_s.reshape(depth, bs, TILE, GROUP_W)[:, :, :CONV_K - 1]
    return (y_prompt, y_sample,
            sd_p, ca_p, h_p.reshape(depth, bp, GROUP_W), cb_p, sg_p, sr_p,
            sd_s, ca_s, h_s.reshape(depth, bs, GROUP_W), cb_s, sg_s, sr_s)
```

```python
import functools
import itertools

import jax
import jax.numpy as jnp
from jax import lax
from jax.experimental import pallas as pl
from jax.experimental.pallas import tpu as pltpu

f32 = jnp.float32
bf16 = jnp.bfloat16

D_MODEL = 2048
N_META = 16
CONV_K = 4
CHUNK = 64
N_HEADS = 4
HEAD_V = 128
GROUP_W = N_HEADS * HEAD_V
DN_DK = 128
GLA_DK = 64
RET_DK = 64
GLA_RANK = 16
GLA_TAU = 16.0
LRU_C = 8.0
ROPE_BASE = 10000.0
EPS = 1e-6
PAST_LEN = 16384
QKV_W = 3 * N_HEADS * DN_DK

OFF_QKV = 0
OFF_XB = 1536
OFF_QC = 2048
OFF_KC = 2304
OFF_VC = 2560
OFF_QD = 3072
OFF_KD = 3328
OFF_VD = 3584
OFF_GATE = 4096
OFF_SMALL = 6144
N_PACK = 6400
SM_ALPHA = 0
SM_BETA = 4
SM_RC = 8

TILE = 8
TILE_OFF = CONV_K - 1
DEC_ROWS = 64
DEC_NG = 1
MAIN_NG = 4

VMEM_LIMIT = 52 * 1024 * 1024
TM_MAIN_IN = 1024
TN_IN = 1280
TM_MAIN_OUT = 512
SCAN_STEPS_PER_SLOT = 16

_DONE = object()
NN = (((1,), (0,)), ((), ()))
NT = (((1,), (1,)), ((), ()))
TN = (((0,), (0,)), ((), ()))


def _split(a):
    hi = a.astype(bf16)
    lo = (a - hi.astype(f32)).astype(bf16)
    return hi, lo


P_SOLVE = 3


def _mm(a, b, dims=NN, passes=1):
    if passes == 6:
        return lax.dot_general(a, b, dims, precision=lax.Precision.HIGHEST, preferred_element_type=f32)
    if passes == 1:
        return lax.dot_general(a.astype(bf16), b.astype(bf16), dims, preferred_element_type=f32)
    ah, al = _split(a)
    bh, bl = _split(b)
    d = lambda x, y: lax.dot_general(x, y, dims, preferred_element_type=f32)
    return d(ah, bh) + (d(ah, bl) + d(al, bh))


def _mm_sel(sel, x, dims=NN, sel_first=True):
    sel = sel.astype(bf16)
    x1 = x.astype(bf16)
    r1 = x - x1.astype(f32)
    x2 = r1.astype(bf16)
    x3 = (r1 - x2.astype(f32)).astype(bf16)
    if sel_first:
        d = lambda y: lax.dot_general(sel, y, dims, preferred_element_type=f32)
    else:
        d = lambda y: lax.dot_general(y, sel, dims, preferred_element_type=f32)
    return d(x1) + (d(x2) + d(x3))


def _softplus(x):
    return jnp.maximum(x, 0.0) + jnp.log1p(jnp.exp(-jnp.abs(x)))


def _sigmoid(x):
    return 0.5 * jnp.tanh(0.5 * x) + 0.5


def _silu(x):
    return x * _sigmoid(x)


def _rms(x):
    return x * lax.rsqrt(jnp.mean(x * x, axis=-1, keepdims=True) + EPS)


def _rows(parts):
    return parts[0] if len(parts) == 1 else jnp.concatenate(parts, axis=0)


def _inproj_kernel(x_ref, nw_ref, w_ref, o_ref, xn_ref):
    @pl.when(pl.program_id(1) == 0)
    def _():
        xn_ref[...] = (_rms(x_ref[...]) * nw_ref[...]).astype(bf16)

    o_ref[...] = lax.dot_general(xn_ref[...], w_ref[...], NT, preferred_element_type=f32)


def _inproj(x, nw, w, layer, tm, tn):
    m = x.shape[0]
    return pl.pallas_call(
        _inproj_kernel,
        grid=(pl.cdiv(m, tm), N_PACK // tn),
        in_specs=[
            pl.BlockSpec((tm, D_MODEL), lambda i, j: (i, 0)),
            pl.BlockSpec((None, 1, D_MODEL), lambda i, j: (layer, 0, 0)),
            pl.BlockSpec((None, tn, D_MODEL), lambda i, j: (layer, j, 0)),
        ],
        out_specs=pl.BlockSpec((tm, tn), lambda i, j: (i, j)),
        out_shape=jax.ShapeDtypeStruct((m, N_PACK), f32),
        scratch_shapes=[pltpu.VMEM((tm, D_MODEL), bf16)],
        compiler_params=pltpu.CompilerParams(
            dimension_semantics=("arbitrary", "arbitrary"), vmem_limit_bytes=VMEM_LIMIT),
        name="inproj",
    )(x, nw, w)


def _outproj_kernel(final, m_ref, w_ref, x_ref, fn_ref, o_ref):
    y = x_ref[...] + jnp.dot(m_ref[...], w_ref[...], preferred_element_type=f32)
    if final:
        y = _rms(y) * fn_ref[...]
    o_ref[...] = y


def _outproj(mixed, w, layer, x, fn, final, tm):
    m = x.shape[0]
    return pl.pallas_call(
        functools.partial(_outproj_kernel, final),
        grid=(pl.cdiv(m, tm),),
        in_specs=[
            pl.BlockSpec((tm, D_MODEL), lambda i: (i, 0)),
            pl.BlockSpec((None, D_MODEL, D_MODEL), lambda i: (layer, 0, 0)),
            pl.BlockSpec((tm, D_MODEL), lambda i: (i, 0)),
            pl.BlockSpec((1, D_MODEL), lambda i: (0, 0)),
        ],
        out_specs=pl.BlockSpec((tm, D_MODEL), lambda i: (i, 0)),
        out_shape=jax.ShapeDtypeStruct((m, D_MODEL), f32),
        compiler_params=pltpu.CompilerParams(
            dimension_semantics=("arbitrary",), vmem_limit_bytes=VMEM_LIMIT),
        name="outproj",
    )(mixed, w, x, fn)


class _Cfg:
    def __init__(self, rows, seg, voff, vlen, nc, ng, gblocks, embedded, blk_off):
        self.rows, self.seg, self.voff, self.vlen = rows, seg, voff, vlen
        self.nc, self.ng, self.gblocks, self.embedded, self.blk_off = nc, ng, gblocks, embedded, blk_off
        self.nseg = rows // seg
        self.nstate = ng * self.nseg
        self.all_valid = (voff == 0 and vlen == seg)


def _tri_inverse(a_list, seg, row, col, eye):
    def blk(s):
        return (row // s) == (col // s)

    b8 = blk(8)
    n = [-jnp.where(b8, a, 0.0) for a in a_list]
    mm = _mm
    n2 = [mm(x, x) for x in n]
    yield
    n4 = [mm(x, x) for x in n2]
    t = [mm(eye + x, eye + y) for x, y in zip(n, n2)]
    yield
    t = [mm(x, eye + y) for x, y in zip(t, n4)]
    s = 8
    while s < seg:
        yield
        mask = blk(2 * s) & jnp.logical_not(blk(s))
        off = [jnp.where(mask, a, 0.0) for a in a_list]
        tb = [mm(x, o) for x, o in zip(t, off)]
        yield
        t = [x - mm(y, x) for x, y in zip(t, tb)]
        s *= 2
    yield
    resid = [eye - x - _mm(a, x, passes=P_SOLVE) for a, x in zip(a_list, t)]
    yield
    return [x + mm(x, r) for x, r in zip(t, resid)]


def _mixer_kernel(cfg, n_alias, *refs):
    R, SEG, NSEG, NG, NST = cfg.rows, cfg.seg, cfg.nseg, cfg.ng, cfg.nstate
    it = iter(refs)
    proj = next(it)
    if cfg.embedded:
        buf_a, buf_b = next(it), next(it)
    else:
        ic_a, ic_b = next(it), next(it)
    sd_in, h_in, sg_in, sr_in = next(it), next(it), next(it), next(it)
    cos_t, sin_t, intra_t, fs_t, ts_t, cd_t = (next(it) for _ in range(6))
    (conv_a, alogv, dtbv, norm_a, conv_b, cbb, wa, wx, ba, bx, lam, w2p, b2, norm_c) = (
        next(it) for _ in range(14))
    for _ in range(n_alias):
        next(it)
    mixed, sd, conv_a_o, h_o, conv_b_o, sg, sr = (next(it) for _ in range(7))
    xp_a, xp_b, l_a, l_b, l_o = (next(it) for _ in range(5))

    c = pl.program_id(1)
    groups = range(NG)
    heads = range(N_HEADS)
    units = [(g, hd) for g in groups for hd in heads]
    nu = len(units)
    seg_rows = [slice(s * SEG, (s + 1) * SEG) for s in range(NSEG)]

    @pl.when(c == 0)
    def _init():
        shared = sd_in.shape[0] != NST
        for i in range(NST):
            j = 0 if shared else i
            sd[i] = sd_in[j]
            sg[i] = sg_in[j]
            sr[i] = sr_in[j]
        h_o[...] = jnp.broadcast_to(h_in[...], (NST, GROUP_W))
        for g in groups:
            if cfg.embedded:
                xp_a[g, 0:8, :] = jnp.zeros((8, QKV_W), f32)
                xp_b[g, 0:8, :] = jnp.zeros((8, GROUP_W), f32)
            else:
                xp_a[g, 0:8, :] = ic_a[...]
                xp_b[g, 0:8, :] = ic_b[...]
            xp_a[g, 8 + R:16 + R, :] = jnp.zeros((8, QKV_W), f32)
            xp_b[g, 8 + R:16 + R, :] = jnp.zeros((8, GROUP_W), f32)

    row = lax.broadcasted_iota(jnp.int32, (R, R), 0)
    col = lax.broadcasted_iota(jnp.int32, (R, R), 1)
    same = (row // SEG) == (col // SEG)
    incl = same & (col <= row)
    strict = same & (col < row)
    eye = (row == col).astype(f32)
    l_incl = incl.astype(f32)
    m_same = same.astype(f32)
    rmod = lax.broadcasted_iota(jnp.int32, (R, 1), 0) % SEG
    valid = (rmod >= cfg.voff) & (rmod < cfg.voff + cfg.vlen)
    is_hist = rmod < TILE_OFF

    def conv(xp, g, x, buf, w_ref):
        if cfg.embedded:
            x = jnp.where(is_hist, buf[g], x)
        xp[g, 8:8 + R, :] = x
        full = xp[g, 0:8 + R, :]
        y = w_ref[CONV_K - 1:CONV_K, :] * x
        for s in range(1, CONV_K):
            y = y + w_ref[CONV_K - 1 - s:CONV_K - s, :] * pltpu.roll(full, s, 0)[8:8 + R]
        return y

    def conv_with_history(xp, x_of, buf, w_ref, hist_out):
        ys = [conv(xp, g, x_of(g), buf, w_ref) for g in groups]
        for g in groups:
            if cfg.embedded:
                hist_out[g] = xp[g, 8 + TILE_OFF + 1:8 + TILE_OFF + 1 + R, :]
            else:
                xp[g, 0:8, :] = xp[g, R:R + 8, :]
        return ys

    small = [proj[g, :, OFF_SMALL:OFF_SMALL + 128] for g in groups]

    def put(g, k, o):
        gate = proj[g, :, OFF_GATE + k * GROUP_W:OFF_GATE + (k + 1) * GROUP_W]
        mixed[g, :, k * GROUP_W:(k + 1) * GROUP_W] = (o * _silu(gate)).astype(mixed.dtype)


    def lru_stream():
        yb = conv_with_history(xp_b, lambda g: proj[g, :, OFF_XB:OFF_XB + GROUP_W],
                               buf_b if cfg.embedded else None, conv_b, conv_b_o)
        yb = [y + cbb[...] for y in yb]
        sp_lam = _softplus(-lam[...])
        r_pre = [[_mm(yb[g][:, n * 128:(n + 1) * 128], wa[n]) for n in heads] for g in groups]
        i_pre = [[_mm(yb[g][:, n * 128:(n + 1) * 128], wx[n]) for n in heads] for g in groups]
        h_all = h_o[...]
        h_cur = {}
        for g, n in units:
            ls = slice(n * 128, (n + 1) * 128)
            x_n = yb[g][:, ls]
            log_a = -LRU_C * _sigmoid(r_pre[g][n] + ba[:, ls]) * sp_lam[:, ls]
            a_t = jnp.exp(log_a)
            b_t = jnp.sqrt(-jnp.tanh(log_a) * (a_t * a_t + 1.0)) * (_sigmoid(i_pre[g][n] + bx[:, ls]) * x_n)
            l_a[g * N_HEADS + n] = a_t
            l_b[g * N_HEADS + n] = b_t
            if not cfg.all_valid:
                l_o[g * N_HEADS + n] = jnp.zeros((R, 128), f32)
            h_cur[g, n] = h_all[g * NSEG:(g + 1) * NSEG, ls]
        yield
        for t in range(cfg.vlen):
            idx = pl.ds(cfg.voff + t, 1) if NSEG == 1 else pl.ds(cfg.voff + t, NSEG, stride=SEG)
            for g, n in units:
                k = g * N_HEADS + n
                h_cur[g, n] = l_a[k, idx, :] * h_cur[g, n] + l_b[k, idx, :]
                l_o[k, idx, :] = h_cur[g, n]
            if t % SCAN_STEPS_PER_SLOT == SCAN_STEPS_PER_SLOT - 1:
                yield
        h_o[...] = _rows([jnp.concatenate([h_cur[g, n] for n in heads], axis=1) for g in groups])
        for g in groups:
            put(g, 1, jnp.concatenate([l_o[g * N_HEADS + n] for n in heads], axis=1))

    def delta_stream():
        ya = conv_with_history(xp_a, lambda g: proj[g, :, OFF_QKV:OFF_QKV + QKV_W],
                               buf_a if cfg.embedded else None, conv_a, conv_a_o)
        qkv = [_silu(y) for y in ya]
        g_all, beta_all = [], []
        for g in groups:
            ga = -jnp.exp(alogv[...]) * _softplus(small[g] + dtbv[...])
            be = _sigmoid(small[g])
            if not cfg.all_valid:
                ga = jnp.where(valid, ga, 0.0)
                be = jnp.where(valid, be, 0.0)
            g_all.append(ga)
            beta_all.append(be)
        gcum = [_mm_sel(l_incl, x) for x in g_all]
        gtot = [_mm_sel(m_same, x) for x in g_all]
        gcum_t = [x.T for x in gcum]

        q_l, k_l, v_l, be_l, gc_l, gt_l, dec_l, eg_l = ([] for _ in range(8))
        for g, hd in units:
            q = qkv[g][:, hd * 128:(hd + 1) * 128]
            k = qkv[g][:, 512 + hd * 128:512 + (hd + 1) * 128]
            q_l.append(q * lax.rsqrt(jnp.sum(q * q, axis=-1, keepdims=True) + EPS) * (DN_DK ** -0.5))
            k_l.append(k * lax.rsqrt(jnp.sum(k * k, axis=-1, keepdims=True) + EPS))
            v_l.append(qkv[g][:, 1024 + hd * 128:1024 + (hd + 1) * 128])
            be_l.append(beta_all[g][:, SM_BETA + hd:SM_BETA + hd + 1])
            gc = gcum[g][:, SM_ALPHA + hd:SM_ALPHA + hd + 1]
            gr = gcum_t[g][SM_ALPHA + hd:SM_ALPHA + hd + 1, :]
            gc_l.append(gc)
            gt_l.append(gtot[g][:, SM_ALPHA + hd:SM_ALPHA + hd + 1])
            dec_l.append(jnp.where(incl, jnp.exp(jnp.where(incl, gc - gr, 0.0)), 0.0))
            eg_l.append(jnp.exp(gc))
        kb_l = [k_l[u] * be_l[u] for u in range(nu)]
        a_l = [jnp.where(strict, _mm(kb_l[u], k_l[u], NT) * dec_l[u], 0.0) for u in range(nu)]
        t_l = yield from _tri_inverse(a_l, SEG, row, col, eye)
        attn_l = [jnp.where(incl, _mm(q_l[u], k_l[u], NT) * dec_l[u], 0.0) for u in range(nu)]
        uw_l = [_mm(t_l[u], jnp.concatenate([v_l[u] * be_l[u], kb_l[u] * eg_l[u]], axis=1), passes=P_SOLVE)
                for u in range(nu)]
        st_l = []
        for u, (g, hd) in enumerate(units):
            qe = q_l[u] * eg_l[u]
            w = uw_l[u][:, 128:]
            st_l.append([_mm(jnp.concatenate([qe[rs], w[rs]], axis=0), sd[g * NSEG + s, hd])
                         for s, rs in enumerate(seg_rows)])
        vn_l = [uw_l[u][:, :128] - _rows([b[SEG:] for b in st_l[u]]) for u in range(nu)]
        o_l = [_rows([b[:SEG] for b in st_l[u]]) + _mm(attn_l[u], vn_l[u]) for u in range(nu)]
        for u, (g, hd) in enumerate(units):
            kd = k_l[u] * jnp.exp(gt_l[u] - gc_l[u])
            for s, rs in enumerate(seg_rows):
                g_last = jnp.exp(gt_l[u][s * SEG:s * SEG + 1, :])
                i = g * NSEG + s
                sd[i, hd] = sd[i, hd] * g_last + _mm(kd[rs], vn_l[u][rs], TN)
        for g in groups:
            put(g, 0, jnp.concatenate(
                [_rms(o_l[g * N_HEADS + hd]) * norm_a[:, hd * 128:(hd + 1) * 128] for hd in heads], axis=1))

    def gla_stream():
        lg = []
        for g in groups:
            x_gate = _mm(small[g], w2p[...]) + b2[...]
            z = -_softplus(-x_gate) * (1.0 / GLA_TAU)
            lg.append(z if cfg.all_valid else jnp.where(valid, z, 0.0))
        bcum = [_mm_sel(l_incl, x) for x in lg]
        btot = [_mm_sel(m_same, x) for x in lg]
        ones = jnp.ones((SEG, 128), f32)
        dec_s = [[jnp.exp(_mm_sel(ones, lg[g][rs], TN, sel_first=False)) for rs in seg_rows]
                 for g in groups]
        q_cs, k_cs, qe_c, kd_c = [], [], [], []
        for g in groups:
            q_c = proj[g, :, OFF_QC:OFF_QC + 256] * (GLA_DK ** -0.5)
            k_c = proj[g, :, OFF_KC:OFF_KC + 256]
            if not cfg.all_valid:
                k_c = jnp.where(valid, k_c, 0.0)
            q_cs.append(q_c)
            k_cs.append(k_c)
            qe_c.append(q_c * jnp.exp(bcum[g]))
            kd_c.append(k_c * jnp.exp(btot[g] - bcum[g]))
        yield
        ks_l = [slice(hd * GLA_DK, (hd + 1) * GLA_DK) for hd in heads]
        vc_l = [proj[g, :, OFF_VC + hd * 128:OFF_VC + (hd + 1) * 128] for g, hd in units]
        halves = [SEG >> (i + 1) for i in range(SEG.bit_length() - 1)]
        nlev = len(halves)
        between = []
        for h in halves:
            ref = (row // (2 * h)) * (2 * h) + h - 1
            between.append(((col > ref) & (col <= row)) | ((col > row) & (col <= ref)))
        between = jnp.concatenate(between, axis=0).astype(f32)
        pair = [((row // (2 * h)) == (col // (2 * h))) & ((row % (2 * h)) >= h) & ((col % (2 * h)) < h)
                for h in halves]
        rep = lambda x: jnp.concatenate([x] * nlev, axis=0)
        e_lv = [jnp.exp(_mm_sel(between, lg[g])) for g in groups]
        q_lv = [rep(q_cs[g]) * e_lv[g] for g in groups]
        k_lv = [rep(k_cs[g]) * e_lv[g] for g in groups]
        yield
        attn_l = [jnp.where(eye > 0.0, _mm(q_cs[g][:, ks_l[hd]], k_cs[g][:, ks_l[hd]], NT), 0.0)
                  for g, hd in units]
        for li in range(nlev):
            lv = slice(li * R, (li + 1) * R)
            attn_l = [jnp.where(pair[li], _mm(q_lv[g][lv, ks_l[hd]], k_lv[g][lv, ks_l[hd]], NT), attn_l[u])
                      for u, (g, hd) in enumerate(units)]
            if li % 2 == 1:
                yield
        yield
        o_l = [_mm(attn_l[u], vc_l[u]) for u in range(nu)]
        for u, (g, hd) in enumerate(units):
            parts = []
            for s, rs in enumerate(seg_rows):
                i = g * NSEG + s
                parts.append(_mm(qe_c[g][rs, ks_l[hd]], sg[i, hd]))
                sg[i, hd] = sg[i, hd] * dec_s[g][s][ks_l[hd], :] + _mm(kd_c[g][rs, ks_l[hd]], vc_l[u][rs], TN)
            o_l[u] = o_l[u] + _rows(parts)
        for g in groups:
            put(g, 2, jnp.concatenate(
                [_rms(o_l[g * N_HEADS + hd]) * norm_c[:, hd * 128:(hd + 1) * 128] for hd in heads], axis=1))

    def ret_stream():
        half = RET_DK // 2
        lane = lax.broadcasted_iota(jnp.int32, (R, 256), 1)
        first_half = (lane % RET_DK) < half

        def rotary(x):
            rot = jnp.where(first_half, pltpu.roll(x, 256 - half, 1), pltpu.roll(x, half, 1))
            return x * cos_t[...] + rot * sin_t[...]

        q_d = [rotary(proj[g, :, OFF_QD:OFF_QD + 256]) for g in groups]
        k_d = [rotary(proj[g, :, OFF_KD:OFF_KD + 256]) * (RET_DK ** -0.5) for g in groups]
        qf = [x * fs_t[...] for x in q_d]
        kt = [x * ts_t[...] for x in k_d]
        yield
        ks_l = [slice(hd * RET_DK, (hd + 1) * RET_DK) for hd in heads]
        vd_l = [proj[g, :, OFF_VD + hd * 128:OFF_VD + (hd + 1) * 128] for g, hd in units]
        attn_l = [_mm(q_d[g][:, ks_l[hd]], k_d[g][:, ks_l[hd]], NT) * intra_t[hd] for g, hd in units]
        o_l = [_mm(attn_l[u], vd_l[u]) for u in range(nu)]
        for u, (g, hd) in enumerate(units):
            parts = []
            for s, rs in enumerate(seg_rows):
                i = g * NSEG + s
                parts.append(_mm(qf[g][rs, ks_l[hd]], sr[i, hd]))
                sr[i, hd] = sr[i, hd] * cd_t[hd, 0:1, :] + _mm(kt[g][rs, ks_l[hd]], vd_l[u][rs], TN)
            o_l[u] = o_l[u] + _rows(parts)
        for g in groups:
            put(g, 3, jnp.concatenate([_rms(o_l[g * N_HEADS + hd]) for hd in heads], axis=1))

    others = itertools.chain(lru_stream(), gla_stream(), ret_stream())
    streams = [delta_stream(), others]
    while streams:
        for stream in list(streams):
            if next(stream, _DONE) is _DONE:
                streams.remove(stream)

    if not cfg.embedded:
        @pl.when(c == cfg.nc - 1)
        def _():
            for g in groups:
                conv_a_o[g] = xp_a[g, 5:8, :]
                conv_b_o[g] = xp_b[g, 5:8, :]


def _mixer(cfg, layer, proj, hist, states, lin, tables, weights, n_out_layers, lout, prev):
    R, NG, GB, NC, NST = cfg.rows, cfg.ng, cfg.gblocks, cfg.nc, cfg.nstate
    sd_in, h_in, sg_in, sr_in = states
    shared = sd_in.shape[1] != GB * NST
    nin = 1 if shared else NST

    def st(i):
        return 0 if shared else i

    if cfg.embedded:
        hist_specs = [pl.BlockSpec((None, NG, R, QKV_W), lambda gb, c: (layer, gb, c, 0)),
                      pl.BlockSpec((None, NG, R, GROUP_W), lambda gb, c: (layer, gb, c, 0))]
    else:
        hist_specs = [pl.BlockSpec((8, QKV_W), lambda gb, c: (0, 0)),
                      pl.BlockSpec((8, GROUP_W), lambda gb, c: (0, 0))]
    in_specs = [pl.BlockSpec((NG, R, N_PACK), lambda gb, c: (cfg.blk_off + gb, c, 0))] + hist_specs + [
        pl.BlockSpec((None, nin, N_HEADS, DN_DK, HEAD_V), lambda gb, c: (lin, st(gb), 0, 0, 0)),
        pl.BlockSpec((None, None, nin, GROUP_W), lambda gb, c: (lin, st(gb), 0, 0)),
        pl.BlockSpec((None, nin, N_HEADS, GLA_DK, HEAD_V), lambda gb, c: (lin, st(gb), 0, 0, 0)),
        pl.BlockSpec((None, nin, N_HEADS, RET_DK, HEAD_V), lambda gb, c: (lin, st(gb), 0, 0, 0)),
        pl.BlockSpec((R, 256), lambda gb, c: (c, 0)),
        pl.BlockSpec((R, 256), lambda gb, c: (c, 0)),
        pl.BlockSpec((N_HEADS, R, R), lambda gb, c: (0, 0, 0)),
        pl.BlockSpec((R, 256), lambda gb, c: (0, 0)),
        pl.BlockSpec((R, 256), lambda gb, c: (0, 0)),
        pl.BlockSpec((N_HEADS, 8, 128), lambda gb, c: (0, 0, 0)),
    ]
    for w in weights:
        in_specs.append(pl.BlockSpec((None,) + w.shape[1:], lambda gb, c, nd=w.ndim: (layer,) + (0,) * (nd - 1)))
    aliases = {}
    if prev is not None:
        for k, p in enumerate(prev):
            aliases[len(in_specs)] = 1 + k
            in_specs.append(pl.BlockSpec(memory_space=pl.ANY))

    n_seq_rows = NC * R
    conv_rows = n_seq_rows if cfg.embedded else CONV_K - 1
    nl = n_out_layers
    out_shape = [
        jax.ShapeDtypeStruct((GB * NG, n_seq_rows, D_MODEL), bf16),
        jax.ShapeDtypeStruct((nl, GB * NST, N_HEADS, DN_DK, HEAD_V), f32),
        jax.ShapeDtypeStruct((nl, GB * NG, conv_rows, QKV_W), f32),
        jax.ShapeDtypeStruct((nl, GB, NST, GROUP_W), f32),
        jax.ShapeDtypeStruct((nl, GB * NG, conv_rows, GROUP_W), f32),
        jax.ShapeDtypeStruct((nl, GB * NST, N_HEADS, GLA_DK, HEAD_V), f32),
        jax.ShapeDtypeStruct((nl, GB * NST, N_HEADS, RET_DK, HEAD_V), f32),
    ]
    if cfg.embedded:
        conv_specs = [pl.BlockSpec((None, NG, R, QKV_W), lambda gb, c: (lout, gb, c, 0)),
                      pl.BlockSpec((None, NG, R, GROUP_W), lambda gb, c: (lout, gb, c, 0))]
    else:
        conv_specs = [pl.BlockSpec((None, NG, CONV_K - 1, QKV_W), lambda gb, c: (lout, gb, 0, 0)),
                      pl.BlockSpec((None, NG, CONV_K - 1, GROUP_W), lambda gb, c: (lout, gb, 0, 0))]
    out_specs = [
        pl.BlockSpec((NG, R, D_MODEL), lambda gb, c: (gb, c, 0)),
        pl.BlockSpec((None, NST, N_HEADS, DN_DK, HEAD_V), lambda gb, c: (lout, gb, 0, 0, 0)),
        conv_specs[0],
        pl.BlockSpec((None, None, NST, GROUP_W), lambda gb, c: (lout, gb, 0, 0)),
        conv_specs[1],
        pl.BlockSpec((None, NST, N_HEADS, GLA_DK, HEAD_V), lambda gb, c: (lout, gb, 0, 0, 0)),
        pl.BlockSpec((None, NST, N_HEADS, RET_DK, HEAD_V), lambda gb, c: (lout, gb, 0, 0, 0)),
    ]
    scratch = [
        pltpu.VMEM((NG, R + 16, QKV_W), f32),
        pltpu.VMEM((NG, R + 16, GROUP_W), f32),
        pltpu.VMEM((NG * N_HEADS, R, 128), f32),
        pltpu.VMEM((NG * N_HEADS, R, 128), f32),
        pltpu.VMEM((NG * N_HEADS, R, 128), f32),
    ]
    return pl.pallas_call(
        functools.partial(_mixer_kernel, cfg, len(aliases)),
        grid=(GB, NC),
        in_specs=in_specs,
        out_specs=out_specs,
        out_shape=out_shape,
        scratch_shapes=scratch,
        input_output_aliases=aliases,
        compiler_params=pltpu.CompilerParams(
            dimension_semantics=("arbitrary", "arbitrary"), vmem_limit_bytes=VMEM_LIMIT),
        name="mixer_r%d_s%d_g%d" % (R, cfg.seg, NG),
    )(proj, *hist, sd_in, h_in, sg_in, sr_in, *tables, *weights, *(prev or ()))


def _rope_tables(pos):
    half = RET_DK // 2
    freqs = ROPE_BASE ** (-jnp.arange(half, dtype=f32) / half)
    ang = pos.astype(f32)[:, None] * freqs
    cos, sin = jnp.cos(ang), jnp.sin(ang)
    cos_h = jnp.concatenate([cos, cos], axis=1)
    sin_h = jnp.concatenate([-sin, sin], axis=1)
    return jnp.tile(cos_h, (1, N_HEADS)), jnp.tile(sin_h, (1, N_HEADS))


def _ret_tables(rows, seg, voff, vlen):
    log_gamma = jnp.log(1.0 - 2.0 ** (-5.0 - jnp.arange(N_HEADS, dtype=f32)))
    r = jnp.arange(rows)
    p = (r % seg - voff).astype(f32)
    ok = ((r % seg) >= voff) & ((r % seg) < voff + vlen)
    rel = p[:, None] - p[None, :]
    pair = ok[:, None] & ok[None, :] & ((r[:, None] // seg) == (r[None, :] // seg)) & (rel >= 0)
    intra = jnp.where(pair[None], jnp.exp(log_gamma[:, None, None] * jnp.maximum(rel, 0.0)[None]), 0.0)
    from_state = jnp.exp(log_gamma[:, None] * (p + 1.0))
    to_state = jnp.where(ok[None], jnp.exp(log_gamma[:, None] * (vlen - 1.0 - p)), 0.0)
    chunk_decay = jnp.exp(log_gamma * vlen)
    fs = jnp.repeat(from_state.T, RET_DK, axis=1)
    ts = jnp.repeat(to_state.T, RET_DK, axis=1)
    cd = jnp.broadcast_to(chunk_decay[:, None, None], (N_HEADS, 8, 128))
    return intra.astype(f32), fs.astype(f32), ts.astype(f32), cd.astype(f32)


SRC_AB = QKV_W
SRC_RUN1 = SRC_AB + 2 * N_HEADS
SRC_RC = SRC_RUN1 + (OFF_QD - OFF_XB)
SRC_RUN2 = SRC_RC + GLA_RANK
SRC_W = SRC_RUN2 + (OFF_SMALL - OFF_QD)
PACK_ROWS = 256
N_WIDE_BLOCKS = OFF_SMALL // PACK_ROWS


def _pack_kernel(src_ref, ab_ref, rc_ref, o_ref):
    i = pl.program_id(1)

    @pl.when(i < N_WIDE_BLOCKS)
    def _():
        o_ref[...] = src_ref[0].astype(bf16)

    @pl.when(i == N_WIDE_BLOCKS)
    def _():
        pad = jnp.zeros((N_PACK - OFF_SMALL - 2 * N_HEADS - GLA_RANK, D_MODEL), f32)
        o_ref[...] = jnp.concatenate([ab_ref[0], rc_ref[0], pad], axis=0).astype(bf16)


def _pack_src_row(i):
    shift = jnp.where(i < OFF_XB // PACK_ROWS, 0,
                      jnp.where(i < OFF_QD // PACK_ROWS, SRC_RUN1 - OFF_XB, SRC_RUN2 - OFF_QD))
    return pl.multiple_of(jnp.minimum(i * PACK_ROWS + shift, SRC_W - PACK_ROWS), 8)


def _pack_w_in(w_in_t):
    depth, n, d = w_in_t.shape
    assert n == SRC_W and d == D_MODEL and OFF_XB % PACK_ROWS == 0 and OFF_QD % PACK_ROWS == 0
    assert N_PACK - OFF_SMALL == PACK_ROWS
    return pl.pallas_call(
        _pack_kernel,
        grid=(depth, N_PACK // PACK_ROWS),
        in_specs=[
            pl.BlockSpec((pl.Element(1), pl.Element(PACK_ROWS), pl.Element(d)),
                         lambda l, i: (l, _pack_src_row(i), 0)),
            pl.BlockSpec((pl.Element(1), pl.Element(2 * N_HEADS), pl.Element(d)), lambda l, i: (l, SRC_AB, 0)),
            pl.BlockSpec((pl.Element(1), pl.Element(GLA_RANK), pl.Element(d)), lambda l, i: (l, SRC_RC, 0)),
        ],
        out_specs=pl.BlockSpec((None, PACK_ROWS, d), lambda l, i: (l, i, 0)),
        out_shape=jax.ShapeDtypeStruct((depth, N_PACK, d), bf16),
        compiler_params=pltpu.CompilerParams(dimension_semantics=("arbitrary", "arbitrary")),
        name="pack_w_in",
    )(w_in_t, w_in_t, w_in_t)


def _lanes(vec, off, width=128):
    return jnp.pad(vec[None, :], ((0, 0), (off, width - off - vec.shape[0])))


def kernel(x_prompt, x_sample, state_delta, state_delta_conv, state_lru, state_lru_conv, state_gla,
           state_ret, meta_tokens, norm_w, w_in, conv_a, a_log, dt_bias, norm_a, conv_b, conv_b_bias,
           lru_wa, lru_ba, lru_wx, lru_bx, lru_lambda, gla_w2, gla_b2, norm_c, w_out, final_norm):
    depth = w_in.shape[0]
    bp, lp = x_prompt.shape[0], x_prompt.shape[1]
    bs, ls = x_sample.shape[0], x_sample.shape[1]
    assert lp % CHUNK == 0 and ls == CONV_K and TILE_OFF + ls <= TILE
    nc_main = lp // CHUNK
    n_tile_rows = bs * TILE
    assert n_tile_rows % DEC_ROWS == 0 and DEC_ROWS % N_META == 0
    n_dec_blocks = n_tile_rows // DEC_ROWS
    seq_per_block = DEC_ROWS // TILE

    w_in_p = _pack_w_in(jnp.swapaxes(w_in, 1, 2))
    w_out_b = w_out.astype(bf16)

    h_main = x_prompt.reshape(bp * lp, D_MODEL)
    tiles = jnp.pad(x_sample, ((0, 0), (TILE_OFF, TILE - TILE_OFF - ls), (0, 0)))
    h_small = jnp.concatenate(
        [tiles.reshape(n_tile_rows, D_MODEL), meta_tokens.astype(x_prompt.dtype),
         jnp.zeros((DEC_ROWS - N_META, D_MODEL), x_prompt.dtype)], axis=0)
    n_small = h_small.shape[0]
    tm_small = n_small // 2
    assert tm_small % 8 == 0

    cfg_main = _Cfg(CHUNK, CHUNK, 0, CHUNK, nc_main, MAIN_NG, bp // MAIN_NG, False, 0)
    cfg_meta = _Cfg(N_META, N_META, 0, N_META, 1, 1, 1, False, n_tile_rows // N_META)
    assert n_dec_blocks % DEC_NG == 0
    cfg_dec = _Cfg(DEC_ROWS, TILE, TILE_OFF, ls, 1, DEC_NG, n_dec_blocks // DEC_NG, True, 0)

    pos_main = N_META + jnp.arange(lp)
    pos_meta = jnp.arange(N_META)
    pos_dec = jnp.tile(PAST_LEN + jnp.arange(TILE) - TILE_OFF, seq_per_block)
    tab_main = _rope_tables(pos_main) + _ret_tables(CHUNK, CHUNK, 0, CHUNK)
    tab_meta = _rope_tables(pos_meta) + _ret_tables(N_META, N_META, 0, N_META)
    tab_dec = _rope_tables(pos_dec) + _ret_tables(DEC_ROWS, TILE, TILE_OFF, ls)

    zeros_meta = (
        jnp.zeros((1, 1, N_HEADS, DN_DK, HEAD_V), f32), jnp.zeros((1, 1, 1, GROUP_W), f32),
        jnp.zeros((1, 1, N_HEADS, GLA_DK, HEAD_V), f32), jnp.zeros((1, 1, N_HEADS, RET_DK, HEAD_V), f32))
    zero_hist = (jnp.zeros((8, QKV_W), f32), jnp.zeros((8, GROUP_W), f32))

    weights = (
        conv_a,
        jnp.pad(a_log[:, None, :], ((0, 0), (0, 0), (SM_ALPHA, 128 - SM_ALPHA - N_HEADS))),
        jnp.pad(dt_bias[:, None, :], ((0, 0), (0, 0), (SM_ALPHA, 128 - SM_ALPHA - N_HEADS))),
        jnp.tile(norm_a, (1, N_HEADS))[:, None, :], conv_b, conv_b_bias[:, None, :],
        lru_wa, lru_wx, lru_ba[:, None, :], lru_bx[:, None, :], lru_lambda[:, None, :],
        jnp.pad(gla_w2, ((0, 0), (SM_RC, 128 - SM_RC - GLA_RANK), (0, 0))), gla_b2[:, None, :],
        jnp.tile(norm_c, (1, N_HEADS))[:, None, :],
    )
    nw = norm_w[:, None, :]
    fn = final_norm[None]

    pad_tile = ((0, 0), (0, 0), (0, TILE - (CONV_K - 1)), (0, 0))
    hist_dec = (jnp.pad(state_delta_conv, pad_tile).reshape(depth, n_dec_blocks, DEC_ROWS, QKV_W),
                jnp.pad(state_lru_conv, pad_tile).reshape(depth, n_dec_blocks, DEC_ROWS, GROUP_W))
    st_dec = (state_delta, state_lru.reshape(depth, cfg_dec.gblocks, cfg_dec.nstate, GROUP_W), state_gla, state_ret)

    p_st, s_st = None, None
    for l in range(depth):
        last = l == depth - 1
        proj_main = _inproj(h_main, nw, w_in_p, l, TM_MAIN_IN, TN_IN)
        proj_small = _inproj(h_small, nw, w_in_p, l, n_small, TN_IN)

        mx_meta, sd_m, ca_m, h_m, cb_m, sg_m, sr_m = _mixer(
            cfg_meta, l, proj_small.reshape(n_small // N_META, N_META, N_PACK), zero_hist, zeros_meta, 0,
            tab_meta, weights, 1, 0, None)
        hist_main = (jnp.pad(ca_m[0, 0], ((8 - (CONV_K - 1), 0), (0, 0))),
                     jnp.pad(cb_m[0, 0], ((8 - (CONV_K - 1), 0), (0, 0))))
        mx_main, *p_st = _mixer(
            cfg_main, l, proj_main.reshape(bp, lp, N_PACK), hist_main, (sd_m, h_m, sg_m, sr_m), 0,
            tab_main, weights, depth, l, p_st)

        mx_dec, *s_st = _mixer(
            cfg_dec, l, proj_small.reshape(n_small // DEC_ROWS, DEC_ROWS, N_PACK), hist_dec, st_dec, l,
            tab_dec, weights, depth, l, s_st)

        h_main = _outproj(mx_main.reshape(bp * lp, D_MODEL), w_out_b, l, h_main, fn, last, TM_MAIN_OUT)
        mx_small = jnp.concatenate(
            [mx_dec.reshape(n_tile_rows, D_MODEL), mx_meta[0],
             jnp.zeros((DEC_ROWS - N_META, D_MODEL), bf16)], axis=0)
        h_small = _outproj(mx_small, w_out_b, l, h_small, fn, last, tm_small)

    y_prompt = h_main.reshape(bp, lp, D_MODEL)
    y_sample = h_small[:n_tile_rows].reshape(bs, TILE, D_MODEL)[:, TILE_OFF:TILE_OFF + ls]
    sd_p, ca_p, h_p, cb_p, sg_p, sr_p = p_st
    sd_s, ca_s, h_s, cb_s, sg_s, sr_s = s_st
    ca_s = ca_s.reshape(depth, bs, TILE, QKV_W)[:, :, :CONV_K - 1]
    cb_s = cb_s.reshape(depth, bs, TILE, GROUP_W)[:, :, :CONV_K - 1]
    return (y_prompt, y_sample,
            sd_p, ca_p, h_p.reshape(depth, bp, GROUP_W), cb_p, sg_p, sr_p,
            sd_s, ca_s, h_s.reshape(depth, bs, GROUP_W), cb_s, sg_s, sr_s)
```

```python
import functools
import itertools

import jax
import jax.numpy as jnp
from jax import lax
from jax.experimental import pallas as pl
from jax.experimental.pallas import tpu as pltpu

f32 = jnp.float32
bf16 = jnp.bfloat16

D_MODEL = 2048
N_META = 16
CONV_K = 4
CHUNK = 64
N_HEADS = 4
HEAD_V = 128
GROUP_W = N_HEADS * HEAD_V
DN_DK = 128
GLA_DK = 64
RET_DK = 64
GLA_RANK = 16
GLA_TAU = 16.0
LRU_C = 8.0
ROPE_BASE = 10000.0
EPS = 1e-6
PAST_LEN = 16384
QKV_W = 3 * N_HEADS * DN_DK

OFF_QKV = 0
OFF_XB = 1536
OFF_QC = 2048
OFF_KC = 2304
OFF_VC = 2560
OFF_QD = 3072
OFF_KD = 3328
OFF_VD = 3584
OFF_GATE = 4096
OFF_SMALL = 6144
N_PACK = 6400
SM_ALPHA = 0
SM_BETA = 4
SM_RC = 8

TILE = 8
TILE_OFF = CONV_K - 1
DEC_ROWS = 64
DEC_NG = 1
MAIN_NG = 4

VMEM_LIMIT = 52 * 1024 * 1024
TM_MAIN_IN = 1024
TN_IN = 1280
TM_MAIN_OUT = 512
SCAN_STEPS_PER_SLOT = 16
GLA_PLAIN_RANGE = 60.0
GLA_ALWAYS_LEVELS_SEG = 16

_DONE = object()
NN = (((1,), (0,)), ((), ()))
NT = (((1,), (1,)), ((), ()))
TN = (((0,), (0,)), ((), ()))


def _split(a):
    hi = a.astype(bf16)
    lo = (a - hi.astype(f32)).astype(bf16)
    return hi, lo


P_SOLVE = 3


def _mm(a, b, dims=NN, passes=1):
    if passes == 6:
        return lax.dot_general(a, b, dims, precision=lax.Precision.HIGHEST, preferred_element_type=f32)
    if passes == 1:
        return lax.dot_general(a.astype(bf16), b.astype(bf16), dims, preferred_element_type=f32)
    ah, al = _split(a)
    bh, bl = _split(b)
    d = lambda x, y: lax.dot_general(x, y, dims, preferred_element_type=f32)
    return d(ah, bh) + (d(ah, bl) + d(al, bh))


def _mm_sel(sel, x, dims=NN, sel_first=True):
    sel = sel.astype(bf16)
    x1 = x.astype(bf16)
    r1 = x - x1.astype(f32)
    x2 = r1.astype(bf16)
    x3 = (r1 - x2.astype(f32)).astype(bf16)
    if sel_first:
        d = lambda y: lax.dot_general(sel, y, dims, preferred_element_type=f32)
    else:
        d = lambda y: lax.dot_general(y, sel, dims, preferred_element_type=f32)
    return d(x1) + (d(x2) + d(x3))


def _softplus(x):
    return jnp.maximum(x, 0.0) + jnp.log1p(jnp.exp(-jnp.abs(x)))


def _sigmoid(x):
    return 0.5 * jnp.tanh(0.5 * x) + 0.5


def _silu(x):
    return x * _sigmoid(x)


def _rms(x):
    return x * lax.rsqrt(jnp.mean(x * x, axis=-1, keepdims=True) + EPS)


def _rows(parts):
    return parts[0] if len(parts) == 1 else jnp.concatenate(parts, axis=0)


def _inproj_kernel(x_ref, nw_ref, w_ref, o_ref, xn_ref):
    @pl.when(pl.program_id(1) == 0)
    def _():
        xn_ref[...] = (_rms(x_ref[...]) * nw_ref[...]).astype(bf16)

    o_ref[...] = lax.dot_general(xn_ref[...], w_ref[...], NT, preferred_element_type=f32)


def _inproj(x, nw, w, layer, tm, tn):
    m = x.shape[0]
    return pl.pallas_call(
        _inproj_kernel,
        grid=(pl.cdiv(m, tm), N_PACK // tn),
        in_specs=[
            pl.BlockSpec((tm, D_MODEL), lambda i, j: (i, 0)),
            pl.BlockSpec((None, 1, D_MODEL), lambda i, j: (layer, 0, 0)),
            pl.BlockSpec((None, tn, D_MODEL), lambda i, j: (layer, j, 0)),
        ],
        out_specs=pl.BlockSpec((tm, tn), lambda i, j: (i, j)),
        out_shape=jax.ShapeDtypeStruct((m, N_PACK), f32),
        scratch_shapes=[pltpu.VMEM((tm, D_MODEL), bf16)],
        compiler_params=pltpu.CompilerParams(
            dimension_semantics=("arbitrary", "arbitrary"), vmem_limit_bytes=VMEM_LIMIT),
        name="inproj",
    )(x, nw, w)


def _outproj_kernel(final, m_ref, w_ref, x_ref, fn_ref, o_ref):
    y = x_ref[...] + jnp.dot(m_ref[...], w_ref[...], preferred_element_type=f32)
    if final:
        y = _rms(y) * fn_ref[...]
    o_ref[...] = y


def _outproj(mixed, w, layer, x, fn, final, tm):
    m = x.shape[0]
    return pl.pallas_call(
        functools.partial(_outproj_kernel, final),
        grid=(pl.cdiv(m, tm),),
        in_specs=[
            pl.BlockSpec((tm, D_MODEL), lambda i: (i, 0)),
            pl.BlockSpec((None, D_MODEL, D_MODEL), lambda i: (layer, 0, 0)),
            pl.BlockSpec((tm, D_MODEL), lambda i: (i, 0)),
            pl.BlockSpec((1, D_MODEL), lambda i: (0, 0)),
        ],
        out_specs=pl.BlockSpec((tm, D_MODEL), lambda i: (i, 0)),
        out_shape=jax.ShapeDtypeStruct((m, D_MODEL), f32),
        compiler_params=pltpu.CompilerParams(
            dimension_semantics=("arbitrary",), vmem_limit_bytes=VMEM_LIMIT),
        name="outproj",
    )(mixed, w, x, fn)


class _Cfg:
    def __init__(self, rows, seg, voff, vlen, nc, ng, gblocks, embedded, blk_off):
        self.rows, self.seg, self.voff, self.vlen = rows, seg, voff, vlen
        self.nc, self.ng, self.gblocks, self.embedded, self.blk_off = nc, ng, gblocks, embedded, blk_off
        self.nseg = rows // seg
        self.nstate = ng * self.nseg
        self.all_valid = (voff == 0 and vlen == seg)


def _tri_inverse(a_list, seg, row, col, eye):
    def blk(s):
        return (row // s) == (col // s)

    b8 = blk(8)
    n = [-jnp.where(b8, a, 0.0) for a in a_list]
    mm = _mm
    n2 = [mm(x, x) for x in n]
    yield
    n4 = [mm(x, x) for x in n2]
    t = [mm(eye + x, eye + y) for x, y in zip(n, n2)]
    yield
    t = [mm(x, eye + y) for x, y in zip(t, n4)]
    s = 8
    while s < seg:
        yield
        mask = blk(2 * s) & jnp.logical_not(blk(s))
        off = [jnp.where(mask, a, 0.0) for a in a_list]
        tb = [mm(x, o) for x, o in zip(t, off)]
        yield
        t = [x - mm(y, x) for x, y in zip(t, tb)]
        s *= 2
    yield
    resid = [eye - x - _mm(a, x, passes=P_SOLVE) for a, x in zip(a_list, t)]
    yield
    return [x + mm(x, r) for x, r in zip(t, resid)]


def _mixer_kernel(cfg, n_alias, *refs):
    R, SEG, NSEG, NG, NST = cfg.rows, cfg.seg, cfg.nseg, cfg.ng, cfg.nstate
    it = iter(refs)
    proj = next(it)
    if cfg.embedded:
        buf_a, buf_b = next(it), next(it)
    else:
        ic_a, ic_b = next(it), next(it)
    sd_in, h_in, sg_in, sr_in = next(it), next(it), next(it), next(it)
    cos_t, sin_t, intra_t, fs_t, ts_t, cd_t = (next(it) for _ in range(6))
    (conv_a, alogv, dtbv, norm_a, conv_b, cbb, wa, wx, ba, bx, lam, w2p, b2, norm_c) = (
        next(it) for _ in range(14))
    for _ in range(n_alias):
        next(it)
    mixed, sd, conv_a_o, h_o, conv_b_o, sg, sr = (next(it) for _ in range(7))
    xp_a, xp_b, l_a, l_b, l_o, gla_attn = (next(it) for _ in range(6))
    gla = {}

    c = pl.program_id(1)
    groups = range(NG)
    heads = range(N_HEADS)
    units = [(g, hd) for g in groups for hd in heads]
    nu = len(units)
    seg_rows = [slice(s * SEG, (s + 1) * SEG) for s in range(NSEG)]

    @pl.when(c == 0)
    def _init():
        shared = sd_in.shape[0] != NST
        for i in range(NST):
            j = 0 if shared else i
            sd[i] = sd_in[j]
            sg[i] = sg_in[j]
            sr[i] = sr_in[j]
        h_o[...] = jnp.broadcast_to(h_in[...], (NST, GROUP_W))
        for g in groups:
            if cfg.embedded:
                xp_a[g, 0:8, :] = jnp.zeros((8, QKV_W), f32)
                xp_b[g, 0:8, :] = jnp.zeros((8, GROUP_W), f32)
            else:
                xp_a[g, 0:8, :] = ic_a[...]
                xp_b[g, 0:8, :] = ic_b[...]
            xp_a[g, 8 + R:16 + R, :] = jnp.zeros((8, QKV_W), f32)
            xp_b[g, 8 + R:16 + R, :] = jnp.zeros((8, GROUP_W), f32)

    row = lax.broadcasted_iota(jnp.int32, (R, R), 0)
    col = lax.broadcasted_iota(jnp.int32, (R, R), 1)
    same = (row // SEG) == (col // SEG)
    incl = same & (col <= row)
    strict = same & (col < row)
    eye = (row == col).astype(f32)
    l_incl = incl.astype(f32)
    m_same = same.astype(f32)
    rmod = lax.broadcasted_iota(jnp.int32, (R, 1), 0) % SEG
    valid = (rmod >= cfg.voff) & (rmod < cfg.voff + cfg.vlen)
    is_hist = rmod < TILE_OFF

    def conv(xp, g, x, buf, w_ref):
        if cfg.embedded:
            x = jnp.where(is_hist, buf[g], x)
        xp[g, 8:8 + R, :] = x
        full = xp[g, 0:8 + R, :]
        y = w_ref[CONV_K - 1:CONV_K, :] * x
        for s in range(1, CONV_K):
            y = y + w_ref[CONV_K - 1 - s:CONV_K - s, :] * pltpu.roll(full, s, 0)[8:8 + R]
        return y

    def conv_with_history(xp, x_of, buf, w_ref, hist_out):
        ys = [conv(xp, g, x_of(g), buf, w_ref) for g in groups]
        for g in groups:
            if cfg.embedded:
                hist_out[g] = xp[g, 8 + TILE_OFF + 1:8 + TILE_OFF + 1 + R, :]
            else:
                xp[g, 0:8, :] = xp[g, R:R + 8, :]
        return ys

    small = [proj[g, :, OFF_SMALL:OFF_SMALL + 128] for g in groups]

    def put(g, k, o):
        gate = proj[g, :, OFF_GATE + k * GROUP_W:OFF_GATE + (k + 1) * GROUP_W]
        mixed[g, :, k * GROUP_W:(k + 1) * GROUP_W] = (o * _silu(gate)).astype(mixed.dtype)


    def lru_stream():
        yb = conv_with_history(xp_b, lambda g: proj[g, :, OFF_XB:OFF_XB + GROUP_W],
                               buf_b if cfg.embedded else None, conv_b, conv_b_o)
        yb = [y + cbb[...] for y in yb]
        sp_lam = _softplus(-lam[...])
        r_pre = [[_mm(yb[g][:, n * 128:(n + 1) * 128], wa[n]) for n in heads] for g in groups]
        i_pre = [[_mm(yb[g][:, n * 128:(n + 1) * 128], wx[n]) for n in heads] for g in groups]
        h_all = h_o[...]
        h_cur = {}
        for g, n in units:
            ls = slice(n * 128, (n + 1) * 128)
            x_n = yb[g][:, ls]
            log_a = -LRU_C * _sigmoid(r_pre[g][n] + ba[:, ls]) * sp_lam[:, ls]
            a_t = jnp.exp(log_a)
            b_t = jnp.sqrt(-jnp.tanh(log_a) * (a_t * a_t + 1.0)) * (_sigmoid(i_pre[g][n] + bx[:, ls]) * x_n)
            l_a[g * N_HEADS + n] = a_t
            l_b[g * N_HEADS + n] = b_t
            if not cfg.all_valid:
                l_o[g * N_HEADS + n] = jnp.zeros((R, 128), f32)
            h_cur[g, n] = h_all[g * NSEG:(g + 1) * NSEG, ls]
        yield
        for t in range(cfg.vlen):
            idx = pl.ds(cfg.voff + t, 1) if NSEG == 1 else pl.ds(cfg.voff + t, NSEG, stride=SEG)
            for g, n in units:
                k = g * N_HEADS + n
                h_cur[g, n] = l_a[k, idx, :] * h_cur[g, n] + l_b[k, idx, :]
                l_o[k, idx, :] = h_cur[g, n]
            if t % SCAN_STEPS_PER_SLOT == SCAN_STEPS_PER_SLOT - 1:
                yield
        h_o[...] = _rows([jnp.concatenate([h_cur[g, n] for n in heads], axis=1) for g in groups])
        for g in groups:
            put(g, 1, jnp.concatenate([l_o[g * N_HEADS + n] for n in heads], axis=1))

    def delta_stream():
        ya = conv_with_history(xp_a, lambda g: proj[g, :, OFF_QKV:OFF_QKV + QKV_W],
                               buf_a if cfg.embedded else None, conv_a, conv_a_o)
        qkv = [_silu(y) for y in ya]
        g_all, beta_all = [], []
        for g in groups:
            ga = -jnp.exp(alogv[...]) * _softplus(small[g] + dtbv[...])
            be = _sigmoid(small[g])
            if not cfg.all_valid:
                ga = jnp.where(valid, ga, 0.0)
                be = jnp.where(valid, be, 0.0)
            g_all.append(ga)
            beta_all.append(be)
        gcum = [_mm_sel(l_incl, x) for x in g_all]
        gtot = [_mm_sel(m_same, x) for x in g_all]
        gcum_t = [x.T for x in gcum]

        q_l, k_l, v_l, be_l, gc_l, gt_l, dec_l, eg_l = ([] for _ in range(8))
        for g, hd in units:
            q = qkv[g][:, hd * 128:(hd + 1) * 128]
            k = qkv[g][:, 512 + hd * 128:512 + (hd + 1) * 128]
            q_l.append(q * lax.rsqrt(jnp.sum(q * q, axis=-1, keepdims=True) + EPS) * (DN_DK ** -0.5))
            k_l.append(k * lax.rsqrt(jnp.sum(k * k, axis=-1, keepdims=True) + EPS))
            v_l.append(qkv[g][:, 1024 + hd * 128:1024 + (hd + 1) * 128])
            be_l.append(beta_all[g][:, SM_BETA + hd:SM_BETA + hd + 1])
            gc = gcum[g][:, SM_ALPHA + hd:SM_ALPHA + hd + 1]
            gr = gcum_t[g][SM_ALPHA + hd:SM_ALPHA + hd + 1, :]
            gc_l.append(gc)
            gt_l.append(gtot[g][:, SM_ALPHA + hd:SM_ALPHA + hd + 1])
            dec_l.append(jnp.where(incl, jnp.exp(jnp.where(incl, gc - gr, 0.0)), 0.0))
            eg_l.append(jnp.exp(gc))
        kb_l = [k_l[u] * be_l[u] for u in range(nu)]
        a_l = [jnp.where(strict, _mm(kb_l[u], k_l[u], NT) * dec_l[u], 0.0) for u in range(nu)]
        t_l = yield from _tri_inverse(a_l, SEG, row, col, eye)
        attn_l = [jnp.where(incl, _mm(q_l[u], k_l[u], NT) * dec_l[u], 0.0) for u in range(nu)]
        uw_l = [_mm(t_l[u], jnp.concatenate([v_l[u] * be_l[u], kb_l[u] * eg_l[u]], axis=1), passes=P_SOLVE)
                for u in range(nu)]
        st_l = []
        for u, (g, hd) in enumerate(units):
            qe = q_l[u] * eg_l[u]
            w = uw_l[u][:, 128:]
            st_l.append([_mm(jnp.concatenate([qe[rs], w[rs]], axis=0), sd[g * NSEG + s, hd])
                         for s, rs in enumerate(seg_rows)])
        vn_l = [uw_l[u][:, :128] - _rows([b[SEG:] for b in st_l[u]]) for u in range(nu)]
        o_l = [_rows([b[:SEG] for b in st_l[u]]) + _mm(attn_l[u], vn_l[u]) for u in range(nu)]
        for u, (g, hd) in enumerate(units):
            kd = k_l[u] * jnp.exp(gt_l[u] - gc_l[u])
            for s, rs in enumerate(seg_rows):
                g_last = jnp.exp(gt_l[u][s * SEG:s * SEG + 1, :])
                i = g * NSEG + s
                sd[i, hd] = sd[i, hd] * g_last + _mm(kd[rs], vn_l[u][rs], TN)
        for g in groups:
            put(g, 0, jnp.concatenate(
                [_rms(o_l[g * N_HEADS + hd]) * norm_a[:, hd * 128:(hd + 1) * 128] for hd in heads], axis=1))

    def gla_stream():
        lg = []
        for g in groups:
            x_gate = _mm(small[g], w2p[...]) + b2[...]
            z = -_softplus(-x_gate) * (1.0 / GLA_TAU)
            lg.append(z if cfg.all_valid else jnp.where(valid, z, 0.0))
        bcum = [_mm_sel(l_incl, x) for x in lg]
        btot = [_mm_sel(m_same, x) for x in lg]
        ones = jnp.ones((SEG, 128), f32)
        dec_s = [[jnp.exp(_mm_sel(ones, lg[g][rs], TN, sel_first=False)) for rs in seg_rows]
                 for g in groups]
        q_cs, k_cs, qe_c, kd_c = [], [], [], []
        for g in groups:
            q_c = proj[g, :, OFF_QC:OFF_QC + 256] * (GLA_DK ** -0.5)
            k_c = proj[g, :, OFF_KC:OFF_KC + 256]
            if not cfg.all_valid:
                k_c = jnp.where(valid, k_c, 0.0)
            q_cs.append(q_c)
            k_cs.append(k_c)
            qe_c.append(q_c * jnp.exp(bcum[g]))
            kd_c.append(k_c * jnp.exp(btot[g] - bcum[g]))
        yield
        ks_l = [slice(hd * GLA_DK, (hd + 1) * GLA_DK) for hd in heads]
        vc_l = [proj[g, :, OFF_VC + hd * 128:OFF_VC + (hd + 1) * 128] for g, hd in units]
        o_state = []
        for u, (g, hd) in enumerate(units):
            parts = []
            for s, rs in enumerate(seg_rows):
                i = g * NSEG + s
                parts.append(_mm(qe_c[g][rs, ks_l[hd]], sg[i, hd]))
                sg[i, hd] = sg[i, hd] * dec_s[g][s][ks_l[hd], :] + _mm(kd_c[g][rs, ks_l[hd]], vc_l[u][rs], TN)
            o_state.append(_rows(parts))
        gla.update(lg=lg, bcum=bcum, btot=btot, q=q_cs, k=k_cs, qe=qe_c, ks=ks_l, v=vc_l, o_state=o_state)
        if not branch_on_range:
            yield
            gla_scores_levels()
            gla_finish()

    def gla_scores_plain():
        for u, (g, hd) in enumerate(units):
            ks = gla["ks"][hd]
            ke = gla["k"][g][:, ks] * jnp.exp(-gla["bcum"][g][:, ks])
            gla_attn[u] = jnp.where(incl, _mm(gla["qe"][g][:, ks], ke, NT), 0.0)

    def gla_scores_levels():
        halves = [SEG >> (i + 1) for i in range(SEG.bit_length() - 1)]
        nlev = len(halves)
        between = []
        for h in halves:
            ref = (row // (2 * h)) * (2 * h) + h - 1
            between.append(((col > ref) & (col <= row)) | ((col > row) & (col <= ref)))
        between = jnp.concatenate(between, axis=0).astype(f32)
        pair = [((row // (2 * h)) == (col // (2 * h))) & ((row % (2 * h)) >= h) & ((col % (2 * h)) < h)
                for h in halves]
        rep = lambda x: jnp.concatenate([x] * nlev, axis=0)
        e_lv = [jnp.exp(_mm_sel(between, gla["lg"][g])) for g in groups]
        q_lv = [rep(gla["q"][g]) * e_lv[g] for g in groups]
        k_lv = [rep(gla["k"][g]) * e_lv[g] for g in groups]
        ks_l = gla["ks"]
        attn_l = [jnp.where(eye > 0.0, _mm(gla["q"][g][:, ks_l[hd]], gla["k"][g][:, ks_l[hd]], NT), 0.0)
                  for g, hd in units]
        for li in range(nlev):
            lv = slice(li * R, (li + 1) * R)
            attn_l = [jnp.where(pair[li], _mm(q_lv[g][lv, ks_l[hd]], k_lv[g][lv, ks_l[hd]], NT), attn_l[u])
                      for u, (g, hd) in enumerate(units)]
        for u in range(nu):
            gla_attn[u] = attn_l[u]

    branch_on_range = SEG > GLA_ALWAYS_LEVELS_SEG

    def gla_finish():
        if branch_on_range:
            worst = jnp.max(-gla["btot"][0])
            for g in list(groups)[1:]:
                worst = jnp.maximum(worst, jnp.max(-gla["btot"][g]))
            plain_ok = worst < GLA_PLAIN_RANGE
            pl.when(plain_ok)(gla_scores_plain)
            pl.when(jnp.logical_not(plain_ok))(gla_scores_levels)
        o_l = [_mm(gla_attn[u], gla["v"][u]) + gla["o_state"][u] for u in range(nu)]
        for g in groups:
            put(g, 2, jnp.concatenate(
                [_rms(o_l[g * N_HEADS + hd]) * norm_c[:, hd * 128:(hd + 1) * 128] for hd in heads], axis=1))

    def ret_stream():
        half = RET_DK // 2
        lane = lax.broadcasted_iota(jnp.int32, (R, 256), 1)
        first_half = (lane % RET_DK) < half

        def rotary(x):
            rot = jnp.where(first_half, pltpu.roll(x, 256 - half, 1), pltpu.roll(x, half, 1))
            return x * cos_t[...] + rot * sin_t[...]

        q_d = [rotary(proj[g, :, OFF_QD:OFF_QD + 256]) for g in groups]
        k_d = [rotary(proj[g, :, OFF_KD:OFF_KD + 256]) * (RET_DK ** -0.5) for g in groups]
        qf = [x * fs_t[...] for x in q_d]
        kt = [x * ts_t[...] for x in k_d]
        yield
        ks_l = [slice(hd * RET_DK, (hd + 1) * RET_DK) for hd in heads]
        vd_l = [proj[g, :, OFF_VD + hd * 128:OFF_VD + (hd + 1) * 128] for g, hd in units]
        attn_l = [_mm(q_d[g][:, ks_l[hd]], k_d[g][:, ks_l[hd]], NT) * intra_t[hd] for g, hd in units]
        o_l = [_mm(attn_l[u], vd_l[u]) for u in range(nu)]
        for u, (g, hd) in enumerate(units):
            parts = []
            for s, rs in enumerate(seg_rows):
                i = g * NSEG + s
                parts.append(_mm(qf[g][rs, ks_l[hd]], sr[i, hd]))
                sr[i, hd] = sr[i, hd] * cd_t[hd, 0:1, :] + _mm(kt[g][rs, ks_l[hd]], vd_l[u][rs], TN)
            o_l[u] = o_l[u] + _rows(parts)
        for g in groups:
            put(g, 3, jnp.concatenate([_rms(o_l[g * N_HEADS + hd]) for hd in heads], axis=1))

    others = itertools.chain(lru_stream(), gla_stream(), ret_stream())
    streams = [delta_stream(), others]
    while streams:
        for stream in list(streams):
            if next(stream, _DONE) is _DONE:
                streams.remove(stream)
    if branch_on_range:
        gla_finish()

    if not cfg.embedded:
        @pl.when(c == cfg.nc - 1)
        def _():
            for g in groups:
                conv_a_o[g] = xp_a[g, 5:8, :]
                conv_b_o[g] = xp_b[g, 5:8, :]


def _mixer(cfg, layer, proj, hist, states, lin, tables, weights, n_out_layers, lout, prev):
    R, NG, GB, NC, NST = cfg.rows, cfg.ng, cfg.gblocks, cfg.nc, cfg.nstate
    sd_in, h_in, sg_in, sr_in = states
    shared = sd_in.shape[1] != GB * NST
    nin = 1 if shared else NST

    def st(i):
        return 0 if shared else i

    if cfg.embedded:
        hist_specs = [pl.BlockSpec((None, NG, R, QKV_W), lambda gb, c: (layer, gb, c, 0)),
                      pl.BlockSpec((None, NG, R, GROUP_W), lambda gb, c: (layer, gb, c, 0))]
    else:
        hist_specs = [pl.BlockSpec((8, QKV_W), lambda gb, c: (0, 0)),
                      pl.BlockSpec((8, GROUP_W), lambda gb, c: (0, 0))]
    in_specs = [pl.BlockSpec((NG, R, N_PACK), lambda gb, c: (cfg.blk_off + gb, c, 0))] + hist_specs + [
        pl.BlockSpec((None, nin, N_HEADS, DN_DK, HEAD_V), lambda gb, c: (lin, st(gb), 0, 0, 0)),
        pl.BlockSpec((None, None, nin, GROUP_W), lambda gb, c: (lin, st(gb), 0, 0)),
        pl.BlockSpec((None, nin, N_HEADS, GLA_DK, HEAD_V), lambda gb, c: (lin, st(gb), 0, 0, 0)),
        pl.BlockSpec((None, nin, N_HEADS, RET_DK, HEAD_V), lambda gb, c: (lin, st(gb), 0, 0, 0)),
        pl.BlockSpec((R, 256), lambda gb, c: (c, 0)),
        pl.BlockSpec((R, 256), lambda gb, c: (c, 0)),
        pl.BlockSpec((N_HEADS, R, R), lambda gb, c: (0, 0, 0)),
        pl.BlockSpec((R, 256), lambda gb, c: (0, 0)),
        pl.BlockSpec((R, 256), lambda gb, c: (0, 0)),
        pl.BlockSpec((N_HEADS, 8, 128), lambda gb, c: (0, 0, 0)),
    ]
    for w in weights:
        in_specs.append(pl.BlockSpec((None,) + w.shape[1:], lambda gb, c, nd=w.ndim: (layer,) + (0,) * (nd - 1)))
    aliases = {}
    if prev is not None:
        for k, p in enumerate(prev):
            aliases[len(in_specs)] = 1 + k
            in_specs.append(pl.BlockSpec(memory_space=pl.ANY))

    n_seq_rows = NC * R
    conv_rows = n_seq_rows if cfg.embedded else CONV_K - 1
    nl = n_out_layers
    out_shape = [
        jax.ShapeDtypeStruct((GB * NG, n_seq_rows, D_MODEL), bf16),
        jax.ShapeDtypeStruct((nl, GB * NST, N_HEADS, DN_DK, HEAD_V), f32),
        jax.ShapeDtypeStruct((nl, GB * NG, conv_rows, QKV_W), f32),
        jax.ShapeDtypeStruct((nl, GB, NST, GROUP_W), f32),
        jax.ShapeDtypeStruct((nl, GB * NG, conv_rows, GROUP_W), f32),
        jax.ShapeDtypeStruct((nl, GB * NST, N_HEADS, GLA_DK, HEAD_V), f32),
        jax.ShapeDtypeStruct((nl, GB * NST, N_HEADS, RET_DK, HEAD_V), f32),
    ]
    if cfg.embedded:
        conv_specs = [pl.BlockSpec((None, NG, R, QKV_W), lambda gb, c: (lout, gb, c, 0)),
                      pl.BlockSpec((None, NG, R, GROUP_W), lambda gb, c: (lout, gb, c, 0))]
    else:
        conv_specs = [pl.BlockSpec((None, NG, CONV_K - 1, QKV_W), lambda gb, c: (lout, gb, 0, 0)),
                      pl.BlockSpec((None, NG, CONV_K - 1, GROUP_W), lambda gb, c: (lout, gb, 0, 0))]
    out_specs = [
        pl.BlockSpec((NG, R, D_MODEL), lambda gb, c: (gb, c, 0)),
        pl.BlockSpec((None, NST, N_HEADS, DN_DK, HEAD_V), lambda gb, c: (lout, gb, 0, 0, 0)),
        conv_specs[0],
        pl.BlockSpec((None, None, NST, GROUP_W), lambda gb, c: (lout, gb, 0, 0)),
        conv_specs[1],
        pl.BlockSpec((None, NST, N_HEADS, GLA_DK, HEAD_V), lambda gb, c: (lout, gb, 0, 0, 0)),
        pl.BlockSpec((None, NST, N_HEADS, RET_DK, HEAD_V), lambda gb, c: (lout, gb, 0, 0, 0)),
    ]
    scratch = [
        pltpu.VMEM((NG, R + 16, QKV_W), f32),
        pltpu.VMEM((NG, R + 16, GROUP_W), f32),
        pltpu.VMEM((NG * N_HEADS, R, 128), f32),
        pltpu.VMEM((NG * N_HEADS, R, 128), f32),
        pltpu.VMEM((NG * N_HEADS, R, 128), f32),
        pltpu.VMEM((NG * N_HEADS, R, R), f32),
    ]
    return pl.pallas_call(
        functools.partial(_mixer_kernel, cfg, len(aliases)),
        grid=(GB, NC),
        in_specs=in_specs,
        out_specs=out_specs,
        out_shape=out_shape,
        scratch_shapes=scratch,
        input_output_aliases=aliases,
        compiler_params=pltpu.CompilerParams(
            dimension_semantics=("arbitrary", "arbitrary"), vmem_limit_bytes=VMEM_LIMIT),
        name="mixer_r%d_s%d_g%d" % (R, cfg.seg, NG),
    )(proj, *hist, sd_in, h_in, sg_in, sr_in, *tables, *weights, *(prev or ()))


def _rope_tables(pos):
    half = RET_DK // 2
    freqs = ROPE_BASE ** (-jnp.arange(half, dtype=f32) / half)
    ang = pos.astype(f32)[:, None] * freqs
    cos, sin = jnp.cos(ang), jnp.sin(ang)
    cos_h = jnp.concatenate([cos, cos], axis=1)
    sin_h = jnp.concatenate([-sin, sin], axis=1)
    return jnp.tile(cos_h, (1, N_HEADS)), jnp.tile(sin_h, (1, N_HEADS))


def _ret_tables(rows, seg, voff, vlen):
    log_gamma = jnp.log(1.0 - 2.0 ** (-5.0 - jnp.arange(N_HEADS, dtype=f32)))
    r = jnp.arange(rows)
    p = (r % seg - voff).astype(f32)
    ok = ((r % seg) >= voff) & ((r % seg) < voff + vlen)
    rel = p[:, None] - p[None, :]
    pair = ok[:, None] & ok[None, :] & ((r[:, None] // seg) == (r[None, :] // seg)) & (rel >= 0)
    intra = jnp.where(pair[None], jnp.exp(log_gamma[:, None, None] * jnp.maximum(rel, 0.0)[None]), 0.0)
    from_state = jnp.exp(log_gamma[:, None] * (p + 1.0))
    to_state = jnp.where(ok[None], jnp.exp(log_gamma[:, None] * (vlen - 1.0 - p)), 0.0)
    chunk_decay = jnp.exp(log_gamma * vlen)
    fs = jnp.repeat(from_state.T, RET_DK, axis=1)
    ts = jnp.repeat(to_state.T, RET_DK, axis=1)
    cd = jnp.broadcast_to(chunk_decay[:, None, None], (N_HEADS, 8, 128))
    return intra.astype(f32), fs.astype(f32), ts.astype(f32), cd.astype(f32)


SRC_AB = QKV_W
SRC_RUN1 = SRC_AB + 2 * N_HEADS
SRC_RC = SRC_RUN1 + (OFF_QD - OFF_XB)
SRC_RUN2 = SRC_RC + GLA_RANK
SRC_W = SRC_RUN2 + (OFF_SMALL - OFF_QD)
PACK_ROWS = 256
N_WIDE_BLOCKS = OFF_SMALL // PACK_ROWS


def _pack_kernel(src_ref, ab_ref, rc_ref, o_ref):
    i = pl.program_id(1)

    @pl.when(i < N_WIDE_BLOCKS)
    def _():
        o_ref[...] = src_ref[0].astype(bf16)

    @pl.when(i == N_WIDE_BLOCKS)
    def _():
        pad = jnp.zeros((N_PACK - OFF_SMALL - 2 * N_HEADS - GLA_RANK, D_MODEL), f32)
        o_ref[...] = jnp.concatenate([ab_ref[0], rc_ref[0], pad], axis=0).astype(bf16)


def _pack_src_row(i):
    shift = jnp.where(i < OFF_XB // PACK_ROWS, 0,
                      jnp.where(i < OFF_QD // PACK_ROWS, SRC_RUN1 - OFF_XB, SRC_RUN2 - OFF_QD))
    return pl.multiple_of(jnp.minimum(i * PACK_ROWS + shift, SRC_W - PACK_ROWS), 8)


def _pack_w_in(w_in_t):
    depth, n, d = w_in_t.shape
    assert n == SRC_W and d == D_MODEL and OFF_XB % PACK_ROWS == 0 and OFF_QD % PACK_ROWS == 0
    assert N_PACK - OFF_SMALL == PACK_ROWS
    return pl.pallas_call(
        _pack_kernel,
        grid=(depth, N_PACK // PACK_ROWS),
        in_specs=[
            pl.BlockSpec((pl.Element(1), pl.Element(PACK_ROWS), pl.Element(d)),
                         lambda l, i: (l, _pack_src_row(i), 0)),
            pl.BlockSpec((pl.Element(1), pl.Element(2 * N_HEADS), pl.Element(d)), lambda l, i: (l, SRC_AB, 0)),
            pl.BlockSpec((pl.Element(1), pl.Element(GLA_RANK), pl.Element(d)), lambda l, i: (l, SRC_RC, 0)),
        ],
        out_specs=pl.BlockSpec((None, PACK_ROWS, d), lambda l, i: (l, i, 0)),
        out_shape=jax.ShapeDtypeStruct((depth, N_PACK, d), bf16),
        compiler_params=pltpu.CompilerParams(dimension_semantics=("arbitrary", "arbitrary")),
        name="pack_w_in",
    )(w_in_t, w_in_t, w_in_t)


def _lanes(vec, off, width=128):
    return jnp.pad(vec[None, :], ((0, 0), (off, width - off - vec.shape[0])))


def kernel(x_prompt, x_sample, state_delta, state_delta_conv, state_lru, state_lru_conv, state_gla,
           state_ret, meta_tokens, norm_w, w_in, conv_a, a_log, dt_bias, norm_a, conv_b, conv_b_bias,
           lru_wa, lru_ba, lru_wx, lru_bx, lru_lambda, gla_w2, gla_b2, norm_c, w_out, final_norm):
    depth = w_in.shape[0]
    bp, lp = x_prompt.shape[0], x_prompt.shape[1]
    bs, ls = x_sample.shape[0], x_sample.shape[1]
    assert lp % CHUNK == 0 and ls == CONV_K and TILE_OFF + ls <= TILE
    nc_main = lp // CHUNK
    n_tile_rows = bs * TILE
    assert n_tile_rows % DEC_ROWS == 0 and DEC_ROWS % N_META == 0
    n_dec_blocks = n_tile_rows // DEC_ROWS
    seq_per_block = DEC_ROWS // TILE

    w_in_p = _pack_w_in(jnp.swapaxes(w_in, 1, 2))
    w_out_b = w_out.astype(bf16)

    h_main = x_prompt.reshape(bp * lp, D_MODEL)
    tiles = jnp.pad(x_sample, ((0, 0), (TILE_OFF, TILE - TILE_OFF - ls), (0, 0)))
    h_small = jnp.concatenate(
        [tiles.reshape(n_tile_rows, D_MODEL), meta_tokens.astype(x_prompt.dtype),
         jnp.zeros((DEC_ROWS - N_META, D_MODEL), x_prompt.dtype)], axis=0)
    n_small = h_small.shape[0]
    tm_small = n_small // 2
    assert tm_small % 8 == 0

    cfg_main = _Cfg(CHUNK, CHUNK, 0, CHUNK, nc_main, MAIN_NG, bp // MAIN_NG, False, 0)
    cfg_meta = _Cfg(N_META, N_META, 0, N_META, 1, 1, 1, False, n_tile_rows // N_META)
    assert n_dec_blocks % DEC_NG == 0
    cfg_dec = _Cfg(DEC_ROWS, TILE, TILE_OFF, ls, 1, DEC_NG, n_dec_blocks // DEC_NG, True, 0)

    pos_main = N_META + jnp.arange(lp)
    pos_meta = jnp.arange(N_META)
    pos_dec = jnp.tile(PAST_LEN + jnp.arange(TILE) - TILE_OFF, seq_per_block)
    tab_main = _rope_tables(pos_main) + _ret_tables(CHUNK, CHUNK, 0, CHUNK)
    tab_meta = _rope_tables(pos_meta) + _ret_tables(N_META, N_META, 0, N_META)
    tab_dec = _rope_tables(pos_dec) + _ret_tables(DEC_ROWS, TILE, TILE_OFF, ls)

    zeros_meta = (
        jnp.zeros((1, 1, N_HEADS, DN_DK, HEAD_V), f32), jnp.zeros((1, 1, 1, GROUP_W), f32),
        jnp.zeros((1, 1, N_HEADS, GLA_DK, HEAD_V), f32), jnp.zeros((1, 1, N_HEADS, RET_DK, HEAD_V), f32))
    zero_hist = (jnp.zeros((8, QKV_W), f32), jnp.zeros((8, GROUP_W), f32))

    weights = (
        conv_a,
        jnp.pad(a_log[:, None, :], ((0, 0), (0, 0), (SM_ALPHA, 128 - SM_ALPHA - N_HEADS))),
        jnp.pad(dt_bias[:, None, :], ((0, 0), (0, 0), (SM_ALPHA, 128 - SM_ALPHA - N_HEADS))),
        jnp.tile(norm_a, (1, N_HEADS))[:, None, :], conv_b, conv_b_bias[:, None, :],
        lru_wa, lru_wx, lru_ba[:, None, :], lru_bx[:, None, :], lru_lambda[:, None, :],
        jnp.pad(gla_w2, ((0, 0), (SM_RC, 128 - SM_RC - GLA_RANK), (0, 0))), gla_b2[:, None, :],
        jnp.tile(norm_c, (1, N_HEADS))[:, None, :],
    )
    nw = norm_w[:, None, :]
    fn = final_norm[None]

    pad_tile = ((0, 0), (0, 0), (0, TILE - (CONV_K - 1)), (0, 0))
    hist_dec = (jnp.pad(state_delta_conv, pad_tile).reshape(depth, n_dec_blocks, DEC_ROWS, QKV_W),
                jnp.pad(state_lru_conv, pad_tile).reshape(depth, n_dec_blocks, DEC_ROWS, GROUP_W))
    st_dec = (state_delta, state_lru.reshape(depth, cfg_dec.gblocks, cfg_dec.nstate, GROUP_W), state_gla, state_ret)

    p_st, s_st = None, None
    for l in range(depth):
        last = l == depth - 1
        proj_main = _inproj(h_main, nw, w_in_p, l, TM_MAIN_IN, TN_IN)
        proj_small = _inproj(h_small, nw, w_in_p, l, n_small, TN_IN)

        mx_meta, sd_m, ca_m, h_m, cb_m, sg_m, sr_m = _mixer(
            cfg_meta, l, proj_small.reshape(n_small // N_META, N_META, N_PACK), zero_hist, zeros_meta, 0,
            tab_meta, weights, 1, 0, None)
        hist_main = (jnp.pad(ca_m[0, 0], ((8 - (CONV_K - 1), 0), (0, 0))),
                     jnp.pad(cb_m[0, 0], ((8 - (CONV_K - 1), 0), (0, 0))))
        mx_main, *p_st = _mixer(
            cfg_main, l, proj_main.reshape(bp, lp, N_PACK), hist_main, (sd_m, h_m, sg_m, sr_m), 0,
            tab_main, weights, depth, l, p_st)

        mx_dec, *s_st = _mixer(
            cfg_dec, l, proj_small.reshape(n_small // DEC_ROWS, DEC_ROWS, N_PACK), hist_dec, st_dec, l,
            tab_dec, weights, depth, l, s_st)

        h_main = _outproj(mx_main.reshape(bp * lp, D_MODEL), w_out_b, l, h_main, fn, last, TM_MAIN_OUT)
        mx_small = jnp.concatenate(
            [mx_dec.reshape(n_tile_rows, D_MODEL), mx_meta[0],
             jnp.zeros((DEC_ROWS - N_META, D_MODEL), bf16)], axis=0)
        h_small = _outproj(mx_small, w_out_b, l, h_small, fn, last, tm_small)

    y_prompt = h_main.reshape(bp, lp, D_MODEL)
    y_sample = h_small[:n_tile_rows].reshape(bs, TILE, D_MODEL)[:, TILE_OFF:TILE_OFF + ls]
    sd_p, ca_p, h_p, cb_p, sg_p, sr_p = p_st
    sd_s, ca_s, h_s, cb_s, sg_s, sr_s = s_st
    ca_s = ca_s.reshape(depth, bs, TILE, QKV_W)[:, :, :CONV_K - 1]
    cb_s = cb_s.reshape(depth, bs, TILE, GROUP_W)[:, :, :CONV_K - 1]
    return (y_prompt, y_sample,
            sd_p, ca_p, h_p.reshape(depth, bp, GROUP_W), cb_p, sg_p, sr_p,
            sd_s, ca_s, h_s.reshape(depth, bs, GROUP_W), cb_s, sg_s, sr_s)
```

```python
import functools
import itertools

import jax
import jax.numpy as jnp
from jax import lax
from jax.experimental import pallas as pl
from jax.experimental.pallas import tpu as pltpu

f32 = jnp.float32
bf16 = jnp.bfloat16

D_MODEL = 2048
N_META = 16
CONV_K = 4
CHUNK = 64
N_HEADS = 4
HEAD_V = 128
GROUP_W = N_HEADS * HEAD_V
DN_DK = 128
GLA_DK = 64
RET_DK = 64
GLA_RANK = 16
GLA_TAU = 16.0
LRU_C = 8.0
ROPE_BASE = 10000.0
EPS = 1e-6
PAST_LEN = 16384
QKV_W = 3 * N_HEADS * DN_DK

OFF_QKV = 0
OFF_XB = 1536
OFF_QC = 2048
OFF_KC = 2304
OFF_VC = 2560
OFF_QD = 3072
OFF_KD = 3328
OFF_VD = 3584
OFF_GATE = 4096
OFF_SMALL = 6144
N_PACK = 6400
SM_ALPHA = 0
SM_BETA = 4
SM_RC = 8

TILE = 8
TILE_OFF = CONV_K - 1
DEC_ROWS = 64
DEC_NG = 1
MAIN_NG = 4

VMEM_LIMIT = 52 * 1024 * 1024
TM_MAIN_IN = 1024
TN_IN = 1280
TM_MAIN_OUT = 512
SCAN_STEPS_PER_SLOT = 16
GLA_PLAIN_RANGE = 60.0
GLA_ALWAYS_LEVELS_SEG = 16

_DONE = object()
NN = (((1,), (0,)), ((), ()))
NT = (((1,), (1,)), ((), ()))
TN = (((0,), (0,)), ((), ()))


def _split(a):
    hi = a.astype(bf16)
    lo = (a - hi.astype(f32)).astype(bf16)
    return hi, lo


P_SOLVE = 3


def _mm(a, b, dims=NN, passes=1):
    if passes == 6:
        return lax.dot_general(a, b, dims, precision=lax.Precision.HIGHEST, preferred_element_type=f32)
    if passes == 1:
        return lax.dot_general(a.astype(bf16), b.astype(bf16), dims, preferred_element_type=f32)
    ah, al = _split(a)
    bh, bl = _split(b)
    d = lambda x, y: lax.dot_general(x, y, dims, preferred_element_type=f32)
    return d(ah, bh) + (d(ah, bl) + d(al, bh))


def _mm_sel(sel, x, dims=NN, sel_first=True):
    sel = sel.astype(bf16)
    x1 = x.astype(bf16)
    r1 = x - x1.astype(f32)
    x2 = r1.astype(bf16)
    x3 = (r1 - x2.astype(f32)).astype(bf16)
    if sel_first:
        d = lambda y: lax.dot_general(sel, y, dims, preferred_element_type=f32)
    else:
        d = lambda y: lax.dot_general(y, sel, dims, preferred_element_type=f32)
    return d(x1) + (d(x2) + d(x3))


def _softplus(x):
    return jnp.maximum(x, 0.0) + jnp.log1p(jnp.exp(-jnp.abs(x)))


def _sigmoid(x):
    return 0.5 * jnp.tanh(0.5 * x) + 0.5


def _silu(x):
    return x * _sigmoid(x)


def _rms(x):
    return x * lax.rsqrt(jnp.mean(x * x, axis=-1, keepdims=True) + EPS)


def _rows(parts):
    return parts[0] if len(parts) == 1 else jnp.concatenate(parts, axis=0)


def _inproj_kernel(x_ref, nw_ref, w_ref, o_ref, xn_ref):
    @pl.when(pl.program_id(1) == 0)
    def _():
        xn_ref[...] = (_rms(x_ref[...]) * nw_ref[...]).astype(bf16)

    o_ref[...] = lax.dot_general(xn_ref[...], w_ref[...], NT, preferred_element_type=f32)


def _inproj(x, nw, w, layer, tm, tn):
    m = x.shape[0]
    return pl.pallas_call(
        _inproj_kernel,
        grid=(pl.cdiv(m, tm), N_PACK // tn),
        in_specs=[
            pl.BlockSpec((tm, D_MODEL), lambda i, j: (i, 0)),
            pl.BlockSpec((None, 1, D_MODEL), lambda i, j: (layer, 0, 0)),
            pl.BlockSpec((None, tn, D_MODEL), lambda i, j: (layer, j, 0)),
        ],
        out_specs=pl.BlockSpec((tm, tn), lambda i, j: (i, j)),
        out_shape=jax.ShapeDtypeStruct((m, N_PACK), f32),
        scratch_shapes=[pltpu.VMEM((tm, D_MODEL), bf16)],
        compiler_params=pltpu.CompilerParams(
            dimension_semantics=("arbitrary", "arbitrary"), vmem_limit_bytes=VMEM_LIMIT),
        name="inproj",
    )(x, nw, w)


def _outproj_kernel(final, m_ref, w_ref, x_ref, fn_ref, o_ref):
    y = x_ref[...] + jnp.dot(m_ref[...], w_ref[...], preferred_element_type=f32)
    if final:
        y = _rms(y) * fn_ref[...]
    o_ref[...] = y


def _outproj(mixed, w, layer, x, fn, final, tm):
    m = x.shape[0]
    return pl.pallas_call(
        functools.partial(_outproj_kernel, final),
        grid=(pl.cdiv(m, tm),),
        in_specs=[
            pl.BlockSpec((tm, D_MODEL), lambda i: (i, 0)),
            pl.BlockSpec((None, D_MODEL, D_MODEL), lambda i: (layer, 0, 0)),
            pl.BlockSpec((tm, D_MODEL), lambda i: (i, 0)),
            pl.BlockSpec((1, D_MODEL), lambda i: (0, 0)),
        ],
        out_specs=pl.BlockSpec((tm, D_MODEL), lambda i: (i, 0)),
        out_shape=jax.ShapeDtypeStruct((m, D_MODEL), f32),
        compiler_params=pltpu.CompilerParams(
            dimension_semantics=("arbitrary",), vmem_limit_bytes=VMEM_LIMIT),
        name="outproj",
    )(mixed, w, x, fn)


class _Cfg:
    def __init__(self, rows, seg, voff, vlen, nc, ng, gblocks, embedded, blk_off):
        self.rows, self.seg, self.voff, self.vlen = rows, seg, voff, vlen
        self.nc, self.ng, self.gblocks, self.embedded, self.blk_off = nc, ng, gblocks, embedded, blk_off
        self.nseg = rows // seg
        self.nstate = ng * self.nseg
        self.all_valid = (voff == 0 and vlen == seg)


def _tri_inverse(a_list, seg, row, col, eye):
    def blk(s):
        return (row // s) == (col // s)

    b8 = blk(8)
    n = [-jnp.where(b8, a, 0.0) for a in a_list]
    mm = _mm
    n2 = [mm(x, x) for x in n]
    yield
    n4 = [mm(x, x) for x in n2]
    t = [mm(eye + x, eye + y) for x, y in zip(n, n2)]
    yield
    t = [mm(x, eye + y) for x, y in zip(t, n4)]
    s = 8
    while s < seg:
        yield
        mask = blk(2 * s) & jnp.logical_not(blk(s))
        off = [jnp.where(mask, a, 0.0) for a in a_list]
        tb = [mm(x, o) for x, o in zip(t, off)]
        yield
        t = [x - mm(y, x) for x, y in zip(t, tb)]
        s *= 2
    yield
    resid = [eye - x - _mm(a, x, passes=P_SOLVE) for a, x in zip(a_list, t)]
    yield
    return [x + mm(x, r) for x, r in zip(t, resid)]


def _mixer_kernel(cfg, n_alias, *refs):
    R, SEG, NSEG, NG, NST = cfg.rows, cfg.seg, cfg.nseg, cfg.ng, cfg.nstate
    it = iter(refs)
    proj = next(it)
    if cfg.embedded:
        buf_a, buf_b = next(it), next(it)
    else:
        ic_a, ic_b = next(it), next(it)
    sd_in, h_in, sg_in, sr_in = next(it), next(it), next(it), next(it)
    cos_t, sin_t, intra_t, fs_t, ts_t, cd_t = (next(it) for _ in range(6))
    (conv_a, alogv, dtbv, norm_a, conv_b, cbb, wa, wx, ba, bx, lam, w2p, b2, norm_c) = (
        next(it) for _ in range(14))
    for _ in range(n_alias):
        next(it)
    mixed, sd, conv_a_o, h_o, conv_b_o, sg, sr = (next(it) for _ in range(7))
    xp_a, xp_b, l_a, l_b, l_o, gla_attn = (next(it) for _ in range(6))
    gla = {}

    c = pl.program_id(1)
    groups = range(NG)
    heads = range(N_HEADS)
    units = [(g, hd) for g in groups for hd in heads]
    nu = len(units)
    seg_rows = [slice(s * SEG, (s + 1) * SEG) for s in range(NSEG)]

    @pl.when(c == 0)
    def _init():
        shared = sd_in.shape[0] != NST
        for i in range(NST):
            j = 0 if shared else i
            sd[i] = sd_in[j]
            sg[i] = sg_in[j]
            sr[i] = sr_in[j]
        h_o[...] = jnp.broadcast_to(h_in[...], (NST, GROUP_W))
        for g in groups:
            if cfg.embedded:
                xp_a[g, 0:8, :] = jnp.zeros((8, QKV_W), f32)
                xp_b[g, 0:8, :] = jnp.zeros((8, GROUP_W), f32)
            else:
                xp_a[g, 0:8, :] = ic_a[...]
                xp_b[g, 0:8, :] = ic_b[...]
            xp_a[g, 8 + R:16 + R, :] = jnp.zeros((8, QKV_W), f32)
            xp_b[g, 8 + R:16 + R, :] = jnp.zeros((8, GROUP_W), f32)

    row = lax.broadcasted_iota(jnp.int32, (R, R), 0)
    col = lax.broadcasted_iota(jnp.int32, (R, R), 1)
    same = (row // SEG) == (col // SEG)
    incl = same & (col <= row)
    strict = same & (col < row)
    eye = (row == col).astype(f32)
    l_incl = incl.astype(f32)
    m_same = same.astype(f32)
    rmod = lax.broadcasted_iota(jnp.int32, (R, 1), 0) % SEG
    valid = (rmod >= cfg.voff) & (rmod < cfg.voff + cfg.vlen)
    is_hist = rmod < TILE_OFF

    def conv(xp, g, x, buf, w_ref):
        if cfg.embedded:
            x = jnp.where(is_hist, buf[g], x)
        xp[g, 8:8 + R, :] = x
        full = xp[g, 0:8 + R, :]
        y = w_ref[CONV_K - 1:CONV_K, :] * x
        for s in range(1, CONV_K):
            y = y + w_ref[CONV_K - 1 - s:CONV_K - s, :] * pltpu.roll(full, s, 0)[8:8 + R]
        return y

    def conv_with_history(xp, x_of, buf, w_ref, hist_out):
        ys = [conv(xp, g, x_of(g), buf, w_ref) for g in groups]
        for g in groups:
            if cfg.embedded:
                hist_out[g] = xp[g, 8 + TILE_OFF + 1:8 + TILE_OFF + 1 + R, :]
            else:
                xp[g, 0:8, :] = xp[g, R:R + 8, :]
        return ys

    small = [proj[g, :, OFF_SMALL:OFF_SMALL + 128] for g in groups]

    def put(g, k, o):
        gate = proj[g, :, OFF_GATE + k * GROUP_W:OFF_GATE + (k + 1) * GROUP_W]
        mixed[g, :, k * GROUP_W:(k + 1) * GROUP_W] = (o * _silu(gate)).astype(mixed.dtype)


    def lru_stream():
        yb = conv_with_history(xp_b, lambda g: proj[g, :, OFF_XB:OFF_XB + GROUP_W],
                               buf_b if cfg.embedded else None, conv_b, conv_b_o)
        yb = [y + cbb[...] for y in yb]
        sp_lam = _softplus(-lam[...])
        r_pre = [[_mm(yb[g][:, n * 128:(n + 1) * 128], wa[n]) for n in heads] for g in groups]
        i_pre = [[_mm(yb[g][:, n * 128:(n + 1) * 128], wx[n]) for n in heads] for g in groups]
        h_all = h_o[...]
        h_cur = {}
        for g, n in units:
            ls = slice(n * 128, (n + 1) * 128)
            x_n = yb[g][:, ls]
            log_a = -LRU_C * _sigmoid(r_pre[g][n] + ba[:, ls]) * sp_lam[:, ls]
            a_t = jnp.exp(log_a)
            b_t = jnp.sqrt(-jnp.tanh(log_a) * (a_t * a_t + 1.0)) * (_sigmoid(i_pre[g][n] + bx[:, ls]) * x_n)
            l_a[g * N_HEADS + n] = a_t
            l_b[g * N_HEADS + n] = b_t
            if not cfg.all_valid:
                l_o[g * N_HEADS + n] = jnp.zeros((R, 128), f32)
            h_cur[g, n] = h_all[g * NSEG:(g + 1) * NSEG, ls]
        yield
        for t in range(cfg.vlen):
            idx = pl.ds(cfg.voff + t, 1) if NSEG == 1 else pl.ds(cfg.voff + t, NSEG, stride=SEG)
            for g, n in units:
                k = g * N_HEADS + n
                h_cur[g, n] = l_a[k, idx, :] * h_cur[g, n] + l_b[k, idx, :]
                l_o[k, idx, :] = h_cur[g, n]
            if t % SCAN_STEPS_PER_SLOT == SCAN_STEPS_PER_SLOT - 1:
                yield
        h_o[...] = _rows([jnp.concatenate([h_cur[g, n] for n in heads], axis=1) for g in groups])
        for g in groups:
            put(g, 1, jnp.concatenate([l_o[g * N_HEADS + n] for n in heads], axis=1))

    def delta_stream():
        ya = conv_with_history(xp_a, lambda g: proj[g, :, OFF_QKV:OFF_QKV + QKV_W],
                               buf_a if cfg.embedded else None, conv_a, conv_a_o)
        qkv = [_silu(y) for y in ya]
        g_all, beta_all = [], []
        for g in groups:
            ga = -jnp.exp(alogv[...]) * _softplus(small[g] + dtbv[...])
            be = _sigmoid(small[g])
            if not cfg.all_valid:
                ga = jnp.where(valid, ga, 0.0)
                be = jnp.where(valid, be, 0.0)
            g_all.append(ga)
            beta_all.append(be)
        gcum = [_mm_sel(l_incl, x) for x in g_all]
        gtot = [_mm_sel(m_same, x) for x in g_all]
        gcum_t = [x.T for x in gcum]

        q_l, k_l, v_l, be_l, gc_l, gt_l, dec_l, eg_l = ([] for _ in range(8))
        for g, hd in units:
            q = qkv[g][:, hd * 128:(hd + 1) * 128]
            k = qkv[g][:, 512 + hd * 128:512 + (hd + 1) * 128]
            q_l.append(q * lax.rsqrt(jnp.sum(q * q, axis=-1, keepdims=True) + EPS) * (DN_DK ** -0.5))
            k_l.append(k * lax.rsqrt(jnp.sum(k * k, axis=-1, keepdims=True) + EPS))
            v_l.append(qkv[g][:, 1024 + hd * 128:1024 + (hd + 1) * 128])
            be_l.append(beta_all[g][:, SM_BETA + hd:SM_BETA + hd + 1])
            gc = gcum[g][:, SM_ALPHA + hd:SM_ALPHA + hd + 1]
            gr = gcum_t[g][SM_ALPHA + hd:SM_ALPHA + hd + 1, :]
            gc_l.append(gc)
            gt_l.append(gtot[g][:, SM_ALPHA + hd:SM_ALPHA + hd + 1])
            dec_l.append(jnp.where(incl, jnp.exp(jnp.where(incl, gc - gr, 0.0)), 0.0))
            eg_l.append(jnp.exp(gc))
        kb_l = [k_l[u] * be_l[u] for u in range(nu)]
        a_l = [jnp.where(strict, _mm(kb_l[u], k_l[u], NT) * dec_l[u], 0.0) for u in range(nu)]
        t_l = yield from _tri_inverse(a_l, SEG, row, col, eye)
        attn_l = [jnp.where(incl, _mm(q_l[u], k_l[u], NT) * dec_l[u], 0.0) for u in range(nu)]
        uw_l = [_mm(t_l[u], jnp.concatenate([v_l[u] * be_l[u], kb_l[u] * eg_l[u]], axis=1), passes=P_SOLVE)
                for u in range(nu)]
        st_l = []
        for u, (g, hd) in enumerate(units):
            qe = q_l[u] * eg_l[u]
            w = uw_l[u][:, 128:]
            st_l.append([_mm(jnp.concatenate([qe[rs], w[rs]], axis=0), sd[g * NSEG + s, hd])
                         for s, rs in enumerate(seg_rows)])
        vn_l = [uw_l[u][:, :128] - _rows([b[SEG:] for b in st_l[u]]) for u in range(nu)]
        o_l = [_rows([b[:SEG] for b in st_l[u]]) + _mm(attn_l[u], vn_l[u]) for u in range(nu)]
        for u, (g, hd) in enumerate(units):
            kd = k_l[u] * jnp.exp(gt_l[u] - gc_l[u])
            for s, rs in enumerate(seg_rows):
                g_last = jnp.exp(gt_l[u][s * SEG:s * SEG + 1, :])
                i = g * NSEG + s
                sd[i, hd] = sd[i, hd] * g_last + _mm(kd[rs], vn_l[u][rs], TN)
        for g in groups:
            put(g, 0, jnp.concatenate(
                [_rms(o_l[g * N_HEADS + hd]) * norm_a[:, hd * 128:(hd + 1) * 128] for hd in heads], axis=1))

    def gla_stream():
        lg = []
        for g in groups:
            x_gate = _mm(small[g], w2p[...]) + b2[...]
            z = -_softplus(-x_gate) * (1.0 / GLA_TAU)
            lg.append(z if cfg.all_valid else jnp.where(valid, z, 0.0))
        bcum = [_mm_sel(l_incl, x) for x in lg]
        btot = [_mm_sel(m_same, x) for x in lg]
        ones = jnp.ones((SEG, 128), f32)
        dec_s = [[jnp.exp(_mm_sel(ones, lg[g][rs], TN, sel_first=False)) for rs in seg_rows]
                 for g in groups]
        q_cs, k_cs, qe_c, kd_c = [], [], [], []
        for g in groups:
            q_c = proj[g, :, OFF_QC:OFF_QC + 256] * (GLA_DK ** -0.5)
            k_c = proj[g, :, OFF_KC:OFF_KC + 256]
            if not cfg.all_valid:
                k_c = jnp.where(valid, k_c, 0.0)
            q_cs.append(q_c)
            k_cs.append(k_c)
            qe_c.append(q_c * jnp.exp(bcum[g]))
            kd_c.append(k_c * jnp.exp(btot[g] - bcum[g]))
        yield
        ks_l = [slice(hd * GLA_DK, (hd + 1) * GLA_DK) for hd in heads]
        vc_l = [proj[g, :, OFF_VC + hd * 128:OFF_VC + (hd + 1) * 128] for g, hd in units]
        o_state = []
        for u, (g, hd) in enumerate(units):
            parts = []
            for s, rs in enumerate(seg_rows):
                i = g * NSEG + s
                parts.append(_mm(qe_c[g][rs, ks_l[hd]], sg[i, hd]))
                sg[i, hd] = sg[i, hd] * dec_s[g][s][ks_l[hd], :] + _mm(kd_c[g][rs, ks_l[hd]], vc_l[u][rs], TN)
            o_state.append(_rows(parts))
        gla.update(lg=lg, bcum=bcum, btot=btot, q=q_cs, k=k_cs, qe=qe_c, ks=ks_l, v=vc_l, o_state=o_state)
        if not branch_on_range:
            yield
            yield from gla_scores_levels()
            yield
            gla_finish()

    def gla_scores_plain():
        for u, (g, hd) in enumerate(units):
            ks = gla["ks"][hd]
            ke = gla["k"][g][:, ks] * jnp.exp(-gla["bcum"][g][:, ks])
            gla_attn[u] = jnp.where(incl, _mm(gla["qe"][g][:, ks], ke, NT), 0.0)

    def gla_scores_levels():
        halves = [SEG >> (i + 1) for i in range(SEG.bit_length() - 1)]
        nlev = len(halves)
        between = []
        for h in halves:
            ref = (row // (2 * h)) * (2 * h) + h - 1
            between.append(((col > ref) & (col <= row)) | ((col > row) & (col <= ref)))
        between = jnp.concatenate(between, axis=0).astype(f32)
        pair = [((row // (2 * h)) == (col // (2 * h))) & ((row % (2 * h)) >= h) & ((col % (2 * h)) < h)
                for h in halves]
        rep = lambda x: jnp.concatenate([x] * nlev, axis=0)
        e_lv = [jnp.exp(_mm_sel(between, gla["lg"][g])) for g in groups]
        q_lv = [rep(gla["q"][g]) * e_lv[g] for g in groups]
        k_lv = [rep(gla["k"][g]) * e_lv[g] for g in groups]
        yield
        ks_l = gla["ks"]
        attn_l = [jnp.where(eye > 0.0, _mm(gla["q"][g][:, ks_l[hd]], gla["k"][g][:, ks_l[hd]], NT), 0.0)
                  for g, hd in units]
        for li in range(nlev):
            lv = slice(li * R, (li + 1) * R)
            attn_l = [jnp.where(pair[li], _mm(q_lv[g][lv, ks_l[hd]], k_lv[g][lv, ks_l[hd]], NT), attn_l[u])
                      for u, (g, hd) in enumerate(units)]
            if li % 2 == 1:
                yield
        for u in range(nu):
            gla_attn[u] = attn_l[u]

    branch_on_range = SEG > GLA_ALWAYS_LEVELS_SEG

    def gla_finish():
        if branch_on_range:
            worst = jnp.max(-gla["btot"][0])
            for g in list(groups)[1:]:
                worst = jnp.maximum(worst, jnp.max(-gla["btot"][g]))
            plain_ok = worst < GLA_PLAIN_RANGE
            pl.when(plain_ok)(gla_scores_plain)

            @pl.when(jnp.logical_not(plain_ok))
            def _():
                for _stage in gla_scores_levels():
                    pass
        o_l = [_mm(gla_attn[u], gla["v"][u]) + gla["o_state"][u] for u in range(nu)]
        for g in groups:
            put(g, 2, jnp.concatenate(
                [_rms(o_l[g * N_HEADS + hd]) * norm_c[:, hd * 128:(hd + 1) * 128] for hd in heads], axis=1))

    def ret_stream():
        half = RET_DK // 2
        lane = lax.broadcasted_iota(jnp.int32, (R, 256), 1)
        first_half = (lane % RET_DK) < half

        def rotary(x):
            rot = jnp.where(first_half, pltpu.roll(x, 256 - half, 1), pltpu.roll(x, half, 1))
            return x * cos_t[...] + rot * sin_t[...]

        q_d = [rotary(proj[g, :, OFF_QD:OFF_QD + 256]) for g in groups]
        k_d = [rotary(proj[g, :, OFF_KD:OFF_KD + 256]) * (RET_DK ** -0.5) for g in groups]
        qf = [x * fs_t[...] for x in q_d]
        kt = [x * ts_t[...] for x in k_d]
        yield
        ks_l = [slice(hd * RET_DK, (hd + 1) * RET_DK) for hd in heads]
        vd_l = [proj[g, :, OFF_VD + hd * 128:OFF_VD + (hd + 1) * 128] for g, hd in units]
        attn_l = [_mm(q_d[g][:, ks_l[hd]], k_d[g][:, ks_l[hd]], NT) * intra_t[hd] for g, hd in units]
        o_l = [_mm(attn_l[u], vd_l[u]) for u in range(nu)]
        for u, (g, hd) in enumerate(units):
            parts = []
            for s, rs in enumerate(seg_rows):
                i = g * NSEG + s
                parts.append(_mm(qf[g][rs, ks_l[hd]], sr[i, hd]))
                sr[i, hd] = sr[i, hd] * cd_t[hd, 0:1, :] + _mm(kt[g][rs, ks_l[hd]], vd_l[u][rs], TN)
            o_l[u] = o_l[u] + _rows(parts)
        for g in groups:
            put(g, 3, jnp.concatenate([_rms(o_l[g * N_HEADS + hd]) for hd in heads], axis=1))

    others = itertools.chain(lru_stream(), gla_stream(), ret_stream())
    streams = [delta_stream(), others]
    while streams:
        for stream in list(streams):
            if next(stream, _DONE) is _DONE:
                streams.remove(stream)
    if branch_on_range:
        gla_finish()

    if not cfg.embedded:
        @pl.when(c == cfg.nc - 1)
        def _():
            for g in groups:
                conv_a_o[g] = xp_a[g, 5:8, :]
                conv_b_o[g] = xp_b[g, 5:8, :]


def _mixer(cfg, layer, proj, hist, states, lin, tables, weights, n_out_layers, lout, prev):
    R, NG, GB, NC, NST = cfg.rows, cfg.ng, cfg.gblocks, cfg.nc, cfg.nstate
    sd_in, h_in, sg_in, sr_in = states
    shared = sd_in.shape[1] != GB * NST
    nin = 1 if shared else NST

    def st(i):
        return 0 if shared else i

    if cfg.embedded:
        hist_specs = [pl.BlockSpec((None, NG, R, QKV_W), lambda gb, c: (layer, gb, c, 0)),
                      pl.BlockSpec((None, NG, R, GROUP_W), lambda gb, c: (layer, gb, c, 0))]
    else:
        hist_specs = [pl.BlockSpec((8, QKV_W), lambda gb, c: (0, 0)),
                      pl.BlockSpec((8, GROUP_W), lambda gb, c: (0, 0))]
    in_specs = [pl.BlockSpec((NG, R, N_PACK), lambda gb, c: (cfg.blk_off + gb, c, 0))] + hist_specs + [
        pl.BlockSpec((None, nin, N_HEADS, DN_DK, HEAD_V), lambda gb, c: (lin, st(gb), 0, 0, 0)),
        pl.BlockSpec((None, None, nin, GROUP_W), lambda gb, c: (lin, st(gb), 0, 0)),
        pl.BlockSpec((None, nin, N_HEADS, GLA_DK, HEAD_V), lambda gb, c: (lin, st(gb), 0, 0, 0)),
        pl.BlockSpec((None, nin, N_HEADS, RET_DK, HEAD_V), lambda gb, c: (lin, st(gb), 0, 0, 0)),
        pl.BlockSpec((R, 256), lambda gb, c: (c, 0)),
        pl.BlockSpec((R, 256), lambda gb, c: (c, 0)),
        pl.BlockSpec((N_HEADS, R, R), lambda gb, c: (0, 0, 0)),
        pl.BlockSpec((R, 256), lambda gb, c: (0, 0)),
        pl.BlockSpec((R, 256), lambda gb, c: (0, 0)),
        pl.BlockSpec((N_HEADS, 8, 128), lambda gb, c: (0, 0, 0)),
    ]
    for w in weights:
        in_specs.append(pl.BlockSpec((None,) + w.shape[1:], lambda gb, c, nd=w.ndim: (layer,) + (0,) * (nd - 1)))
    aliases = {}
    if prev is not None:
        for k, p in enumerate(prev):
            aliases[len(in_specs)] = 1 + k
            in_specs.append(pl.BlockSpec(memory_space=pl.ANY))

    n_seq_rows = NC * R
    conv_rows = n_seq_rows if cfg.embedded else CONV_K - 1
    nl = n_out_layers
    out_shape = [
        jax.ShapeDtypeStruct((GB * NG, n_seq_rows, D_MODEL), bf16),
        jax.ShapeDtypeStruct((nl, GB * NST, N_HEADS, DN_DK, HEAD_V), f32),
        jax.ShapeDtypeStruct((nl, GB * NG, conv_rows, QKV_W), f32),
        jax.ShapeDtypeStruct((nl, GB, NST, GROUP_W), f32),
        jax.ShapeDtypeStruct((nl, GB * NG, conv_rows, GROUP_W), f32),
        jax.ShapeDtypeStruct((nl, GB * NST, N_HEADS, GLA_DK, HEAD_V), f32),
        jax.ShapeDtypeStruct((nl, GB * NST, N_HEADS, RET_DK, HEAD_V), f32),
    ]
    if cfg.embedded:
        conv_specs = [pl.BlockSpec((None, NG, R, QKV_W), lambda gb, c: (lout, gb, c, 0)),
                      pl.BlockSpec((None, NG, R, GROUP_W), lambda gb, c: (lout, gb, c, 0))]
    else:
        conv_specs = [pl.BlockSpec((None, NG, CONV_K - 1, QKV_W), lambda gb, c: (lout, gb, 0, 0)),
                      pl.BlockSpec((None, NG, CONV_K - 1, GROUP_W), lambda gb, c: (lout, gb, 0, 0))]
    out_specs = [
        pl.BlockSpec((NG, R, D_MODEL), lambda gb, c: (gb, c, 0)),
        pl.BlockSpec((None, NST, N_HEADS, DN_DK, HEAD_V), lambda gb, c: (lout, gb, 0, 0, 0)),
        conv_specs[0],
        pl.BlockSpec((None, None, NST, GROUP_W), lambda gb, c: (lout, gb, 0, 0)),
        conv_specs[1],
        pl.BlockSpec((None, NST, N_HEADS, GLA_DK, HEAD_V), lambda gb, c: (lout, gb, 0, 0, 0)),
        pl.BlockSpec((None, NST, N_HEADS, RET_DK, HEAD_V), lambda gb, c: (lout, gb, 0, 0, 0)),
    ]
    scratch = [
        pltpu.VMEM((NG, R + 16, QKV_W), f32),
        pltpu.VMEM((NG, R + 16, GROUP_W), f32),
        pltpu.VMEM((NG * N_HEADS, R, 128), f32),
        pltpu.VMEM((NG * N_HEADS, R, 128), f32),
        pltpu.VMEM((NG * N_HEADS, R, 128), f32),
        pltpu.VMEM((NG * N_HEADS, R, R), f32),
    ]
    return pl.pallas_call(
        functools.partial(_mixer_kernel, cfg, len(aliases)),
        grid=(GB, NC),
        in_specs=in_specs,
        out_specs=out_specs,
        out_shape=out_shape,
        scratch_shapes=scratch,
        input_output_aliases=aliases,
        compiler_params=pltpu.CompilerParams(
            dimension_semantics=("arbitrary", "arbitrary"), vmem_limit_bytes=VMEM_LIMIT),
        name="mixer_r%d_s%d_g%d" % (R, cfg.seg, NG),
    )(proj, *hist, sd_in, h_in, sg_in, sr_in, *tables, *weights, *(prev or ()))


def _rope_tables(pos):
    half = RET_DK // 2
    freqs = ROPE_BASE ** (-jnp.arange(half, dtype=f32) / half)
    ang = pos.astype(f32)[:, None] * freqs
    cos, sin = jnp.cos(ang), jnp.sin(ang)
    cos_h = jnp.concatenate([cos, cos], axis=1)
    sin_h = jnp.concatenate([-sin, sin], axis=1)
    return jnp.tile(cos_h, (1, N_HEADS)), jnp.tile(sin_h, (1, N_HEADS))


def _ret_tables(rows, seg, voff, vlen):
    log_gamma = jnp.log(1.0 - 2.0 ** (-5.0 - jnp.arange(N_HEADS, dtype=f32)))
    r = jnp.arange(rows)
    p = (r % seg - voff).astype(f32)
    ok = ((r % seg) >= voff) & ((r % seg) < voff + vlen)
    rel = p[:, None] - p[None, :]
    pair = ok[:, None] & ok[None, :] & ((r[:, None] // seg) == (r[None, :] // seg)) & (rel >= 0)
    intra = jnp.where(pair[None], jnp.exp(log_gamma[:, None, None] * jnp.maximum(rel, 0.0)[None]), 0.0)
    from_state = jnp.exp(log_gamma[:, None] * (p + 1.0))
    to_state = jnp.where(ok[None], jnp.exp(log_gamma[:, None] * (vlen - 1.0 - p)), 0.0)
    chunk_decay = jnp.exp(log_gamma * vlen)
    fs = jnp.repeat(from_state.T, RET_DK, axis=1)
    ts = jnp.repeat(to_state.T, RET_DK, axis=1)
    cd = jnp.broadcast_to(chunk_decay[:, None, None], (N_HEADS, 8, 128))
    return intra.astype(f32), fs.astype(f32), ts.astype(f32), cd.astype(f32)


SRC_AB = QKV_W
SRC_RUN1 = SRC_AB + 2 * N_HEADS
SRC_RC = SRC_RUN1 + (OFF_QD - OFF_XB)
SRC_RUN2 = SRC_RC + GLA_RANK
SRC_W = SRC_RUN2 + (OFF_SMALL - OFF_QD)
PACK_ROWS = 256
N_WIDE_BLOCKS = OFF_SMALL // PACK_ROWS


def _pack_kernel(src_ref, ab_ref, rc_ref, o_ref):
    i = pl.program_id(1)

    @pl.when(i < N_WIDE_BLOCKS)
    def _():
        o_ref[...] = src_ref[0].astype(bf16)

    @pl.when(i == N_WIDE_BLOCKS)
    def _():
        pad = jnp.zeros((N_PACK - OFF_SMALL - 2 * N_HEADS - GLA_RANK, D_MODEL), f32)
        o_ref[...] = jnp.concatenate([ab_ref[0], rc_ref[0], pad], axis=0).astype(bf16)


def _pack_src_row(i):
    shift = jnp.where(i < OFF_XB // PACK_ROWS, 0,
                      jnp.where(i < OFF_QD // PACK_ROWS, SRC_RUN1 - OFF_XB, SRC_RUN2 - OFF_QD))
    return pl.multiple_of(jnp.minimum(i * PACK_ROWS + shift, SRC_W - PACK_ROWS), 8)


def _pack_w_in(w_in_t):
    depth, n, d = w_in_t.shape
    assert n == SRC_W and d == D_MODEL and OFF_XB % PACK_ROWS == 0 and OFF_QD % PACK_ROWS == 0
    assert N_PACK - OFF_SMALL == PACK_ROWS
    return pl.pallas_call(
        _pack_kernel,
        grid=(depth, N_PACK // PACK_ROWS),
        in_specs=[
            pl.BlockSpec((pl.Element(1), pl.Element(PACK_ROWS), pl.Element(d)),
                         lambda l, i: (l, _pack_src_row(i), 0)),
            pl.BlockSpec((pl.Element(1), pl.Element(2 * N_HEADS), pl.Element(d)), lambda l, i: (l, SRC_AB, 0)),
            pl.BlockSpec((pl.Element(1), pl.Element(GLA_RANK), pl.Element(d)), lambda l, i: (l, SRC_RC, 0)),
        ],
        out_specs=pl.BlockSpec((None, PACK_ROWS, d), lambda l, i: (l, i, 0)),
        out_shape=jax.ShapeDtypeStruct((depth, N_PACK, d), bf16),
        compiler_params=pltpu.CompilerParams(dimension_semantics=("arbitrary", "arbitrary")),
        name="pack_w_in",
    )(w_in_t, w_in_t, w_in_t)


def _lanes(vec, off, width=128):
    return jnp.pad(vec[None, :], ((0, 0), (off, width - off - vec.shape[0])))


def kernel(x_prompt, x_sample, state_delta, state_delta_conv, state_lru, state_lru_conv, state_gla,
           state_ret, meta_tokens, norm_w, w_in, conv_a, a_log, dt_bias, norm_a, conv_b, conv_b_bias,
           lru_wa, lru_ba, lru_wx, lru_bx, lru_lambda, gla_w2, gla_b2, norm_c, w_out, final_norm):
    depth = w_in.shape[0]
    bp, lp = x_prompt.shape[0], x_prompt.shape[1]
    bs, ls = x_sample.shape[0], x_sample.shape[1]
    assert lp % CHUNK == 0 and ls == CONV_K and TILE_OFF + ls <= TILE
    nc_main = lp // CHUNK
    n_tile_rows = bs * TILE
    assert n_tile_rows % DEC_ROWS == 0 and DEC_ROWS % N_META == 0
    n_dec_blocks = n_tile_rows // DEC_ROWS
    seq_per_block = DEC_ROWS // TILE

    w_in_p = _pack_w_in(jnp.swapaxes(w_in, 1, 2))
    w_out_b = w_out.astype(bf16)

    h_main = x_prompt.reshape(bp * lp, D_MODEL)
    tiles = jnp.pad(x_sample, ((0, 0), (TILE_OFF, TILE - TILE_OFF - ls), (0, 0)))
    h_small = jnp.concatenate(
        [tiles.reshape(n_tile_rows, D_MODEL), meta_tokens.astype(x_prompt.dtype),
         jnp.zeros((DEC_ROWS - N_META, D_MODEL), x_prompt.dtype)], axis=0)
    n_small = h_small.shape[0]
    tm_small = n_small // 2
    assert tm_small % 8 == 0

    cfg_main = _Cfg(CHUNK, CHUNK, 0, CHUNK, nc_main, MAIN_NG, bp // MAIN_NG, False, 0)
    cfg_meta = _Cfg(N_META, N_META, 0, N_META, 1, 1, 1, False, n_tile_rows // N_META)
    assert n_dec_blocks % DEC_NG == 0
    cfg_dec = _Cfg(DEC_ROWS, TILE, TILE_OFF, ls, 1, DEC_NG, n_dec_blocks // DEC_NG, True, 0)

    pos_main = N_META + jnp.arange(lp)
    pos_meta = jnp.arange(N_META)
    pos_dec = jnp.tile(PAST_LEN + jnp.arange(TILE) - TILE_OFF, seq_per_block)
    tab_main = _rope_tables(pos_main) + _ret_tables(CHUNK, CHUNK, 0, CHUNK)
    tab_meta = _rope_tables(pos_meta) + _ret_tables(N_META, N_META, 0, N_META)
    tab_dec = _rope_tables(pos_dec) + _ret_tables(DEC_ROWS, TILE, TILE_OFF, ls)

    zeros_meta = (
        jnp.zeros((1, 1, N_HEADS, DN_DK, HEAD_V), f32), jnp.zeros((1, 1, 1, GROUP_W), f32),
        jnp.zeros((1, 1, N_HEADS, GLA_DK, HEAD_V), f32), jnp.zeros((1, 1, N_HEADS, RET_DK, HEAD_V), f32))
    zero_hist = (jnp.zeros((8, QKV_W), f32), jnp.zeros((8, GROUP_W), f32))

    weights = (
        conv_a,
        jnp.pad(a_log[:, None, :], ((0, 0), (0, 0), (SM_ALPHA, 128 - SM_ALPHA - N_HEADS))),
        jnp.pad(dt_bias[:, None, :], ((0, 0), (0, 0), (SM_ALPHA, 128 - SM_ALPHA - N_HEADS))),
        jnp.tile(norm_a, (1, N_HEADS))[:, None, :], conv_b, conv_b_bias[:, None, :],
        lru_wa, lru_wx, lru_ba[:, None, :], lru_bx[:, None, :], lru_lambda[:, None, :],
        jnp.pad(gla_w2, ((0, 0), (SM_RC, 128 - SM_RC - GLA_RANK), (0, 0))), gla_b2[:, None, :],
        jnp.tile(norm_c, (1, N_HEADS))[:, None, :],
    )
    nw = norm_w[:, None, :]
    fn = final_norm[None]

    pad_tile = ((0, 0), (0, 0), (0, TILE - (CONV_K - 1)), (0, 0))
    hist_dec = (jnp.pad(state_delta_conv, pad_tile).reshape(depth, n_dec_blocks, DEC_ROWS, QKV_W),
                jnp.pad(state_lru_conv, pad_tile).reshape(depth, n_dec_blocks, DEC_ROWS, GROUP_W))
    st_dec = (state_delta, state_lru.reshape(depth, cfg_dec.gblocks, cfg_dec.nstate, GROUP_W), state_gla, state_ret)

    p_st, s_st = None, None
    for l in range(depth):
        last = l == depth - 1
        proj_main = _inproj(h_main, nw, w_in_p, l, TM_MAIN_IN, TN_IN)
        proj_small = _inproj(h_small, nw, w_in_p, l, n_small, TN_IN)

        mx_meta, sd_m, ca_m, h_m, cb_m, sg_m, sr_m = _mixer(
            cfg_meta, l, proj_small.reshape(n_small // N_META, N_META, N_PACK), zero_hist, zeros_meta, 0,
            tab_meta, weights, 1, 0, None)
        hist_main = (jnp.pad(ca_m[0, 0], ((8 - (CONV_K - 1), 0), (0, 0))),
                     jnp.pad(cb_m[0, 0], ((8 - (CONV_K - 1), 0), (0, 0))))
        mx_main, *p_st = _mixer(
            cfg_main, l, proj_main.reshape(bp, lp, N_PACK), hist_main, (sd_m, h_m, sg_m, sr_m), 0,
            tab_main, weights, depth, l, p_st)

        mx_dec, *s_st = _mixer(
            cfg_dec, l, proj_small.reshape(n_small // DEC_ROWS, DEC_ROWS, N_PACK), hist_dec, st_dec, l,
            tab_dec, weights, depth, l, s_st)

        h_main = _outproj(mx_main.reshape(bp * lp, D_MODEL), w_out_b, l, h_main, fn, last, TM_MAIN_OUT)
        mx_small = jnp.concatenate(
            [mx_dec.reshape(n_tile_rows, D_MODEL), mx_meta[0],
             jnp.zeros((DEC_ROWS - N_META, D_MODEL), bf16)], axis=0)
        h_small = _outproj(mx_small, w_out_b, l, h_small, fn, last, tm_small)

    y_prompt = h_main.reshape(bp, lp, D_MODEL)
    y_sample = h_small[:n_tile_rows].reshape(bs, TILE, D_MODEL)[:, TILE_OFF:TILE_OFF + ls]
    sd_p, ca_p, h_p, cb_p, sg_p, sr_p = p_st
    sd_s, ca_s, h_s, cb_s, sg_s, sr_s = s_st
    ca_s = ca_s.reshape(depth, bs, TILE, QKV_W)[:, :, :CONV_K - 1]
    cb_s = cb_s.reshape(depth, bs, TILE, GROUP_W)[:, :, :CONV_K - 1]
    return (y_prompt, y_sample,
            sd_p, ca_p, h_p.reshape(depth, bp, GROUP_W), cb_p, sg_p, sr_p,
            sd_s, ca_s, h_s.reshape(depth, bs, GROUP_W), cb_s, sg_s, sr_s)
```

```python
import functools
import itertools

import jax
import jax.numpy as jnp
from jax import lax
from jax.experimental import pallas as pl
from jax.experimental.pallas import tpu as pltpu

f32 = jnp.float32
bf16 = jnp.bfloat16

D_MODEL = 2048
N_META = 16
CONV_K = 4
CHUNK = 64
N_HEADS = 4
HEAD_V = 128
GROUP_W = N_HEADS * HEAD_V
DN_DK = 128
GLA_DK = 64
RET_DK = 64
GLA_RANK = 16
GLA_TAU = 16.0
LRU_C = 8.0
ROPE_BASE = 10000.0
EPS = 1e-6
PAST_LEN = 16384
QKV_W = 3 * N_HEADS * DN_DK

OFF_QKV = 0
OFF_XB = 1536
OFF_QC = 2048
OFF_KC = 2304
OFF_VC = 2560
OFF_QD = 3072
OFF_KD = 3328
OFF_VD = 3584
OFF_GATE = 4096
OFF_SMALL = 6144
N_PACK = 6400
SM_ALPHA = 0
SM_BETA = 4
SM_RC = 8

TILE = 8
TILE_OFF = CONV_K - 1
DEC_ROWS = 64
DEC_NG = 2
MAIN_NG = 4

VMEM_LIMIT = 56 * 1024 * 1024
TM_MAIN_IN = 1024
TN_IN = 1280
TM_MAIN_OUT = 512
SCAN_STEPS_PER_SLOT = 16
GLA_PLAIN_RANGE = 60.0
GLA_ALWAYS_LEVELS_SEG = 16

_DONE = object()
NN = (((1,), (0,)), ((), ()))
NT = (((1,), (1,)), ((), ()))
TN = (((0,), (0,)), ((), ()))


def _split(a):
    hi = a.astype(bf16)
    lo = (a - hi.astype(f32)).astype(bf16)
    return hi, lo


P_SOLVE = 3


def _mm(a, b, dims=NN, passes=1):
    if passes == 6:
        return lax.dot_general(a, b, dims, precision=lax.Precision.HIGHEST, preferred_element_type=f32)
    if passes == 1:
        return lax.dot_general(a.astype(bf16), b.astype(bf16), dims, preferred_element_type=f32)
    ah, al = _split(a)
    bh, bl = _split(b)
    d = lambda x, y: lax.dot_general(x, y, dims, preferred_element_type=f32)
    return d(ah, bh) + (d(ah, bl) + d(al, bh))


def _mm_sel(sel, x, dims=NN, sel_first=True):
    sel = sel.astype(bf16)
    x1 = x.astype(bf16)
    r1 = x - x1.astype(f32)
    x2 = r1.astype(bf16)
    x3 = (r1 - x2.astype(f32)).astype(bf16)
    if sel_first:
        d = lambda y: lax.dot_general(sel, y, dims, preferred_element_type=f32)
    else:
        d = lambda y: lax.dot_general(y, sel, dims, preferred_element_type=f32)
    return d(x1) + (d(x2) + d(x3))


def _softplus(x):
    return jnp.maximum(x, 0.0) + jnp.log1p(jnp.exp(-jnp.abs(x)))


def _sigmoid(x):
    return 0.5 * jnp.tanh(0.5 * x) + 0.5


def _silu(x):
    return x * _sigmoid(x)


def _rms(x):
    return x * lax.rsqrt(jnp.mean(x * x, axis=-1, keepdims=True) + EPS)


def _rows(parts):
    return parts[0] if len(parts) == 1 else jnp.concatenate(parts, axis=0)


def _inproj_kernel(x_ref, nw_ref, w_ref, o_ref, xn_ref):
    @pl.when(pl.program_id(1) == 0)
    def _():
        xn_ref[...] = (_rms(x_ref[...]) * nw_ref[...]).astype(bf16)

    o_ref[...] = lax.dot_general(xn_ref[...], w_ref[...], NT, preferred_element_type=f32)


def _inproj(x, nw, w, layer, tm, tn):
    m = x.shape[0]
    return pl.pallas_call(
        _inproj_kernel,
        grid=(pl.cdiv(m, tm), N_PACK // tn),
        in_specs=[
            pl.BlockSpec((tm, D_MODEL), lambda i, j: (i, 0)),
            pl.BlockSpec((None, 1, D_MODEL), lambda i, j: (layer, 0, 0)),
            pl.BlockSpec((None, tn, D_MODEL), lambda i, j: (layer, j, 0)),
        ],
        out_specs=pl.BlockSpec((tm, tn), lambda i, j: (i, j)),
        out_shape=jax.ShapeDtypeStruct((m, N_PACK), f32),
        scratch_shapes=[pltpu.VMEM((tm, D_MODEL), bf16)],
        compiler_params=pltpu.CompilerParams(
            dimension_semantics=("arbitrary", "arbitrary"), vmem_limit_bytes=VMEM_LIMIT),
        name="inproj",
    )(x, nw, w)


def _outproj_kernel(final, m_ref, w_ref, x_ref, fn_ref, o_ref):
    y = x_ref[...] + jnp.dot(m_ref[...], w_ref[...], preferred_element_type=f32)
    if final:
        y = _rms(y) * fn_ref[...]
    o_ref[...] = y


def _outproj(mixed, w, layer, x, fn, final, tm):
    m = x.shape[0]
    return pl.pallas_call(
        functools.partial(_outproj_kernel, final),
        grid=(pl.cdiv(m, tm),),
        in_specs=[
            pl.BlockSpec((tm, D_MODEL), lambda i: (i, 0)),
            pl.BlockSpec((None, D_MODEL, D_MODEL), lambda i: (layer, 0, 0)),
            pl.BlockSpec((tm, D_MODEL), lambda i: (i, 0)),
            pl.BlockSpec((1, D_MODEL), lambda i: (0, 0)),
        ],
        out_specs=pl.BlockSpec((tm, D_MODEL), lambda i: (i, 0)),
        out_shape=jax.ShapeDtypeStruct((m, D_MODEL), f32),
        compiler_params=pltpu.CompilerParams(
            dimension_semantics=("arbitrary",), vmem_limit_bytes=VMEM_LIMIT),
        name="outproj",
    )(mixed, w, x, fn)


class _Cfg:
    def __init__(self, rows, seg, voff, vlen, nc, ng, gblocks, embedded, blk_off):
        self.rows, self.seg, self.voff, self.vlen = rows, seg, voff, vlen
        self.nc, self.ng, self.gblocks, self.embedded, self.blk_off = nc, ng, gblocks, embedded, blk_off
        self.nseg = rows // seg
        self.nstate = ng * self.nseg
        self.all_valid = (voff == 0 and vlen == seg)


def _tri_inverse(a_list, seg, row, col, eye):
    def blk(s):
        return (row // s) == (col // s)

    b8 = blk(8)
    n = [-jnp.where(b8, a, 0.0) for a in a_list]
    mm = _mm
    n2 = [mm(x, x) for x in n]
    yield
    n4 = [mm(x, x) for x in n2]
    t = [mm(eye + x, eye + y) for x, y in zip(n, n2)]
    yield
    t = [mm(x, eye + y) for x, y in zip(t, n4)]
    s = 8
    while s < seg:
        yield
        mask = blk(2 * s) & jnp.logical_not(blk(s))
        off = [jnp.where(mask, a, 0.0) for a in a_list]
        tb = [mm(x, o) for x, o in zip(t, off)]
        yield
        t = [x - mm(y, x) for x, y in zip(t, tb)]
        s *= 2
    yield
    resid = [eye - x - _mm(a, x, passes=P_SOLVE) for a, x in zip(a_list, t)]
    yield
    return [x + mm(x, r) for x, r in zip(t, resid)]


def _mixer_kernel(cfg, n_alias, *refs):
    R, SEG, NSEG, NG, NST = cfg.rows, cfg.seg, cfg.nseg, cfg.ng, cfg.nstate
    it = iter(refs)
    proj = next(it)
    if cfg.embedded:
        buf_a, buf_b = next(it), next(it)
    else:
        ic_a, ic_b = next(it), next(it)
    sd_in, h_in, sg_in, sr_in = next(it), next(it), next(it), next(it)
    cos_t, sin_t, intra_t, fs_t, ts_t, cd_t = (next(it) for _ in range(6))
    (conv_a, alogv, dtbv, norm_a, conv_b, cbb, wa, wx, ba, bx, lam, w2p, b2, norm_c) = (
        next(it) for _ in range(14))
    for _ in range(n_alias):
        next(it)
    mixed, sd, conv_a_o, h_o, conv_b_o, sg, sr = (next(it) for _ in range(7))
    xp_a, xp_b, l_a, l_b, l_o, gla_attn = (next(it) for _ in range(6))
    gla = {}

    c = pl.program_id(1)
    groups = range(NG)
    heads = range(N_HEADS)
    units = [(g, hd) for g in groups for hd in heads]
    nu = len(units)
    seg_rows = [slice(s * SEG, (s + 1) * SEG) for s in range(NSEG)]

    @pl.when(c == 0)
    def _init():
        shared = sd_in.shape[0] != NST
        for i in range(NST):
            j = 0 if shared else i
            sd[i] = sd_in[j]
            sg[i] = sg_in[j]
            sr[i] = sr_in[j]
        h_o[...] = jnp.broadcast_to(h_in[...], (NST, GROUP_W))
        for g in groups:
            if cfg.embedded:
                xp_a[g, 0:8, :] = jnp.zeros((8, QKV_W), f32)
                xp_b[g, 0:8, :] = jnp.zeros((8, GROUP_W), f32)
            else:
                xp_a[g, 0:8, :] = ic_a[...]
                xp_b[g, 0:8, :] = ic_b[...]
            xp_a[g, 8 + R:16 + R, :] = jnp.zeros((8, QKV_W), f32)
            xp_b[g, 8 + R:16 + R, :] = jnp.zeros((8, GROUP_W), f32)

    row = lax.broadcasted_iota(jnp.int32, (R, R), 0)
    col = lax.broadcasted_iota(jnp.int32, (R, R), 1)
    same = (row // SEG) == (col // SEG)
    incl = same & (col <= row)
    strict = same & (col < row)
    eye = (row == col).astype(f32)
    l_incl = incl.astype(f32)
    m_same = same.astype(f32)
    rmod = lax.broadcasted_iota(jnp.int32, (R, 1), 0) % SEG
    valid = (rmod >= cfg.voff) & (rmod < cfg.voff + cfg.vlen)
    is_hist = rmod < TILE_OFF

    def conv(xp, g, x, buf, w_ref):
        if cfg.embedded:
            x = jnp.where(is_hist, buf[g], x)
        xp[g, 8:8 + R, :] = x
        full = xp[g, 0:8 + R, :]
        y = w_ref[CONV_K - 1:CONV_K, :] * x
        for s in range(1, CONV_K):
            y = y + w_ref[CONV_K - 1 - s:CONV_K - s, :] * pltpu.roll(full, s, 0)[8:8 + R]
        return y

    def conv_with_history(xp, x_of, buf, w_ref, hist_out):
        ys = [conv(xp, g, x_of(g), buf, w_ref) for g in groups]
        for g in groups:
            if cfg.embedded:
                hist_out[g] = xp[g, 8 + TILE_OFF + 1:8 + TILE_OFF + 1 + R, :]
            else:
                xp[g, 0:8, :] = xp[g, R:R + 8, :]
        return ys

    small = [proj[g, :, OFF_SMALL:OFF_SMALL + 128] for g in groups]

    def put(g, k, o):
        gate = proj[g, :, OFF_GATE + k * GROUP_W:OFF_GATE + (k + 1) * GROUP_W]
        mixed[g, :, k * GROUP_W:(k + 1) * GROUP_W] = (o * _silu(gate)).astype(mixed.dtype)


    def lru_stream():
        yb = conv_with_history(xp_b, lambda g: proj[g, :, OFF_XB:OFF_XB + GROUP_W],
                               buf_b if cfg.embedded else None, conv_b, conv_b_o)
        yb = [y + cbb[...] for y in yb]
        sp_lam = _softplus(-lam[...])
        r_pre = [[_mm(yb[g][:, n * 128:(n + 1) * 128], wa[n]) for n in heads] for g in groups]
        i_pre = [[_mm(yb[g][:, n * 128:(n + 1) * 128], wx[n]) for n in heads] for g in groups]
        h_all = h_o[...]
        h_cur = {}
        for g, n in units:
            ls = slice(n * 128, (n + 1) * 128)
            x_n = yb[g][:, ls]
            log_a = -LRU_C * _sigmoid(r_pre[g][n] + ba[:, ls]) * sp_lam[:, ls]
            a_t = jnp.exp(log_a)
            b_t = jnp.sqrt(-jnp.tanh(log_a) * (a_t * a_t + 1.0)) * (_sigmoid(i_pre[g][n] + bx[:, ls]) * x_n)
            l_a[g * N_HEADS + n] = a_t
            l_b[g * N_HEADS + n] = b_t
            if not cfg.all_valid:
                l_o[g * N_HEADS + n] = jnp.zeros((R, 128), f32)
            h_cur[g, n] = h_all[g * NSEG:(g + 1) * NSEG, ls]
        yield
        for t in range(cfg.vlen):
            idx = pl.ds(cfg.voff + t, 1) if NSEG == 1 else pl.ds(cfg.voff + t, NSEG, stride=SEG)
            for g, n in units:
                k = g * N_HEADS + n
                h_cur[g, n] = l_a[k, idx, :] * h_cur[g, n] + l_b[k, idx, :]
                l_o[k, idx, :] = h_cur[g, n]
            if t % SCAN_STEPS_PER_SLOT == SCAN_STEPS_PER_SLOT - 1:
                yield
        h_o[...] = _rows([jnp.concatenate([h_cur[g, n] for n in heads], axis=1) for g in groups])
        for g in groups:
            put(g, 1, jnp.concatenate([l_o[g * N_HEADS + n] for n in heads], axis=1))

    def delta_stream():
        ya = conv_with_history(xp_a, lambda g: proj[g, :, OFF_QKV:OFF_QKV + QKV_W],
                               buf_a if cfg.embedded else None, conv_a, conv_a_o)
        qkv = [_silu(y) for y in ya]
        g_all, beta_all = [], []
        for g in groups:
            ga = -jnp.exp(alogv[...]) * _softplus(small[g] + dtbv[...])
            be = _sigmoid(small[g])
            if not cfg.all_valid:
                ga = jnp.where(valid, ga, 0.0)
                be = jnp.where(valid, be, 0.0)
            g_all.append(ga)
            beta_all.append(be)
        gcum = [_mm_sel(l_incl, x) for x in g_all]
        gtot = [_mm_sel(m_same, x) for x in g_all]
        gcum_t = [x.T for x in gcum]

        q_l, k_l, v_l, be_l, gc_l, gt_l, dec_l, eg_l = ([] for _ in range(8))
        for g, hd in units:
            q = qkv[g][:, hd * 128:(hd + 1) * 128]
            k = qkv[g][:, 512 + hd * 128:512 + (hd + 1) * 128]
            q_l.append(q * lax.rsqrt(jnp.sum(q * q, axis=-1, keepdims=True) + EPS) * (DN_DK ** -0.5))
            k_l.append(k * lax.rsqrt(jnp.sum(k * k, axis=-1, keepdims=True) + EPS))
            v_l.append(qkv[g][:, 1024 + hd * 128:1024 + (hd + 1) * 128])
            be_l.append(beta_all[g][:, SM_BETA + hd:SM_BETA + hd + 1])
            gc = gcum[g][:, SM_ALPHA + hd:SM_ALPHA + hd + 1]
            gr = gcum_t[g][SM_ALPHA + hd:SM_ALPHA + hd + 1, :]
            gc_l.append(gc)
            gt_l.append(gtot[g][:, SM_ALPHA + hd:SM_ALPHA + hd + 1])
            dec_l.append(jnp.where(incl, jnp.exp(jnp.where(incl, gc - gr, 0.0)), 0.0))
            eg_l.append(jnp.exp(gc))
        kb_l = [k_l[u] * be_l[u] for u in range(nu)]
        a_l = [jnp.where(strict, _mm(kb_l[u], k_l[u], NT) * dec_l[u], 0.0) for u in range(nu)]
        t_l = yield from _tri_inverse(a_l, SEG, row, col, eye)
        attn_l = [jnp.where(incl, _mm(q_l[u], k_l[u], NT) * dec_l[u], 0.0) for u in range(nu)]
        uw_l = [_mm(t_l[u], jnp.concatenate([v_l[u] * be_l[u], kb_l[u] * eg_l[u]], axis=1), passes=P_SOLVE)
                for u in range(nu)]
        st_l = []
        for u, (g, hd) in enumerate(units):
            qe = q_l[u] * eg_l[u]
            w = uw_l[u][:, 128:]
            st_l.append([_mm(jnp.concatenate([qe[rs], w[rs]], axis=0), sd[g * NSEG + s, hd])
                         for s, rs in enumerate(seg_rows)])
        vn_l = [uw_l[u][:, :128] - _rows([b[SEG:] for b in st_l[u]]) for u in range(nu)]
        o_l = [_rows([b[:SEG] for b in st_l[u]]) + _mm(attn_l[u], vn_l[u]) for u in range(nu)]
        for u, (g, hd) in enumerate(units):
            kd = k_l[u] * jnp.exp(gt_l[u] - gc_l[u])
            for s, rs in enumerate(seg_rows):
                g_last = jnp.exp(gt_l[u][s * SEG:s * SEG + 1, :])
                i = g * NSEG + s
                sd[i, hd] = sd[i, hd] * g_last + _mm(kd[rs], vn_l[u][rs], TN)
        for g in groups:
            put(g, 0, jnp.concatenate(
                [_rms(o_l[g * N_HEADS + hd]) * norm_a[:, hd * 128:(hd + 1) * 128] for hd in heads], axis=1))

    def gla_stream():
        lg = []
        for g in groups:
            x_gate = _mm(small[g], w2p[...]) + b2[...]
            z = -_softplus(-x_gate) * (1.0 / GLA_TAU)
            lg.append(z if cfg.all_valid else jnp.where(valid, z, 0.0))
        bcum = [_mm_sel(l_incl, x) for x in lg]
        btot = [_mm_sel(m_same, x) for x in lg]
        ones = jnp.ones((SEG, 128), f32)
        dec_s = [[jnp.exp(_mm_sel(ones, lg[g][rs], TN, sel_first=False)) for rs in seg_rows]
                 for g in groups]
        q_cs, k_cs, qe_c, kd_c = [], [], [], []
        for g in groups:
            q_c = proj[g, :, OFF_QC:OFF_QC + 256] * (GLA_DK ** -0.5)
            k_c = proj[g, :, OFF_KC:OFF_KC + 256]
            if not cfg.all_valid:
                k_c = jnp.where(valid, k_c, 0.0)
            q_cs.append(q_c)
            k_cs.append(k_c)
            qe_c.append(q_c * jnp.exp(bcum[g]))
            kd_c.append(k_c * jnp.exp(btot[g] - bcum[g]))
        yield
        ks_l = [slice(hd * GLA_DK, (hd + 1) * GLA_DK) for hd in heads]
        vc_l = [proj[g, :, OFF_VC + hd * 128:OFF_VC + (hd + 1) * 128] for g, hd in units]
        o_state = []
        for u, (g, hd) in enumerate(units):
            parts = []
            for s, rs in enumerate(seg_rows):
                i = g * NSEG + s
                parts.append(_mm(qe_c[g][rs, ks_l[hd]], sg[i, hd]))
                sg[i, hd] = sg[i, hd] * dec_s[g][s][ks_l[hd], :] + _mm(kd_c[g][rs, ks_l[hd]], vc_l[u][rs], TN)
            o_state.append(_rows(parts))
        gla.update(lg=lg, bcum=bcum, btot=btot, q=q_cs, k=k_cs, qe=qe_c, ks=ks_l, v=vc_l, o_state=o_state)
        if not branch_on_range:
            yield
            yield from gla_scores_levels()
            yield
            gla_finish()

    def gla_scores_plain():
        for u, (g, hd) in enumerate(units):
            ks = gla["ks"][hd]
            ke = gla["k"][g][:, ks] * jnp.exp(-gla["bcum"][g][:, ks])
            gla_attn[u] = jnp.where(incl, _mm(gla["qe"][g][:, ks], ke, NT), 0.0)

    def gla_scores_levels():
        halves = [SEG >> (i + 1) for i in range(SEG.bit_length() - 1)]
        nlev = len(halves)
        between = []
        for h in halves:
            ref = (row // (2 * h)) * (2 * h) + h - 1
            between.append(((col > ref) & (col <= row)) | ((col > row) & (col <= ref)))
        between = jnp.concatenate(between, axis=0).astype(f32)
        pair = [((row // (2 * h)) == (col // (2 * h))) & ((row % (2 * h)) >= h) & ((col % (2 * h)) < h)
                for h in halves]
        rep = lambda x: jnp.concatenate([x] * nlev, axis=0)
        e_lv = [jnp.exp(_mm_sel(between, gla["lg"][g])) for g in groups]
        q_lv = [rep(gla["q"][g]) * e_lv[g] for g in groups]
        k_lv = [rep(gla["k"][g]) * e_lv[g] for g in groups]
        yield
        ks_l = gla["ks"]
        attn_l = [jnp.where(eye > 0.0, _mm(gla["q"][g][:, ks_l[hd]], gla["k"][g][:, ks_l[hd]], NT), 0.0)
                  for g, hd in units]
        for li in range(nlev):
            lv = slice(li * R, (li + 1) * R)
            attn_l = [jnp.where(pair[li], _mm(q_lv[g][lv, ks_l[hd]], k_lv[g][lv, ks_l[hd]], NT), attn_l[u])
                      for u, (g, hd) in enumerate(units)]
            if li % 2 == 1:
                yield
        for u in range(nu):
            gla_attn[u] = attn_l[u]

    branch_on_range = SEG > GLA_ALWAYS_LEVELS_SEG

    def gla_finish():
        if branch_on_range:
            worst = jnp.max(-gla["btot"][0])
            for g in list(groups)[1:]:
                worst = jnp.maximum(worst, jnp.max(-gla["btot"][g]))
            plain_ok = worst < GLA_PLAIN_RANGE
            pl.when(plain_ok)(gla_scores_plain)

            @pl.when(jnp.logical_not(plain_ok))
            def _():
                for _stage in gla_scores_levels():
                    pass
        o_l = [_mm(gla_attn[u], gla["v"][u]) + gla["o_state"][u] for u in range(nu)]
        for g in groups:
            put(g, 2, jnp.concatenate(
                [_rms(o_l[g * N_HEADS + hd]) * norm_c[:, hd * 128:(hd + 1) * 128] for hd in heads], axis=1))

    def ret_stream():
        half = RET_DK // 2
        lane = lax.broadcasted_iota(jnp.int32, (R, 256), 1)
        first_half = (lane % RET_DK) < half

        def rotary(x):
            rot = jnp.where(first_half, pltpu.roll(x, 256 - half, 1), pltpu.roll(x, half, 1))
            return x * cos_t[...] + rot * sin_t[...]

        q_d = [rotary(proj[g, :, OFF_QD:OFF_QD + 256]) for g in groups]
        k_d = [rotary(proj[g, :, OFF_KD:OFF_KD + 256]) * (RET_DK ** -0.5) for g in groups]
        qf = [x * fs_t[...] for x in q_d]
        kt = [x * ts_t[...] for x in k_d]
        yield
        ks_l = [slice(hd * RET_DK, (hd + 1) * RET_DK) for hd in heads]
        vd_l = [proj[g, :, OFF_VD + hd * 128:OFF_VD + (hd + 1) * 128] for g, hd in units]
        attn_l = [_mm(q_d[g][:, ks_l[hd]], k_d[g][:, ks_l[hd]], NT) * intra_t[hd] for g, hd in units]
        o_l = [_mm(attn_l[u], vd_l[u]) for u in range(nu)]
        for u, (g, hd) in enumerate(units):
            parts = []
            for s, rs in enumerate(seg_rows):
                i = g * NSEG + s
                parts.append(_mm(qf[g][rs, ks_l[hd]], sr[i, hd]))
                sr[i, hd] = sr[i, hd] * cd_t[hd, 0:1, :] + _mm(kt[g][rs, ks_l[hd]], vd_l[u][rs], TN)
            o_l[u] = o_l[u] + _rows(parts)
        for g in groups:
            put(g, 3, jnp.concatenate([_rms(o_l[g * N_HEADS + hd]) for hd in heads], axis=1))

    others = itertools.chain(lru_stream(), gla_stream(), ret_stream())
    streams = [delta_stream(), others]
    while streams:
        for stream in list(streams):
            if next(stream, _DONE) is _DONE:
                streams.remove(stream)
    if branch_on_range:
        gla_finish()

    if not cfg.embedded:
        @pl.when(c == cfg.nc - 1)
        def _():
            for g in groups:
                conv_a_o[g] = xp_a[g, 5:8, :]
                conv_b_o[g] = xp_b[g, 5:8, :]


def _mixer(cfg, layer, proj, hist, states, lin, tables, weights, n_out_layers, lout, prev):
    R, NG, GB, NC, NST = cfg.rows, cfg.ng, cfg.gblocks, cfg.nc, cfg.nstate
    sd_in, h_in, sg_in, sr_in = states
    shared = sd_in.shape[1] != GB * NST
    nin = 1 if shared else NST

    def st(i):
        return 0 if shared else i

    if cfg.embedded:
        hist_specs = [pl.BlockSpec((None, NG, R, QKV_W), lambda gb, c: (layer, gb, c, 0)),
                      pl.BlockSpec((None, NG, R, GROUP_W), lambda gb, c: (layer, gb, c, 0))]
    else:
        hist_specs = [pl.BlockSpec((8, QKV_W), lambda gb, c: (0, 0)),
                      pl.BlockSpec((8, GROUP_W), lambda gb, c: (0, 0))]
    in_specs = [pl.BlockSpec((NG, R, N_PACK), lambda gb, c: (cfg.blk_off + gb, c, 0))] + hist_specs + [
        pl.BlockSpec((None, nin, N_HEADS, DN_DK, HEAD_V), lambda gb, c: (lin, st(gb), 0, 0, 0)),
        pl.BlockSpec((None, None, nin, GROUP_W), lambda gb, c: (lin, st(gb), 0, 0)),
        pl.BlockSpec((None, nin, N_HEADS, GLA_DK, HEAD_V), lambda gb, c: (lin, st(gb), 0, 0, 0)),
        pl.BlockSpec((None, nin, N_HEADS, RET_DK, HEAD_V), lambda gb, c: (lin, st(gb), 0, 0, 0)),
        pl.BlockSpec((R, 256), lambda gb, c: (c, 0)),
        pl.BlockSpec((R, 256), lambda gb, c: (c, 0)),
        pl.BlockSpec((N_HEADS, R, R), lambda gb, c: (0, 0, 0)),
        pl.BlockSpec((R, 256), lambda gb, c: (0, 0)),
        pl.BlockSpec((R, 256), lambda gb, c: (0, 0)),
        pl.BlockSpec((N_HEADS, 8, 128), lambda gb, c: (0, 0, 0)),
    ]
    for w in weights:
        in_specs.append(pl.BlockSpec((None,) + w.shape[1:], lambda gb, c, nd=w.ndim: (layer,) + (0,) * (nd - 1)))
    aliases = {}
    if prev is not None:
        for k, p in enumerate(prev):
            aliases[len(in_specs)] = 1 + k
            in_specs.append(pl.BlockSpec(memory_space=pl.ANY))

    n_seq_rows = NC * R
    conv_rows = n_seq_rows if cfg.embedded else CONV_K - 1
    nl = n_out_layers
    out_shape = [
        jax.ShapeDtypeStruct((GB * NG, n_seq_rows, D_MODEL), bf16),
        jax.ShapeDtypeStruct((nl, GB * NST, N_HEADS, DN_DK, HEAD_V), f32),
        jax.ShapeDtypeStruct((nl, GB * NG, conv_rows, QKV_W), f32),
        jax.ShapeDtypeStruct((nl, GB, NST, GROUP_W), f32),
        jax.ShapeDtypeStruct((nl, GB * NG, conv_rows, GROUP_W), f32),
        jax.ShapeDtypeStruct((nl, GB * NST, N_HEADS, GLA_DK, HEAD_V), f32),
        jax.ShapeDtypeStruct((nl, GB * NST, N_HEADS, RET_DK, HEAD_V), f32),
    ]
    if cfg.embedded:
        conv_specs = [pl.BlockSpec((None, NG, R, QKV_W), lambda gb, c: (lout, gb, c, 0)),
                      pl.BlockSpec((None, NG, R, GROUP_W), lambda gb, c: (lout, gb, c, 0))]
    else:
        conv_specs = [pl.BlockSpec((None, NG, CONV_K - 1, QKV_W), lambda gb, c: (lout, gb, 0, 0)),
                      pl.BlockSpec((None, NG, CONV_K - 1, GROUP_W), lambda gb, c: (lout, gb, 0, 0))]
    out_specs = [
        pl.BlockSpec((NG, R, D_MODEL), lambda gb, c: (gb, c, 0)),
        pl.BlockSpec((None, NST, N_HEADS, DN_DK, HEAD_V), lambda gb, c: (lout, gb, 0, 0, 0)),
        conv_specs[0],
        pl.BlockSpec((None, None, NST, GROUP_W), lambda gb, c: (lout, gb, 0, 0)),
        conv_specs[1],
        pl.BlockSpec((None, NST, N_HEADS, GLA_DK, HEAD_V), lambda gb, c: (lout, gb, 0, 0, 0)),
        pl.BlockSpec((None, NST, N_HEADS, RET_DK, HEAD_V), lambda gb, c: (lout, gb, 0, 0, 0)),
    ]
    scratch = [
        pltpu.VMEM((NG, R + 16, QKV_W), f32),
        pltpu.VMEM((NG, R + 16, GROUP_W), f32),
        pltpu.VMEM((NG * N_HEADS, R, 128), f32),
        pltpu.VMEM((NG * N_HEADS, R, 128), f32),
        pltpu.VMEM((NG * N_HEADS, R, 128), f32),
        pltpu.VMEM((NG * N_HEADS, R, R), f32),
    ]
    return pl.pallas_call(
        functools.partial(_mixer_kernel, cfg, len(aliases)),
        grid=(GB, NC),
        in_specs=in_specs,
        out_specs=out_specs,
        out_shape=out_shape,
        scratch_shapes=scratch,
        input_output_aliases=aliases,
        compiler_params=pltpu.CompilerParams(
            dimension_semantics=("arbitrary", "arbitrary"), vmem_limit_bytes=VMEM_LIMIT),
        name="mixer_r%d_s%d_g%d" % (R, cfg.seg, NG),
    )(proj, *hist, sd_in, h_in, sg_in, sr_in, *tables, *weights, *(prev or ()))


def _rope_tables(pos):
    half = RET_DK // 2
    freqs = ROPE_BASE ** (-jnp.arange(half, dtype=f32) / half)
    ang = pos.astype(f32)[:, None] * freqs
    cos, sin = jnp.cos(ang), jnp.sin(ang)
    cos_h = jnp.concatenate([cos, cos], axis=1)
    sin_h = jnp.concatenate([-sin, sin], axis=1)
    return jnp.tile(cos_h, (1, N_HEADS)), jnp.tile(sin_h, (1, N_HEADS))


def _ret_tables(rows, seg, voff, vlen):
    log_gamma = jnp.log(1.0 - 2.0 ** (-5.0 - jnp.arange(N_HEADS, dtype=f32)))
    r = jnp.arange(rows)
    p = (r % seg - voff).astype(f32)
    ok = ((r % seg) >= voff) & ((r % seg) < voff + vlen)
    rel = p[:, None] - p[None, :]
    pair = ok[:, None] & ok[None, :] & ((r[:, None] // seg) == (r[None, :] // seg)) & (rel >= 0)
    intra = jnp.where(pair[None], jnp.exp(log_gamma[:, None, None] * jnp.maximum(rel, 0.0)[None]), 0.0)
    from_state = jnp.exp(log_gamma[:, None] * (p + 1.0))
    to_state = jnp.where(ok[None], jnp.exp(log_gamma[:, None] * (vlen - 1.0 - p)), 0.0)
    chunk_decay = jnp.exp(log_gamma * vlen)
    fs = jnp.repeat(from_state.T, RET_DK, axis=1)
    ts = jnp.repeat(to_state.T, RET_DK, axis=1)
    cd = jnp.broadcast_to(chunk_decay[:, None, None], (N_HEADS, 8, 128))
    return intra.astype(f32), fs.astype(f32), ts.astype(f32), cd.astype(f32)


SRC_AB = QKV_W
SRC_RUN1 = SRC_AB + 2 * N_HEADS
SRC_RC = SRC_RUN1 + (OFF_QD - OFF_XB)
SRC_RUN2 = SRC_RC + GLA_RANK
SRC_W = SRC_RUN2 + (OFF_SMALL - OFF_QD)
PACK_ROWS = 256
N_WIDE_BLOCKS = OFF_SMALL // PACK_ROWS


def _pack_kernel(src_ref, ab_ref, rc_ref, o_ref):
    i = pl.program_id(1)

    @pl.when(i < N_WIDE_BLOCKS)
    def _():
        o_ref[...] = src_ref[0].astype(bf16)

    @pl.when(i == N_WIDE_BLOCKS)
    def _():
        pad = jnp.zeros((N_PACK - OFF_SMALL - 2 * N_HEADS - GLA_RANK, D_MODEL), f32)
        o_ref[...] = jnp.concatenate([ab_ref[0], rc_ref[0], pad], axis=0).astype(bf16)


def _pack_src_row(i):
    shift = jnp.where(i < OFF_XB // PACK_ROWS, 0,
                      jnp.where(i < OFF_QD // PACK_ROWS, SRC_RUN1 - OFF_XB, SRC_RUN2 - OFF_QD))
    return pl.multiple_of(jnp.minimum(i * PACK_ROWS + shift, SRC_W - PACK_ROWS), 8)


def _pack_w_in(w_in_t):
    depth, n, d = w_in_t.shape
    assert n == SRC_W and d == D_MODEL and OFF_XB % PACK_ROWS == 0 and OFF_QD % PACK_ROWS == 0
    assert N_PACK - OFF_SMALL == PACK_ROWS
    return pl.pallas_call(
        _pack_kernel,
        grid=(depth, N_PACK // PACK_ROWS),
        in_specs=[
            pl.BlockSpec((pl.Element(1), pl.Element(PACK_ROWS), pl.Element(d)),
                         lambda l, i: (l, _pack_src_row(i), 0)),
            pl.BlockSpec((pl.Element(1), pl.Element(2 * N_HEADS), pl.Element(d)), lambda l, i: (l, SRC_AB, 0)),
            pl.BlockSpec((pl.Element(1), pl.Element(GLA_RANK), pl.Element(d)), lambda l, i: (l, SRC_RC, 0)),
        ],
        out_specs=pl.BlockSpec((None, PACK_ROWS, d), lambda l, i: (l, i, 0)),
        out_shape=jax.ShapeDtypeStruct((depth, N_PACK, d), bf16),
        compiler_params=pltpu.CompilerParams(dimension_semantics=("arbitrary", "arbitrary")),
        name="pack_w_in",
    )(w_in_t, w_in_t, w_in_t)


def _lanes(vec, off, width=128):
    return jnp.pad(vec[None, :], ((0, 0), (off, width - off - vec.shape[0])))


def kernel(x_prompt, x_sample, state_delta, state_delta_conv, state_lru, state_lru_conv, state_gla,
           state_ret, meta_tokens, norm_w, w_in, conv_a, a_log, dt_bias, norm_a, conv_b, conv_b_bias,
           lru_wa, lru_ba, lru_wx, lru_bx, lru_lambda, gla_w2, gla_b2, norm_c, w_out, final_norm):
    depth = w_in.shape[0]
    bp, lp = x_prompt.shape[0], x_prompt.shape[1]
    bs, ls = x_sample.shape[0], x_sample.shape[1]
    assert lp % CHUNK == 0 and ls == CONV_K and TILE_OFF + ls <= TILE
    nc_main = lp // CHUNK
    n_tile_rows = bs * TILE
    assert n_tile_rows % DEC_ROWS == 0 and DEC_ROWS % N_META == 0
    n_dec_blocks = n_tile_rows // DEC_ROWS
    seq_per_block = DEC_ROWS // TILE

    w_in_p = _pack_w_in(jnp.swapaxes(w_in, 1, 2))
    w_out_b = w_out.astype(bf16)

    h_main = x_prompt.reshape(bp * lp, D_MODEL)
    tiles = jnp.pad(x_sample, ((0, 0), (TILE_OFF, TILE - TILE_OFF - ls), (0, 0)))
    h_small = jnp.concatenate(
        [tiles.reshape(n_tile_rows, D_MODEL), meta_tokens.astype(x_prompt.dtype),
         jnp.zeros((DEC_ROWS - N_META, D_MODEL), x_prompt.dtype)], axis=0)
    n_small = h_small.shape[0]
    tm_small = n_small // 2
    assert tm_small % 8 == 0

    cfg_main = _Cfg(CHUNK, CHUNK, 0, CHUNK, nc_main, MAIN_NG, bp // MAIN_NG, False, 0)
    cfg_meta = _Cfg(N_META, N_META, 0, N_META, 1, 1, 1, False, n_tile_rows // N_META)
    assert n_dec_blocks % DEC_NG == 0
    cfg_dec = _Cfg(DEC_ROWS, TILE, TILE_OFF, ls, 1, DEC_NG, n_dec_blocks // DEC_NG, True, 0)

    pos_main = N_META + jnp.arange(lp)
    pos_meta = jnp.arange(N_META)
    pos_dec = jnp.tile(PAST_LEN + jnp.arange(TILE) - TILE_OFF, seq_per_block)
    tab_main = _rope_tables(pos_main) + _ret_tables(CHUNK, CHUNK, 0, CHUNK)
    tab_meta = _rope_tables(pos_meta) + _ret_tables(N_META, N_META, 0, N_META)
    tab_dec = _rope_tables(pos_dec) + _ret_tables(DEC_ROWS, TILE, TILE_OFF, ls)

    zeros_meta = (
        jnp.zeros((1, 1, N_HEADS, DN_DK, HEAD_V), f32), jnp.zeros((1, 1, 1, GROUP_W), f32),
        jnp.zeros((1, 1, N_HEADS, GLA_DK, HEAD_V), f32), jnp.zeros((1, 1, N_HEADS, RET_DK, HEAD_V), f32))
    zero_hist = (jnp.zeros((8, QKV_W), f32), jnp.zeros((8, GROUP_W), f32))

    weights = (
        conv_a,
        jnp.pad(a_log[:, None, :], ((0, 0), (0, 0), (SM_ALPHA, 128 - SM_ALPHA - N_HEADS))),
        jnp.pad(dt_bias[:, None, :], ((0, 0), (0, 0), (SM_ALPHA, 128 - SM_ALPHA - N_HEADS))),
        jnp.tile(norm_a, (1, N_HEADS))[:, None, :], conv_b, conv_b_bias[:, None, :],
        lru_wa, lru_wx, lru_ba[:, None, :], lru_bx[:, None, :], lru_lambda[:, None, :],
        jnp.pad(gla_w2, ((0, 0), (SM_RC, 128 - SM_RC - GLA_RANK), (0, 0))), gla_b2[:, None, :],
        jnp.tile(norm_c, (1, N_HEADS))[:, None, :],
    )
    nw = norm_w[:, None, :]
    fn = final_norm[None]

    pad_tile = ((0, 0), (0, 0), (0, TILE - (CONV_K - 1)), (0, 0))
    hist_dec = (jnp.pad(state_delta_conv, pad_tile).reshape(depth, n_dec_blocks, DEC_ROWS, QKV_W),
                jnp.pad(state_lru_conv, pad_tile).reshape(depth, n_dec_blocks, DEC_ROWS, GROUP_W))
    st_dec = (state_delta, state_lru.reshape(depth, cfg_dec.gblocks, cfg_dec.nstate, GROUP_W), state_gla, state_ret)

    p_st, s_st = None, None
    for l in range(depth):
        last = l == depth - 1
        proj_main = _inproj(h_main, nw, w_in_p, l, TM_MAIN_IN, TN_IN)
        proj_small = _inproj(h_small, nw, w_in_p, l, n_small, TN_IN)

        mx_meta, sd_m, ca_m, h_m, cb_m, sg_m, sr_m = _mixer(
            cfg_meta, l, proj_small.reshape(n_small // N_META, N_META, N_PACK), zero_hist, zeros_meta, 0,
            tab_meta, weights, 1, 0, None)
        hist_main = (jnp.pad(ca_m[0, 0], ((8 - (CONV_K - 1), 0), (0, 0))),
                     jnp.pad(cb_m[0, 0], ((8 - (CONV_K - 1), 0), (0, 0))))
        mx_main, *p_st = _mixer(
            cfg_main, l, proj_main.reshape(bp, lp, N_PACK), hist_main, (sd_m, h_m, sg_m, sr_m), 0,
            tab_main, weights, depth, l, p_st)

        mx_dec, *s_st = _mixer(
            cfg_dec, l, proj_small.reshape(n_small // DEC_ROWS, DEC_ROWS, N_PACK), hist_dec, st_dec, l,
            tab_dec, weights, depth, l, s_st)

        h_main = _outproj(mx_main.reshape(bp * lp, D_MODEL), w_out_b, l, h_main, fn, last, TM_MAIN_OUT)
        mx_small = jnp.concatenate(
            [mx_dec.reshape(n_tile_rows, D_MODEL), mx_meta[0],
             jnp.zeros((DEC_ROWS - N_META, D_MODEL), bf16)], axis=0)
        h_small = _outproj(mx_small, w_out_b, l, h_small, fn, last, tm_small)

    y_prompt = h_main.reshape(bp, lp, D_MODEL)
    y_sample = h_small[:n_tile_rows].reshape(bs, TILE, D_MODEL)[:, TILE_OFF:TILE_OFF + ls]
    sd_p, ca_p, h_p, cb_p, sg_p, sr_p = p_st
    sd_s, ca_s, h_s, cb_s, sg_s, sr_s = s_st
    ca_s = ca_s.reshape(depth, bs, TILE, QKV_W)[:, :, :CONV_K - 1]
    cb_s = cb_s.reshape(depth, bs, TILE, GROUP_W)[:, :, :CONV_K - 1]
    return (y_prompt, y_sample,
            sd_p, ca_p, h_p.reshape(depth, bp, GROUP_W), cb_p, sg_p, sr_p,
            sd_s, ca_s, h_s.reshape(depth, bs, GROUP_W), cb_s, sg_s, sr_s)
```

```python
import functools
import itertools

import jax
import jax.numpy as jnp
from jax import lax
from jax.experimental import pallas as pl
from jax.experimental.pallas import tpu as pltpu

f32 = jnp.float32
bf16 = jnp.bfloat16

D_MODEL = 2048
N_META = 16
CONV_K = 4
CHUNK = 64
N_HEADS = 4
HEAD_V = 128
GROUP_W = N_HEADS * HEAD_V
DN_DK = 128
GLA_DK = 64
RET_DK = 64
GLA_RANK = 16
GLA_TAU = 16.0
LRU_C = 8.0
ROPE_BASE = 10000.0
EPS = 1e-6
PAST_LEN = 16384
QKV_W = 3 * N_HEADS * DN_DK

OFF_QKV = 0
OFF_XB = 1536
OFF_QC = 2048
OFF_KC = 2304
OFF_VC = 2560
OFF_QD = 3072
OFF_KD = 3328
OFF_VD = 3584
OFF_GATE = 4096
OFF_SMALL = 6144
N_PACK = 6400
SM_ALPHA = 0
SM_BETA = 4
SM_RC = 8

TILE = 8
TILE_OFF = CONV_K - 1
DEC_ROWS = 64
DEC_NG = 1
MAIN_NG = 4

VMEM_LIMIT = 52 * 1024 * 1024
TM_MAIN_IN = 1024
TN_IN = 1280
TM_MAIN_OUT = 1024
VMEM_LIMIT_OUTPROJ = 58 * 1024 * 1024
SCAN_STEPS_PER_SLOT = 16
GLA_PLAIN_RANGE = 60.0
GLA_ALWAYS_LEVELS_SEG = 16

_DONE = object()
NN = (((1,), (0,)), ((), ()))
NT = (((1,), (1,)), ((), ()))
TN = (((0,), (0,)), ((), ()))


def _split(a):
    hi = a.astype(bf16)
    lo = (a - hi.astype(f32)).astype(bf16)
    return hi, lo


P_SOLVE = 3


def _mm(a, b, dims=NN, passes=1):
    if passes == 6:
        return lax.dot_general(a, b, dims, precision=lax.Precision.HIGHEST, preferred_element_type=f32)
    if passes == 1:
        return lax.dot_general(a.astype(bf16), b.astype(bf16), dims, preferred_element_type=f32)
    ah, al = _split(a)
    bh, bl = _split(b)
    d = lambda x, y: lax.dot_general(x, y, dims, preferred_element_type=f32)
    return d(ah, bh) + (d(ah, bl) + d(al, bh))


def _mm_sel(sel, x, dims=NN, sel_first=True):
    sel = sel.astype(bf16)
    x1 = x.astype(bf16)
    r1 = x - x1.astype(f32)
    x2 = r1.astype(bf16)
    x3 = (r1 - x2.astype(f32)).astype(bf16)
    if sel_first:
        d = lambda y: lax.dot_general(sel, y, dims, preferred_element_type=f32)
    else:
        d = lambda y: lax.dot_general(y, sel, dims, preferred_element_type=f32)
    return d(x1) + (d(x2) + d(x3))


def _softplus(x):
    return jnp.maximum(x, 0.0) + jnp.log1p(jnp.exp(-jnp.abs(x)))


def _sigmoid(x):
    return 0.5 * jnp.tanh(0.5 * x) + 0.5


def _silu(x):
    return x * _sigmoid(x)


def _rms(x):
    return x * lax.rsqrt(jnp.mean(x * x, axis=-1, keepdims=True) + EPS)


def _rows(parts):
    return parts[0] if len(parts) == 1 else jnp.concatenate(parts, axis=0)


def _inproj_kernel(x_ref, nw_ref, w_ref, o_ref, xn_ref):
    @pl.when(pl.program_id(1) == 0)
    def _():
        xn_ref[...] = (_rms(x_ref[...]) * nw_ref[...]).astype(bf16)

    o_ref[...] = lax.dot_general(xn_ref[...], w_ref[...], NT, preferred_element_type=f32)


def _inproj(x, nw, w, layer, tm, tn):
    m = x.shape[0]
    return pl.pallas_call(
        _inproj_kernel,
        grid=(pl.cdiv(m, tm), N_PACK // tn),
        in_specs=[
            pl.BlockSpec((tm, D_MODEL), lambda i, j: (i, 0)),
            pl.BlockSpec((None, 1, D_MODEL), lambda i, j: (layer, 0, 0)),
            pl.BlockSpec((None, tn, D_MODEL), lambda i, j: (layer, j, 0)),
        ],
        out_specs=pl.BlockSpec((tm, tn), lambda i, j: (i, j)),
        out_shape=jax.ShapeDtypeStruct((m, N_PACK), f32),
        scratch_shapes=[pltpu.VMEM((tm, D_MODEL), bf16)],
        compiler_params=pltpu.CompilerParams(
            dimension_semantics=("arbitrary", "arbitrary"), vmem_limit_bytes=VMEM_LIMIT),
        name="inproj",
    )(x, nw, w)


def _outproj_kernel(final, m_ref, w_ref, x_ref, fn_ref, o_ref):
    y = x_ref[...] + jnp.dot(m_ref[...], w_ref[...], preferred_element_type=f32)
    if final:
        y = _rms(y) * fn_ref[...]
    o_ref[...] = y


def _outproj(mixed, w, layer, x, fn, final, tm):
    m = x.shape[0]
    return pl.pallas_call(
        functools.partial(_outproj_kernel, final),
        grid=(pl.cdiv(m, tm),),
        in_specs=[
            pl.BlockSpec((tm, D_MODEL), lambda i: (i, 0)),
            pl.BlockSpec((None, D_MODEL, D_MODEL), lambda i: (layer, 0, 0), pipeline_mode=pl.Buffered(1)),
            pl.BlockSpec((tm, D_MODEL), lambda i: (i, 0)),
            pl.BlockSpec((1, D_MODEL), lambda i: (0, 0)),
        ],
        out_specs=pl.BlockSpec((tm, D_MODEL), lambda i: (i, 0)),
        out_shape=jax.ShapeDtypeStruct((m, D_MODEL), f32),
        compiler_params=pltpu.CompilerParams(
            dimension_semantics=("arbitrary",), vmem_limit_bytes=VMEM_LIMIT_OUTPROJ),
        name="outproj",
    )(mixed, w, x, fn)


class _Cfg:
    def __init__(self, rows, seg, voff, vlen, nc, ng, gblocks, embedded, blk_off):
        self.rows, self.seg, self.voff, self.vlen = rows, seg, voff, vlen
        self.nc, self.ng, self.gblocks, self.embedded, self.blk_off = nc, ng, gblocks, embedded, blk_off
        self.nseg = rows // seg
        self.nstate = ng * self.nseg
        self.all_valid = (voff == 0 and vlen == seg)


def _tri_inverse(a_list, seg, row, col, eye):
    def blk(s):
        return (row // s) == (col // s)

    b8 = blk(8)
    n = [-jnp.where(b8, a, 0.0) for a in a_list]
    mm = _mm
    n2 = [mm(x, x) for x in n]
    yield
    n4 = [mm(x, x) for x in n2]
    t = [mm(eye + x, eye + y) for x, y in zip(n, n2)]
    yield
    t = [mm(x, eye + y) for x, y in zip(t, n4)]
    s = 8
    while s < seg:
        yield
        mask = blk(2 * s) & jnp.logical_not(blk(s))
        off = [jnp.where(mask, a, 0.0) for a in a_list]
        tb = [mm(x, o) for x, o in zip(t, off)]
        yield
        t = [x - mm(y, x) for x, y in zip(t, tb)]
        s *= 2
    yield
    resid = [eye - x - _mm(a, x, passes=P_SOLVE) for a, x in zip(a_list, t)]
    yield
    return [x + mm(x, r) for x, r in zip(t, resid)]


def _mixer_kernel(cfg, n_alias, *refs):
    R, SEG, NSEG, NG, NST = cfg.rows, cfg.seg, cfg.nseg, cfg.ng, cfg.nstate
    it = iter(refs)
    proj = next(it)
    if cfg.embedded:
        buf_a, buf_b = next(it), next(it)
    else:
        ic_a, ic_b = next(it), next(it)
    sd_in, h_in, sg_in, sr_in = next(it), next(it), next(it), next(it)
    cos_t, sin_t, intra_t, fs_t, ts_t, cd_t = (next(it) for _ in range(6))
    (conv_a, alogv, dtbv, norm_a, conv_b, cbb, wa, wx, ba, bx, lam, w2p, b2, norm_c) = (
        next(it) for _ in range(14))
    for _ in range(n_alias):
        next(it)
    mixed, sd, conv_a_o, h_o, conv_b_o, sg, sr = (next(it) for _ in range(7))
    xp_a, xp_b, l_a, l_b, l_o, gla_attn = (next(it) for _ in range(6))
    gla = {}

    c = pl.program_id(1)
    groups = range(NG)
    heads = range(N_HEADS)
    units = [(g, hd) for g in groups for hd in heads]
    nu = len(units)
    seg_rows = [slice(s * SEG, (s + 1) * SEG) for s in range(NSEG)]

    @pl.when(c == 0)
    def _init():
        shared = sd_in.shape[0] != NST
        for i in range(NST):
            j = 0 if shared else i
            sd[i] = sd_in[j]
            sg[i] = sg_in[j]
            sr[i] = sr_in[j]
        h_o[...] = jnp.broadcast_to(h_in[...], (NST, GROUP_W))
        for g in groups:
            if cfg.embedded:
                xp_a[g, 0:8, :] = jnp.zeros((8, QKV_W), f32)
                xp_b[g, 0:8, :] = jnp.zeros((8, GROUP_W), f32)
            else:
                xp_a[g, 0:8, :] = ic_a[...]
                xp_b[g, 0:8, :] = ic_b[...]
            xp_a[g, 8 + R:16 + R, :] = jnp.zeros((8, QKV_W), f32)
            xp_b[g, 8 + R:16 + R, :] = jnp.zeros((8, GROUP_W), f32)

    row = lax.broadcasted_iota(jnp.int32, (R, R), 0)
    col = lax.broadcasted_iota(jnp.int32, (R, R), 1)
    same = (row // SEG) == (col // SEG)
    incl = same & (col <= row)
    strict = same & (col < row)
    eye = (row == col).astype(f32)
    l_incl = incl.astype(f32)
    m_same = same.astype(f32)
    rmod = lax.broadcasted_iota(jnp.int32, (R, 1), 0) % SEG
    valid = (rmod >= cfg.voff) & (rmod < cfg.voff + cfg.vlen)
    is_hist = rmod < TILE_OFF

    def conv(xp, g, x, buf, w_ref):
        if cfg.embedded:
            x = jnp.where(is_hist, buf[g], x)
        xp[g, 8:8 + R, :] = x
        full = xp[g, 0:8 + R, :]
        y = w_ref[CONV_K - 1:CONV_K, :] * x
        for s in range(1, CONV_K):
            y = y + w_ref[CONV_K - 1 - s:CONV_K - s, :] * pltpu.roll(full, s, 0)[8:8 + R]
        return y

    def conv_with_history(xp, x_of, buf, w_ref, hist_out):
        ys = [conv(xp, g, x_of(g), buf, w_ref) for g in groups]
        for g in groups:
            if cfg.embedded:
                hist_out[g] = xp[g, 8 + TILE_OFF + 1:8 + TILE_OFF + 1 + R, :]
            else:
                xp[g, 0:8, :] = xp[g, R:R + 8, :]
        return ys

    small = [proj[g, :, OFF_SMALL:OFF_SMALL + 128] for g in groups]

    def put(g, k, o):
        gate = proj[g, :, OFF_GATE + k * GROUP_W:OFF_GATE + (k + 1) * GROUP_W]
        mixed[g, :, k * GROUP_W:(k + 1) * GROUP_W] = (o * _silu(gate)).astype(mixed.dtype)


    def lru_stream():
        yb = conv_with_history(xp_b, lambda g: proj[g, :, OFF_XB:OFF_XB + GROUP_W],
                               buf_b if cfg.embedded else None, conv_b, conv_b_o)
        yb = [y + cbb[...] for y in yb]
        sp_lam = _softplus(-lam[...])
        r_pre = [[_mm(yb[g][:, n * 128:(n + 1) * 128], wa[n]) for n in heads] for g in groups]
        i_pre = [[_mm(yb[g][:, n * 128:(n + 1) * 128], wx[n]) for n in heads] for g in groups]
        h_all = h_o[...]
        h_cur = {}
        for g, n in units:
            ls = slice(n * 128, (n + 1) * 128)
            x_n = yb[g][:, ls]
            log_a = -LRU_C * _sigmoid(r_pre[g][n] + ba[:, ls]) * sp_lam[:, ls]
            a_t = jnp.exp(log_a)
            b_t = jnp.sqrt(-jnp.tanh(log_a) * (a_t * a_t + 1.0)) * (_sigmoid(i_pre[g][n] + bx[:, ls]) * x_n)
            l_a[g * N_HEADS + n] = a_t
            l_b[g * N_HEADS + n] = b_t
            if not cfg.all_valid:
                l_o[g * N_HEADS + n] = jnp.zeros((R, 128), f32)
            h_cur[g, n] = h_all[g * NSEG:(g + 1) * NSEG, ls]
        yield
        for t in range(cfg.vlen):
            idx = pl.ds(cfg.voff + t, 1) if NSEG == 1 else pl.ds(cfg.voff + t, NSEG, stride=SEG)
            for g, n in units:
                k = g * N_HEADS + n
                h_cur[g, n] = l_a[k, idx, :] * h_cur[g, n] + l_b[k, idx, :]
                l_o[k, idx, :] = h_cur[g, n]
            if t % SCAN_STEPS_PER_SLOT == SCAN_STEPS_PER_SLOT - 1:
                yield
        h_o[...] = _rows([jnp.concatenate([h_cur[g, n] for n in heads], axis=1) for g in groups])
        for g in groups:
            put(g, 1, jnp.concatenate([l_o[g * N_HEADS + n] for n in heads], axis=1))

    def delta_stream():
        ya = conv_with_history(xp_a, lambda g: proj[g, :, OFF_QKV:OFF_QKV + QKV_W],
                               buf_a if cfg.embedded else None, conv_a, conv_a_o)
        qkv = [_silu(y) for y in ya]
        g_all, beta_all = [], []
        for g in groups:
            ga = -jnp.exp(alogv[...]) * _softplus(small[g] + dtbv[...])
            be = _sigmoid(small[g])
            if not cfg.all_valid:
                ga = jnp.where(valid, ga, 0.0)
                be = jnp.where(valid, be, 0.0)
            g_all.append(ga)
            beta_all.append(be)
        gcum = [_mm_sel(l_incl, x) for x in g_all]
        gtot = [_mm_sel(m_same, x) for x in g_all]
        gcum_t = [x.T for x in gcum]

        q_l, k_l, v_l, be_l, gc_l, gt_l, dec_l, eg_l = ([] for _ in range(8))
        for g, hd in units:
            q = qkv[g][:, hd * 128:(hd + 1) * 128]
            k = qkv[g][:, 512 + hd * 128:512 + (hd + 1) * 128]
            q_l.append(q * lax.rsqrt(jnp.sum(q * q, axis=-1, keepdims=True) + EPS) * (DN_DK ** -0.5))
            k_l.append(k * lax.rsqrt(jnp.sum(k * k, axis=-1, keepdims=True) + EPS))
            v_l.append(qkv[g][:, 1024 + hd * 128:1024 + (hd + 1) * 128])
            be_l.append(beta_all[g][:, SM_BETA + hd:SM_BETA + hd + 1])
            gc = gcum[g][:, SM_ALPHA + hd:SM_ALPHA + hd + 1]
            gr = gcum_t[g][SM_ALPHA + hd:SM_ALPHA + hd + 1, :]
            gc_l.append(gc)
            gt_l.append(gtot[g][:, SM_ALPHA + hd:SM_ALPHA + hd + 1])
            dec_l.append(jnp.where(incl, jnp.exp(jnp.where(incl, gc - gr, 0.0)), 0.0))
            eg_l.append(jnp.exp(gc))
        kb_l = [k_l[u] * be_l[u] for u in range(nu)]
        a_l = [jnp.where(strict, _mm(kb_l[u], k_l[u], NT) * dec_l[u], 0.0) for u in range(nu)]
        t_l = yield from _tri_inverse(a_l, SEG, row, col, eye)
        attn_l = [jnp.where(incl, _mm(q_l[u], k_l[u], NT) * dec_l[u], 0.0) for u in range(nu)]
        uw_l = [_mm(t_l[u], jnp.concatenate([v_l[u] * be_l[u], kb_l[u] * eg_l[u]], axis=1), passes=P_SOLVE)
                for u in range(nu)]
        st_l = []
        for u, (g, hd) in enumerate(units):
            qe = q_l[u] * eg_l[u]
            w = uw_l[u][:, 128:]
            st_l.append([_mm(jnp.concatenate([qe[rs], w[rs]], axis=0), sd[g * NSEG + s, hd])
                         for s, rs in enumerate(seg_rows)])
        vn_l = [uw_l[u][:, :128] - _rows([b[SEG:] for b in st_l[u]]) for u in range(nu)]
        o_l = [_rows([b[:SEG] for b in st_l[u]]) + _mm(attn_l[u], vn_l[u]) for u in range(nu)]
        for u, (g, hd) in enumerate(units):
            kd = k_l[u] * jnp.exp(gt_l[u] - gc_l[u])
            for s, rs in enumerate(seg_rows):
                g_last = jnp.exp(gt_l[u][s * SEG:s * SEG + 1, :])
                i = g * NSEG + s
                sd[i, hd] = sd[i, hd] * g_last + _mm(kd[rs], vn_l[u][rs], TN)
        for g in groups:
            put(g, 0, jnp.concatenate(
                [_rms(o_l[g * N_HEADS + hd]) * norm_a[:, hd * 128:(hd + 1) * 128] for hd in heads], axis=1))

    def gla_stream():
        lg = []
        for g in groups:
            x_gate = _mm(small[g], w2p[...]) + b2[...]
            z = -_softplus(-x_gate) * (1.0 / GLA_TAU)
            lg.append(z if cfg.all_valid else jnp.where(valid, z, 0.0))
        bcum = [_mm_sel(l_incl, x) for x in lg]
        btot = [_mm_sel(m_same, x) for x in lg]
        ones = jnp.ones((SEG, 128), f32)
        dec_s = [[jnp.exp(_mm_sel(ones, lg[g][rs], TN, sel_first=False)) for rs in seg_rows]
                 for g in groups]
        q_cs, k_cs, qe_c, kd_c = [], [], [], []
        for g in groups:
            q_c = proj[g, :, OFF_QC:OFF_QC + 256] * (GLA_DK ** -0.5)
            k_c = proj[g, :, OFF_KC:OFF_KC + 256]
            if not cfg.all_valid:
                k_c = jnp.where(valid, k_c, 0.0)
            q_cs.append(q_c)
            k_cs.append(k_c)
            qe_c.append(q_c * jnp.exp(bcum[g]))
            kd_c.append(k_c * jnp.exp(btot[g] - bcum[g]))
        yield
        ks_l = [slice(hd * GLA_DK, (hd + 1) * GLA_DK) for hd in heads]
        vc_l = [proj[g, :, OFF_VC + hd * 128:OFF_VC + (hd + 1) * 128] for g, hd in units]
        o_state = []
        for u, (g, hd) in enumerate(units):
            parts = []
            for s, rs in enumerate(seg_rows):
                i = g * NSEG + s
                parts.append(_mm(qe_c[g][rs, ks_l[hd]], sg[i, hd]))
                sg[i, hd] = sg[i, hd] * dec_s[g][s][ks_l[hd], :] + _mm(kd_c[g][rs, ks_l[hd]], vc_l[u][rs], TN)
            o_state.append(_rows(parts))
        gla.update(lg=lg, bcum=bcum, btot=btot, q=q_cs, k=k_cs, qe=qe_c, ks=ks_l, v=vc_l, o_state=o_state)
        if not branch_on_range:
            yield
            yield from gla_scores_levels()
            yield
            gla_finish()

    def gla_scores_plain():
        for u, (g, hd) in enumerate(units):
            ks = gla["ks"][hd]
            ke = gla["k"][g][:, ks] * jnp.exp(-gla["bcum"][g][:, ks])
            gla_attn[u] = jnp.where(incl, _mm(gla["qe"][g][:, ks], ke, NT), 0.0)

    def gla_scores_levels():
        halves = [SEG >> (i + 1) for i in range(SEG.bit_length() - 1)]
        nlev = len(halves)
        between = []
        for h in halves:
            ref = (row // (2 * h)) * (2 * h) + h - 1
            between.append(((col > ref) & (col <= row)) | ((col > row) & (col <= ref)))
        between = jnp.concatenate(between, axis=0).astype(f32)
        pair = [((row // (2 * h)) == (col // (2 * h))) & ((row % (2 * h)) >= h) & ((col % (2 * h)) < h)
                for h in halves]
        rep = lambda x: jnp.concatenate([x] * nlev, axis=0)
        e_lv = [jnp.exp(_mm_sel(between, gla["lg"][g])) for g in groups]
        q_lv = [rep(gla["q"][g]) * e_lv[g] for g in groups]
        k_lv = [rep(gla["k"][g]) * e_lv[g] for g in groups]
        yield
        ks_l = gla["ks"]
        attn_l = [jnp.where(eye > 0.0, _mm(gla["q"][g][:, ks_l[hd]], gla["k"][g][:, ks_l[hd]], NT), 0.0)
                  for g, hd in units]
        for li in range(nlev):
            lv = slice(li * R, (li + 1) * R)
            attn_l = [jnp.where(pair[li], _mm(q_lv[g][lv, ks_l[hd]], k_lv[g][lv, ks_l[hd]], NT), attn_l[u])
                      for u, (g, hd) in enumerate(units)]
            if li % 2 == 1:
                yield
        for u in range(nu):
            gla_attn[u] = attn_l[u]

    branch_on_range = SEG > GLA_ALWAYS_LEVELS_SEG

    def gla_finish():
        if branch_on_range:
            worst = jnp.max(-gla["btot"][0])
            for g in list(groups)[1:]:
                worst = jnp.maximum(worst, jnp.max(-gla["btot"][g]))
            plain_ok = worst < GLA_PLAIN_RANGE
            pl.when(plain_ok)(gla_scores_plain)

            @pl.when(jnp.logical_not(plain_ok))
            def _():
                for _stage in gla_scores_levels():
                    pass
        o_l = [_mm(gla_attn[u], gla["v"][u]) + gla["o_state"][u] for u in range(nu)]
        for g in groups:
            put(g, 2, jnp.concatenate(
                [_rms(o_l[g * N_HEADS + hd]) * norm_c[:, hd * 128:(hd + 1) * 128] for hd in heads], axis=1))

    def ret_stream():
        half = RET_DK // 2
        lane = lax.broadcasted_iota(jnp.int32, (R, 256), 1)
        first_half = (lane % RET_DK) < half

        def rotary(x):
            rot = jnp.where(first_half, pltpu.roll(x, 256 - half, 1), pltpu.roll(x, half, 1))
            return x * cos_t[...] + rot * sin_t[...]

        q_d = [rotary(proj[g, :, OFF_QD:OFF_QD + 256]) for g in groups]
        k_d = [rotary(proj[g, :, OFF_KD:OFF_KD + 256]) * (RET_DK ** -0.5) for g in groups]
        qf = [x * fs_t[...] for x in q_d]
        kt = [x * ts_t[...] for x in k_d]
        yield
        ks_l = [slice(hd * RET_DK, (hd + 1) * RET_DK) for hd in heads]
        vd_l = [proj[g, :, OFF_VD + hd * 128:OFF_VD + (hd + 1) * 128] for g, hd in units]
        attn_l = [_mm(q_d[g][:, ks_l[hd]], k_d[g][:, ks_l[hd]], NT) * intra_t[hd] for g, hd in units]
        o_l = [_mm(attn_l[u], vd_l[u]) for u in range(nu)]
        for u, (g, hd) in enumerate(units):
            parts = []
            for s, rs in enumerate(seg_rows):
                i = g * NSEG + s
                parts.append(_mm(qf[g][rs, ks_l[hd]], sr[i, hd]))
                sr[i, hd] = sr[i, hd] * cd_t[hd, 0:1, :] + _mm(kt[g][rs, ks_l[hd]], vd_l[u][rs], TN)
            o_l[u] = o_l[u] + _rows(parts)
        for g in groups:
            put(g, 3, jnp.concatenate([_rms(o_l[g * N_HEADS + hd]) for hd in heads], axis=1))

    others = itertools.chain(lru_stream(), gla_stream(), ret_stream())
    streams = [delta_stream(), others]
    while streams:
        for stream in list(streams):
            if next(stream, _DONE) is _DONE:
                streams.remove(stream)
    if branch_on_range:
        gla_finish()

    if not cfg.embedded:
        @pl.when(c == cfg.nc - 1)
        def _():
            for g in groups:
                conv_a_o[g] = xp_a[g, 5:8, :]
                conv_b_o[g] = xp_b[g, 5:8, :]


def _mixer(cfg, layer, proj, hist, states, lin, tables, weights, n_out_layers, lout, prev):
    R, NG, GB, NC, NST = cfg.rows, cfg.ng, cfg.gblocks, cfg.nc, cfg.nstate
    sd_in, h_in, sg_in, sr_in = states
    shared = sd_in.shape[1] != GB * NST
    nin = 1 if shared else NST

    def st(i):
        return 0 if shared else i

    if cfg.embedded:
        hist_specs = [pl.BlockSpec((None, NG, R, QKV_W), lambda gb, c: (layer, gb, c, 0)),
                      pl.BlockSpec((None, NG, R, GROUP_W), lambda gb, c: (layer, gb, c, 0))]
    else:
        hist_specs = [pl.BlockSpec((8, QKV_W), lambda gb, c: (0, 0)),
                      pl.BlockSpec((8, GROUP_W), lambda gb, c: (0, 0))]
    in_specs = [pl.BlockSpec((NG, R, N_PACK), lambda gb, c: (cfg.blk_off + gb, c, 0))] + hist_specs + [
        pl.BlockSpec((None, nin, N_HEADS, DN_DK, HEAD_V), lambda gb, c: (lin, st(gb), 0, 0, 0)),
        pl.BlockSpec((None, None, nin, GROUP_W), lambda gb, c: (lin, st(gb), 0, 0)),
        pl.BlockSpec((None, nin, N_HEADS, GLA_DK, HEAD_V), lambda gb, c: (lin, st(gb), 0, 0, 0)),
        pl.BlockSpec((None, nin, N_HEADS, RET_DK, HEAD_V), lambda gb, c: (lin, st(gb), 0, 0, 0)),
        pl.BlockSpec((R, 256), lambda gb, c: (c, 0)),
        pl.BlockSpec((R, 256), lambda gb, c: (c, 0)),
        pl.BlockSpec((N_HEADS, R, R), lambda gb, c: (0, 0, 0)),
        pl.BlockSpec((R, 256), lambda gb, c: (0, 0)),
        pl.BlockSpec((R, 256), lambda gb, c: (0, 0)),
        pl.BlockSpec((N_HEADS, 8, 128), lambda gb, c: (0, 0, 0)),
    ]
    for w in weights:
        in_specs.append(pl.BlockSpec((None,) + w.shape[1:], lambda gb, c, nd=w.ndim: (layer,) + (0,) * (nd - 1)))
    aliases = {}
    if prev is not None:
        for k, p in enumerate(prev):
            aliases[len(in_specs)] = 1 + k
            in_specs.append(pl.BlockSpec(memory_space=pl.ANY))

    n_seq_rows = NC * R
    conv_rows = n_seq_rows if cfg.embedded else CONV_K - 1
    nl = n_out_layers
    out_shape = [
        jax.ShapeDtypeStruct((GB * NG, n_seq_rows, D_MODEL), bf16),
        jax.ShapeDtypeStruct((nl, GB * NST, N_HEADS, DN_DK, HEAD_V), f32),
        jax.ShapeDtypeStruct((nl, GB * NG, conv_rows, QKV_W), f32),
        jax.ShapeDtypeStruct((nl, GB, NST, GROUP_W), f32),
        jax.ShapeDtypeStruct((nl, GB * NG, conv_rows, GROUP_W), f32),
        jax.ShapeDtypeStruct((nl, GB * NST, N_HEADS, GLA_DK, HEAD_V), f32),
        jax.ShapeDtypeStruct((nl, GB * NST, N_HEADS, RET_DK, HEAD_V), f32),
    ]
    if cfg.embedded:
        conv_specs = [pl.BlockSpec((None, NG, R, QKV_W), lambda gb, c: (lout, gb, c, 0)),
                      pl.BlockSpec((None, NG, R, GROUP_W), lambda gb, c: (lout, gb, c, 0))]
    else:
        conv_specs = [pl.BlockSpec((None, NG, CONV_K - 1, QKV_W), lambda gb, c: (lout, gb, 0, 0)),
                      pl.BlockSpec((None, NG, CONV_K - 1, GROUP_W), lambda gb, c: (lout, gb, 0, 0))]
    out_specs = [
        pl.BlockSpec((NG, R, D_MODEL), lambda gb, c: (gb, c, 0)),
        pl.BlockSpec((None, NST, N_HEADS, DN_DK, HEAD_V), lambda gb, c: (lout, gb, 0, 0, 0)),
        conv_specs[0],
        pl.BlockSpec((None, None, NST, GROUP_W), lambda gb, c: (lout, gb, 0, 0)),
        conv_specs[1],
        pl.BlockSpec((None, NST, N_HEADS, GLA_DK, HEAD_V), lambda gb, c: (lout, gb, 0, 0, 0)),
        pl.BlockSpec((None, NST, N_HEADS, RET_DK, HEAD_V), lambda gb, c: (lout, gb, 0, 0, 0)),
    ]
    scratch = [
        pltpu.VMEM((NG, R + 16, QKV_W), f32),
        pltpu.VMEM((NG, R + 16, GROUP_W), f32),
        pltpu.VMEM((NG * N_HEADS, R, 128), f32),
        pltpu.VMEM((NG * N_HEADS, R, 128), f32),
        pltpu.VMEM((NG * N_HEADS, R, 128), f32),
        pltpu.VMEM((NG * N_HEADS, R, R), f32),
    ]
    return pl.pallas_call(
        functools.partial(_mixer_kernel, cfg, len(aliases)),
        grid=(GB, NC),
        in_specs=in_specs,
        out_specs=out_specs,
        out_shape=out_shape,
        scratch_shapes=scratch,
        input_output_aliases=aliases,
        compiler_params=pltpu.CompilerParams(
            dimension_semantics=("arbitrary", "arbitrary"), vmem_limit_bytes=VMEM_LIMIT),
        name="mixer_r%d_s%d_g%d" % (R, cfg.seg, NG),
    )(proj, *hist, sd_in, h_in, sg_in, sr_in, *tables, *weights, *(prev or ()))


def _rope_tables(pos):
    half = RET_DK // 2
    freqs = ROPE_BASE ** (-jnp.arange(half, dtype=f32) / half)
    ang = pos.astype(f32)[:, None] * freqs
    cos, sin = jnp.cos(ang), jnp.sin(ang)
    cos_h = jnp.concatenate([cos, cos], axis=1)
    sin_h = jnp.concatenate([-sin, sin], axis=1)
    return jnp.tile(cos_h, (1, N_HEADS)), jnp.tile(sin_h, (1, N_HEADS))


def _ret_tables(rows, seg, voff, vlen):
    log_gamma = jnp.log(1.0 - 2.0 ** (-5.0 - jnp.arange(N_HEADS, dtype=f32)))
    r = jnp.arange(rows)
    p = (r % seg - voff).astype(f32)
    ok = ((r % seg) >= voff) & ((r % seg) < voff + vlen)
    rel = p[:, None] - p[None, :]
    pair = ok[:, None] & ok[None, :] & ((r[:, None] // seg) == (r[None, :] // seg)) & (rel >= 0)
    intra = jnp.where(pair[None], jnp.exp(log_gamma[:, None, None] * jnp.maximum(rel, 0.0)[None]), 0.0)
    from_state = jnp.exp(log_gamma[:, None] * (p + 1.0))
    to_state = jnp.where(ok[None], jnp.exp(log_gamma[:, None] * (vlen - 1.0 - p)), 0.0)
    chunk_decay = jnp.exp(log_gamma * vlen)
    fs = jnp.repeat(from_state.T, RET_DK, axis=1)
    ts = jnp.repeat(to_state.T, RET_DK, axis=1)
    cd = jnp.broadcast_to(chunk_decay[:, None, None], (N_HEADS, 8, 128))
    return intra.astype(f32), fs.astype(f32), ts.astype(f32), cd.astype(f32)


SRC_AB = QKV_W
SRC_RUN1 = SRC_AB + 2 * N_HEADS
SRC_RC = SRC_RUN1 + (OFF_QD - OFF_XB)
SRC_RUN2 = SRC_RC + GLA_RANK
SRC_W = SRC_RUN2 + (OFF_SMALL - OFF_QD)
PACK_ROWS = 256
N_WIDE_BLOCKS = OFF_SMALL // PACK_ROWS


def _pack_kernel(src_ref, ab_ref, rc_ref, o_ref):
    i = pl.program_id(1)

    @pl.when(i < N_WIDE_BLOCKS)
    def _():
        o_ref[...] = src_ref[0].astype(bf16)

    @pl.when(i == N_WIDE_BLOCKS)
    def _():
        pad = jnp.zeros((N_PACK - OFF_SMALL - 2 * N_HEADS - GLA_RANK, D_MODEL), f32)
        o_ref[...] = jnp.concatenate([ab_ref[0], rc_ref[0], pad], axis=0).astype(bf16)


def _pack_src_row(i):
    shift = jnp.where(i < OFF_XB // PACK_ROWS, 0,
                      jnp.where(i < OFF_QD // PACK_ROWS, SRC_RUN1 - OFF_XB, SRC_RUN2 - OFF_QD))
    return pl.multiple_of(jnp.minimum(i * PACK_ROWS + shift, SRC_W - PACK_ROWS), 8)


def _pack_w_in(w_in_t):
    depth, n, d = w_in_t.shape
    assert n == SRC_W and d == D_MODEL and OFF_XB % PACK_ROWS == 0 and OFF_QD % PACK_ROWS == 0
    assert N_PACK - OFF_SMALL == PACK_ROWS
    return pl.pallas_call(
        _pack_kernel,
        grid=(depth, N_PACK // PACK_ROWS),
        in_specs=[
            pl.BlockSpec((pl.Element(1), pl.Element(PACK_ROWS), pl.Element(d)),
                         lambda l, i: (l, _pack_src_row(i), 0)),
            pl.BlockSpec((pl.Element(1), pl.Element(2 * N_HEADS), pl.Element(d)), lambda l, i: (l, SRC_AB, 0)),
            pl.BlockSpec((pl.Element(1), pl.Element(GLA_RANK), pl.Element(d)), lambda l, i: (l, SRC_RC, 0)),
        ],
        out_specs=pl.BlockSpec((None, PACK_ROWS, d), lambda l, i: (l, i, 0)),
        out_shape=jax.ShapeDtypeStruct((depth, N_PACK, d), bf16),
        compiler_params=pltpu.CompilerParams(dimension_semantics=("arbitrary", "arbitrary")),
        name="pack_w_in",
    )(w_in_t, w_in_t, w_in_t)


def _lanes(vec, off, width=128):
    return jnp.pad(vec[None, :], ((0, 0), (off, width - off - vec.shape[0])))


def kernel(x_prompt, x_sample, state_delta, state_delta_conv, state_lru, state_lru_conv, state_gla,
           state_ret, meta_tokens, norm_w, w_in, conv_a, a_log, dt_bias, norm_a, conv_b, conv_b_bias,
           lru_wa, lru_ba, lru_wx, lru_bx, lru_lambda, gla_w2, gla_b2, norm_c, w_out, final_norm):
    depth = w_in.shape[0]
    bp, lp = x_prompt.shape[0], x_prompt.shape[1]
    bs, ls = x_sample.shape[0], x_sample.shape[1]
    assert lp % CHUNK == 0 and ls == CONV_K and TILE_OFF + ls <= TILE
    nc_main = lp // CHUNK
    n_tile_rows = bs * TILE
    assert n_tile_rows % DEC_ROWS == 0 and DEC_ROWS % N_META == 0
    n_dec_blocks = n_tile_rows // DEC_ROWS
    seq_per_block = DEC_ROWS // TILE

    w_in_p = _pack_w_in(jnp.swapaxes(w_in, 1, 2))
    w_out_b = w_out.astype(bf16)

    h_main = x_prompt.reshape(bp * lp, D_MODEL)
    tiles = jnp.pad(x_sample, ((0, 0), (TILE_OFF, TILE - TILE_OFF - ls), (0, 0)))
    h_small = jnp.concatenate(
        [tiles.reshape(n_tile_rows, D_MODEL), meta_tokens.astype(x_prompt.dtype),
         jnp.zeros((DEC_ROWS - N_META, D_MODEL), x_prompt.dtype)], axis=0)
    n_small = h_small.shape[0]
    tm_small = n_small // 2
    assert tm_small % 8 == 0

    cfg_main = _Cfg(CHUNK, CHUNK, 0, CHUNK, nc_main, MAIN_NG, bp // MAIN_NG, False, 0)
    cfg_meta = _Cfg(N_META, N_META, 0, N_META, 1, 1, 1, False, n_tile_rows // N_META)
    assert n_dec_blocks % DEC_NG == 0
    cfg_dec = _Cfg(DEC_ROWS, TILE, TILE_OFF, ls, 1, DEC_NG, n_dec_blocks // DEC_NG, True, 0)

    pos_main = N_META + jnp.arange(lp)
    pos_meta = jnp.arange(N_META)
    pos_dec = jnp.tile(PAST_LEN + jnp.arange(TILE) - TILE_OFF, seq_per_block)
    tab_main = _rope_tables(pos_main) + _ret_tables(CHUNK, CHUNK, 0, CHUNK)
    tab_meta = _rope_tables(pos_meta) + _ret_tables(N_META, N_META, 0, N_META)
    tab_dec = _rope_tables(pos_dec) + _ret_tables(DEC_ROWS, TILE, TILE_OFF, ls)

    zeros_meta = (
        jnp.zeros((1, 1, N_HEADS, DN_DK, HEAD_V), f32), jnp.zeros((1, 1, 1, GROUP_W), f32),
        jnp.zeros((1, 1, N_HEADS, GLA_DK, HEAD_V), f32), jnp.zeros((1, 1, N_HEADS, RET_DK, HEAD_V), f32))
    zero_hist = (jnp.zeros((8, QKV_W), f32), jnp.zeros((8, GROUP_W), f32))

    weights = (
        conv_a,
        jnp.pad(a_log[:, None, :], ((0, 0), (0, 0), (SM_ALPHA, 128 - SM_ALPHA - N_HEADS))),
        jnp.pad(dt_bias[:, None, :], ((0, 0), (0, 0), (SM_ALPHA, 128 - SM_ALPHA - N_HEADS))),
        jnp.tile(norm_a, (1, N_HEADS))[:, None, :], conv_b, conv_b_bias[:, None, :],
        lru_wa, lru_wx, lru_ba[:, None, :], lru_bx[:, None, :], lru_lambda[:, None, :],
        jnp.pad(gla_w2, ((0, 0), (SM_RC, 128 - SM_RC - GLA_RANK), (0, 0))), gla_b2[:, None, :],
        jnp.tile(norm_c, (1, N_HEADS))[:, None, :],
    )
    nw = norm_w[:, None, :]
    fn = final_norm[None]

    pad_tile = ((0, 0), (0, 0), (0, TILE - (CONV_K - 1)), (0, 0))
    hist_dec = (jnp.pad(state_delta_conv, pad_tile).reshape(depth, n_dec_blocks, DEC_ROWS, QKV_W),
                jnp.pad(state_lru_conv, pad_tile).reshape(depth, n_dec_blocks, DEC_ROWS, GROUP_W))
    st_dec = (state_delta, state_lru.reshape(depth, cfg_dec.gblocks, cfg_dec.nstate, GROUP_W), state_gla, state_ret)

    p_st, s_st = None, None
    for l in range(depth):
        last = l == depth - 1
        proj_main = _inproj(h_main, nw, w_in_p, l, TM_MAIN_IN, TN_IN)
        proj_small = _inproj(h_small, nw, w_in_p, l, n_small, TN_IN)

        mx_meta, sd_m, ca_m, h_m, cb_m, sg_m, sr_m = _mixer(
            cfg_meta, l, proj_small.reshape(n_small // N_META, N_META, N_PACK), zero_hist, zeros_meta, 0,
            tab_meta, weights, 1, 0, None)
        hist_main = (jnp.pad(ca_m[0, 0], ((8 - (CONV_K - 1), 0), (0, 0))),
                     jnp.pad(cb_m[0, 0], ((8 - (CONV_K - 1), 0), (0, 0))))
        mx_main, *p_st = _mixer(
            cfg_main, l, proj_main.reshape(bp, lp, N_PACK), hist_main, (sd_m, h_m, sg_m, sr_m), 0,
            tab_main, weights, depth, l, p_st)

        mx_dec, *s_st = _mixer(
            cfg_dec, l, proj_small.reshape(n_small // DEC_ROWS, DEC_ROWS, N_PACK), hist_dec, st_dec, l,
            tab_dec, weights, depth, l, s_st)

        h_main = _outproj(mx_main.reshape(bp * lp, D_MODEL), w_out_b, l, h_main, fn, last, TM_MAIN_OUT)
        mx_small = jnp.concatenate(
            [mx_dec.reshape(n_tile_rows, D_MODEL), mx_meta[0],
             jnp.zeros((DEC_ROWS - N_META, D_MODEL), bf16)], axis=0)
        h_small = _outproj(mx_small, w_out_b, l, h_small, fn, last, tm_small)

    y_prompt = h_main.reshape(bp, lp, D_MODEL)
    y_sample = h_small[:n_tile_rows].reshape(bs, TILE, D_MODEL)[:, TILE_OFF:TILE_OFF + ls]
    sd_p, ca_p, h_p, cb_p, sg_p, sr_p = p_st
    sd_s, ca_s, h_s, cb_s, sg_s, sr_s = s_st
    ca_s = ca_s.reshape(depth, bs, TILE, QKV_W)[:, :, :CONV_K - 1]
    cb_s = cb_s.reshape(depth, bs, TILE, GROUP_W)[:, :, :CONV_K - 1]
    return (y_prompt, y_sample,
            sd_p, ca_p, h_p.reshape(depth, bp, GROUP_W), cb_p, sg_p, sr_p,
            sd_s, ca_s, h_s.reshape(depth, bs, GROUP_W), cb_s, sg_s, sr_s)
```
